```python
import jax, jax.numpy as jnp
from jax import lax
import numpy as np

D_MODEL = 1024
BATCH = 4
SEQ = 4096
DEPTH = 1

N_MEM = 256
CONV_WIDTH = D_MODEL
CONV_K = 3
POOL_WINDOWS = (2, 4, 8, 16)
POOL_GROUPS = len(POOL_WINDOWS)
POOL_WIDTH = D_MODEL
POOL_GROUP_DIM = POOL_WIDTH // POOL_GROUPS
X_HEADS = 4
X_HEAD_DIM = D_MODEL // X_HEADS
X_WIDTH = X_HEADS * X_HEAD_DIM
N_BRANCH = 3
D_FF = ((8 * D_MODEL // 3 + 255) // 256) * 256
EPS = 1e-6
IN_SPLITS = (CONV_WIDTH, CONV_WIDTH, CONV_WIDTH, POOL_WIDTH, X_WIDTH, D_MODEL, D_MODEL, D_MODEL)
D_IN = sum(IN_SPLITS)

kernel_name = "hybrid_gated_conv_pool_memxattn_block"


def rms_norm(x, g):
    xf = x.astype(jnp.float32)
    y = xf * lax.rsqrt(jnp.mean(xf * xf, axis=-1, keepdims=True) + EPS)
    return (y * g.astype(jnp.float32)).astype(x.dtype)


def causal_depthwise_conv(u, w):
    c = u.shape[-1]
    return lax.conv_general_dilated(
        u, w[:, None, :].astype(u.dtype), window_strides=(1,), padding=[(CONV_K - 1, 0)],
        dimension_numbers=("NWC", "WIO", "NWC"), feature_group_count=c)


def multiscale_causal_pool(u):
    b, s, _ = u.shape
    uf = u.astype(jnp.float32).reshape(b, s, POOL_GROUPS, POOL_GROUP_DIM)
    csum = jnp.cumsum(uf, axis=1)
    pos = jnp.arange(1, s + 1, dtype=jnp.int32)
    outs = []
    for g, w in enumerate(POOL_WINDOWS):
        cg = csum[:, :, g]
        lower = jnp.pad(cg, ((0, 0), (w, 0), (0, 0)))[:, :s]
        cnt = jnp.minimum(pos, w).astype(jnp.float32)[None, :, None]
        outs.append((cg - lower) / cnt - uf[:, :, g])
    return jnp.stack(outs, axis=2).astype(u.dtype)


def memory_cross_attention(q, mem_n, w_kv):
    b, s, _ = q.shape
    m = mem_n.shape[1]
    qh = q.reshape(b, s, X_HEADS, X_HEAD_DIM)
    kv = mem_n @ w_kv
    k, v = jnp.split(kv, 2, axis=-1)
    k = k.reshape(b, m, X_HEADS, X_HEAD_DIM)
    v = v.reshape(b, m, X_HEADS, X_HEAD_DIM)
    scores = jnp.einsum("bshd,bmhd->bhsm", qh, k).astype(jnp.float32) * (X_HEAD_DIM ** -0.5)
    probs = jax.nn.softmax(scores, axis=-1).astype(v.dtype)
    o = jnp.einsum("bhsm,bmhd->bshd", probs, v)
    return o.reshape(b, s, X_WIDTH)


def setup_inputs(seed: int = 0) -> dict:
    key = jax.random.key(seed)
    ks = jax.random.split(key, 20)
    f32 = jnp.float32
    L = DEPTH

    def nrm(k, shape, fan_in):
        return jax.random.normal(k, shape, f32) * (fan_in ** -0.5)

    def gain(k, shape):
        return 1.0 + 0.05 * jax.random.normal(k, shape, f32)

    return {
        "x": jax.random.normal(ks[0], (BATCH, SEQ, D_MODEL), f32),
        "mem": jax.random.normal(ks[1], (BATCH, N_MEM, D_MODEL), f32),
        "norm_mix": gain(ks[2], (L, D_MODEL)),
        "w_in": nrm(ks[3], (L, D_MODEL, D_IN), D_MODEL),
        "conv_w": nrm(ks[4], (L, CONV_K, CONV_WIDTH), CONV_K),
        "w_conv_out": nrm(ks[5], (L, CONV_WIDTH, D_MODEL), CONV_WIDTH),
        "w_pool": nrm(ks[6], (L, POOL_GROUPS, POOL_GROUP_DIM, POOL_GROUP_DIM), POOL_GROUP_DIM),
        "pool_scale": gain(ks[7], (L, POOL_WIDTH)),
        "norm_mem": gain(ks[8], (L, D_MODEL)),
        "w_kv": nrm(ks[9], (L, D_MODEL, 2 * X_WIDTH), D_MODEL),
        "w_xattn_out": nrm(ks[10], (L, X_WIDTH, D_MODEL), X_WIDTH),
        "w_out": nrm(ks[11], (L, D_MODEL, D_MODEL), D_MODEL),
        "norm_ffn": gain(ks[12], (L, D_MODEL)),
        "w_gate": nrm(ks[13], (L, D_MODEL, D_FF), D_MODEL),
        "w_up": nrm(ks[14], (L, D_MODEL, D_FF), D_MODEL),
        "w_down": nrm(ks[15], (L, D_FF, D_MODEL), D_FF),
        "norm_final": gain(ks[16], (D_MODEL,)),
    }


def reference(x, mem, norm_mix, w_in, conv_w, w_conv_out, w_pool, pool_scale, norm_mem,
              w_kv, w_xattn_out, w_out, norm_ffn, w_gate, w_up, w_down, norm_final):
    b, s, _ = x.shape
    offsets = np.cumsum((0,) + IN_SPLITS)
    for l in range(DEPTH):
        h = rms_norm(x, norm_mix[l])
        proj = h @ w_in[l]
        b_a, c_a, u_a, u_p, q_x, g_a, g_p, g_x = [
            proj[..., int(offsets[i]):int(offsets[i + 1])] for i in range(len(IN_SPLITS))]

        y_a = (b_a * causal_depthwise_conv(c_a * u_a, conv_w[l])) @ w_conv_out[l]

        pooled = multiscale_causal_pool(u_p)
        y_p = jnp.einsum("bsgc,gcd->bsgd", pooled, w_pool[l]).reshape(b, s, POOL_WIDTH) * pool_scale[l]

        mem_n = rms_norm(mem, norm_mem[l])
        y_x = memory_cross_attention(q_x, mem_n, w_kv[l]) @ w_xattn_out[l]

        merged = (jax.nn.sigmoid(g_a) * y_a + jax.nn.sigmoid(g_p) * y_p
                  + jax.nn.sigmoid(g_x) * y_x)
        x = x + merged @ w_out[l]

        h = rms_norm(x, norm_ffn[l])
        x = x + (jax.nn.silu(h @ w_gate[l]) * (h @ w_up[l])) @ w_down[l]
    return rms_norm(x, norm_final)
```

```python
import functools

import jax
import jax.numpy as jnp
from jax import lax
from jax.experimental import pallas as pl
from jax.experimental.pallas import tpu as pltpu

F32 = jnp.float32
BF16 = jnp.bfloat16

D_MODEL = 1024
N_MEM = 256
CONV_K = 3
POOL_WINDOWS = (2, 4, 8, 16)
X_HEADS = 4
X_HEAD_DIM = D_MODEL // X_HEADS
D_FF = 2816
EPS = 1e-6

COL_B_A, COL_C_A, COL_U_A, COL_U_P, COL_Q_X, COL_G_A, COL_G_P, COL_G_X = (
    i * D_MODEL for i in range(8))
D_IN = 8 * D_MODEL

V7X_MXU_WIDTH = 256
V7X_VMEM_BYTES = 64 * 1024 * 1024
SUBLANES = 8
HALO = 16
TOKEN_TILE = 512
CH = V7X_MXU_WIDTH
N_CH = D_MODEL // CH


def _rms_norm(x, gain):
    ms = jnp.mean(x * x, axis=-1, keepdims=True)
    return (x * lax.rsqrt(ms + EPS)) * gain


def _sigmoid(x):
    return 1.0 / (1.0 + jnp.exp(-x))


def _dot(a, b):
    return jnp.dot(a, b, preferred_element_type=F32)


def _cols(j):
    return slice(j * CH, (j + 1) * CH)


def _kv_kernel(mem_ref, gain_ref, wkv_ref, kt_ref, v_ref):
    mn = _rms_norm(mem_ref[0], gain_ref[...]).astype(BF16)
    scale = X_HEAD_DIM ** -0.5
    for h in range(X_HEADS):
        k = _dot(mn, wkv_ref[:, _cols(h)]) * scale
        kt_ref[0, _cols(h), :] = k.T.astype(BF16)
    for j in range(N_CH):
        v_ref[0, :, _cols(j)] = _dot(mn, wkv_ref[:, D_MODEL + j * CH:D_MODEL + (j + 1) * CH]).astype(BF16)


def _mixer_kernel(x_ref, kt_ref, v_ref, gain_ref, win_ref, convw_ref, wco_ref, wpool_ref,
                  pscale_ref, wxo_ref, wout_ref, o_ref,
                  h_buf, cu_buf, up_buf, z_buf, merged_buf):
    t = pl.program_id(1)
    tile = x_ref.shape[1]

    @pl.when(t == 0)
    def _():
        cu_buf[0:HALO, :] = jnp.zeros((HALO, D_MODEL), F32)
        up_buf[0:HALO, :] = jnp.zeros((HALO, D_MODEL), F32)

    h_buf[...] = _rms_norm(x_ref[0], gain_ref[...]).astype(BF16)

    def proj(col0, j):
        return _dot(h_buf[...], win_ref[:, col0 + j * CH:col0 + (j + 1) * CH])

    for j in range(N_CH):
        cu_buf[HALO:, _cols(j)] = proj(COL_C_A, j) * proj(COL_U_A, j)
    for j in range(N_CH):
        cw = convw_ref[:, _cols(j)]
        conv = cw[CONV_K - 1:CONV_K] * cu_buf[HALO:, _cols(j)]
        for k in range(CONV_K - 1):
            lag = CONV_K - 1 - k
            conv = conv + cw[k:k + 1] * cu_buf[HALO - lag:HALO - lag + tile, _cols(j)]
        z_buf[:, _cols(j)] = (proj(COL_B_A, j) * conv).astype(BF16)
    cu_buf[0:HALO, :] = cu_buf[tile:tile + HALO, :]
    for j in range(N_CH):
        y = _dot(z_buf[...], wco_ref[:, _cols(j)])
        merged_buf[:, _cols(j)] = _sigmoid(proj(COL_G_A, j)) * y

    for g in range(len(POOL_WINDOWS)):
        up_buf[HALO:, _cols(g)] = proj(COL_U_P, g)
    pos = t * tile + 1 + lax.broadcasted_iota(jnp.int32, (tile, 1), 0)
    for g, w in enumerate(POOL_WINDOWS):
        e = up_buf[:, _cols(g)]
        s = e
        k = 1
        while k < w:
            s = s + pltpu.roll(s, k, 0)
            k *= 2
        inv_cnt = 1.0 / jnp.minimum(pos, w).astype(F32)
        pooled = s[HALO:] * inv_cnt - e[HALO:]
        y = _dot(pooled.astype(BF16), wpool_ref[g]) * pscale_ref[:, _cols(g)]
        merged_buf[:, _cols(g)] += _sigmoid(proj(COL_G_P, g)) * y
    up_buf[0:HALO, :] = up_buf[tile:tile + HALO, :]

    for h in range(X_HEADS):
        q = proj(COL_Q_X, h).astype(BF16)
        s = _dot(q, kt_ref[0, _cols(h), :])
        p = jnp.exp(s - jnp.max(s, axis=-1, keepdims=True))
        inv_l = 1.0 / jnp.sum(p, axis=-1, keepdims=True)
        z_buf[:, _cols(h)] = (_dot(p.astype(BF16), v_ref[0, :, _cols(h)]) * inv_l).astype(BF16)
    for j in range(N_CH):
        y = _dot(z_buf[...], wxo_ref[:, _cols(j)])
        merged_buf[:, _cols(j)] += _sigmoid(proj(COL_G_X, j)) * y

    h_buf[...] = merged_buf[...].astype(BF16)
    for j in range(N_CH):
        o_ref[0, :, _cols(j)] = x_ref[0, :, _cols(j)] + _dot(h_buf[...], wout_ref[:, _cols(j)])


def _ffn_kernel(x_ref, gain_ref, wg_ref, wu_ref, wd_ref, gfin_ref, o_ref, h_buf, a_buf, y_buf):
    h_buf[...] = _rms_norm(x_ref[...], gain_ref[...]).astype(BF16)
    for c in range(D_FF // CH):
        g = _dot(h_buf[...], wg_ref[:, _cols(c)])
        u = _dot(h_buf[...], wu_ref[:, _cols(c)])
        a_buf[:, _cols(c)] = (g * _sigmoid(g) * u).astype(BF16)
    for j in range(N_CH):
        y_buf[:, _cols(j)] = x_ref[:, _cols(j)] + _dot(a_buf[...], wd_ref[:, _cols(j)])
    o_ref[...] = _rms_norm(y_buf[...], gfin_ref[...])


def _resident(shape):
    return pl.BlockSpec(shape, lambda *_: (0,) * len(shape), pipeline_mode=pl.Buffered(1))


def _nbytes(shape, dtype):
    n = 1
    for s in shape:
        n *= s
    return n * jnp.dtype(dtype).itemsize


def _vmem_limit(resident, streamed, scratch):
    need = sum(resident) + 2 * sum(streamed) + 2 * sum(scratch)
    return min(need + (4 << 20), V7X_VMEM_BYTES - (4 << 20))


def kernel(x, mem, norm_mix, w_in, conv_w, w_conv_out, w_pool, pool_scale, norm_mem, w_kv,
           w_xattn_out, w_out, norm_ffn, w_gate, w_up, w_down, norm_final):
    batch, seq, d = x.shape
    assert d == D_MODEL and mem.shape == (batch, N_MEM, D_MODEL)
    assert w_in.shape == (1, D_MODEL, D_IN) and w_gate.shape == (1, D_MODEL, D_FF)
    assert seq % TOKEN_TILE == 0
    tile = TOKEN_TILE
    row = lambda a: a.reshape(1, -1).astype(F32)

    kt, v = pl.pallas_call(
        _kv_kernel,
        grid=(batch,),
        in_specs=[
            pl.BlockSpec((1, N_MEM, D_MODEL), lambda b: (b, 0, 0)),
            _resident((1, D_MODEL)),
            _resident((D_MODEL, 2 * D_MODEL)),
        ],
        out_specs=[
            pl.BlockSpec((1, D_MODEL, N_MEM), lambda b: (b, 0, 0)),
            pl.BlockSpec((1, N_MEM, D_MODEL), lambda b: (b, 0, 0)),
        ],
        out_shape=[
            jax.ShapeDtypeStruct((batch, D_MODEL, N_MEM), BF16),
            jax.ShapeDtypeStruct((batch, N_MEM, D_MODEL), BF16),
        ],
        compiler_params=pltpu.CompilerParams(dimension_semantics=("arbitrary",)),
        name="kv_proj",
    )(mem, row(norm_mem[0]), w_kv[0].astype(BF16))

    mixer_resident = [
        _nbytes((D_MODEL, D_IN), BF16), 3 * _nbytes((D_MODEL, D_MODEL), BF16),
        _nbytes((len(POOL_WINDOWS), CH, CH), BF16)]
    mixer_streamed = [2 * _nbytes((tile, D_MODEL), F32), 2 * _nbytes((N_MEM, D_MODEL), BF16)]
    mixer_scratch = [
        _nbytes((tile, D_MODEL), BF16), 2 * _nbytes((HALO + tile, D_MODEL), F32),
        _nbytes((tile, D_MODEL), BF16), _nbytes((tile, D_MODEL), F32)]
    x1 = pl.pallas_call(
        _mixer_kernel,
        grid=(batch, seq // tile),
        in_specs=[
            pl.BlockSpec((1, tile, D_MODEL), lambda b, t: (b, t, 0)),
            pl.BlockSpec((1, D_MODEL, N_MEM), lambda b, t: (b, 0, 0)),
            pl.BlockSpec((1, N_MEM, D_MODEL), lambda b, t: (b, 0, 0)),
            _resident((1, D_MODEL)),
            _resident((D_MODEL, D_IN)),
            _resident((CONV_K, D_MODEL)),
            _resident((D_MODEL, D_MODEL)),
            _resident((len(POOL_WINDOWS), CH, CH)),
            _resident((1, D_MODEL)),
            _resident((D_MODEL, D_MODEL)),
            _resident((D_MODEL, D_MODEL)),
        ],
        out_specs=pl.BlockSpec((1, tile, D_MODEL), lambda b, t: (b, t, 0)),
        out_shape=jax.ShapeDtypeStruct((batch, seq, D_MODEL), F32),
        scratch_shapes=[
            pltpu.VMEM((tile, D_MODEL), BF16),
            pltpu.VMEM((HALO + tile, D_MODEL), F32),
            pltpu.VMEM((HALO + tile, D_MODEL), F32),
            pltpu.VMEM((tile, D_MODEL), BF16),
            pltpu.VMEM((tile, D_MODEL), F32),
        ],
        compiler_params=pltpu.CompilerParams(
            dimension_semantics=("arbitrary", "arbitrary"),
            vmem_limit_bytes=_vmem_limit(mixer_resident, mixer_streamed, mixer_scratch)),
        name="mixer",
    )(x, kt, v, row(norm_mix[0]), w_in[0].astype(BF16), conv_w[0].astype(F32),
      w_conv_out[0].astype(BF16), w_pool[0].astype(BF16), row(pool_scale[0]),
      w_xattn_out[0].astype(BF16), w_out[0].astype(BF16))

    n_tok = batch * seq
    ffn_resident = [3 * _nbytes((D_MODEL, D_FF), BF16)]
    ffn_streamed = [2 * _nbytes((tile, D_MODEL), F32)]
    ffn_scratch = [_nbytes((tile, D_MODEL), BF16), _nbytes((tile, D_FF), BF16), _nbytes((tile, D_MODEL), F32)]
    out = pl.pallas_call(
        _ffn_kernel,
        grid=(n_tok // tile,),
        in_specs=[
            pl.BlockSpec((tile, D_MODEL), lambda i: (i, 0)),
            _resident((1, D_MODEL)),
            _resident((D_MODEL, D_FF)),
            _resident((D_MODEL, D_FF)),
            _resident((D_FF, D_MODEL)),
            _resident((1, D_MODEL)),
        ],
        out_specs=pl.BlockSpec((tile, D_MODEL), lambda i: (i, 0)),
        out_shape=jax.ShapeDtypeStruct((n_tok, D_MODEL), F32),
        scratch_shapes=[
            pltpu.VMEM((tile, D_MODEL), BF16),
            pltpu.VMEM((tile, D_FF), BF16),
            pltpu.VMEM((tile, D_MODEL), F32),
        ],
        compiler_params=pltpu.CompilerParams(
            dimension_semantics=("arbitrary",),
            vmem_limit_bytes=_vmem_limit(ffn_resident, ffn_streamed, ffn_scratch)),
        name="ffn",
    )(x1.reshape(n_tok, D_MODEL), row(norm_ffn[0]), w_gate[0].astype(BF16), w_up[0].astype(BF16),
      w_down[0].astype(BF16), row(norm_final))
    return out.reshape(batch, seq, D_MODEL)
```

```python
import functools

import jax
import jax.numpy as jnp
from jax import lax
from jax.experimental import pallas as pl
from jax.experimental.pallas import tpu as pltpu

F32 = jnp.float32
BF16 = jnp.bfloat16

D_MODEL = 1024
N_MEM = 256
CONV_K = 3
POOL_WINDOWS = (2, 4, 8, 16)
X_HEADS = 4
X_HEAD_DIM = D_MODEL // X_HEADS
D_FF = 2816
EPS = 1e-6

COL_B_A, COL_C_A, COL_U_A, COL_U_P, COL_Q_X, COL_G_A, COL_G_P, COL_G_X = (
    i * D_MODEL for i in range(8))
D_IN = 8 * D_MODEL

V7X_MXU_WIDTH = 256
V7X_VMEM_BYTES = 64 * 1024 * 1024
SUBLANES = 8
HALO = 16
TOKEN_TILE = 512
CH = V7X_MXU_WIDTH
N_CH = D_MODEL // CH


def _rms_norm(x, gain):
    ms = jnp.mean(x * x, axis=-1, keepdims=True)
    return (x * lax.rsqrt(ms + EPS)) * gain


def _sigmoid(x):
    return 1.0 / (1.0 + jnp.exp(-x))


def _dot(a, b):
    return jnp.dot(a, b, preferred_element_type=F32)


def _cols(j):
    return slice(j * CH, (j + 1) * CH)


def _kv_kernel(mem_ref, gain_ref, wkv_ref, kt_ref, v_ref):
    mn = _rms_norm(mem_ref[0], gain_ref[...]).astype(BF16)
    scale = X_HEAD_DIM ** -0.5
    for h in range(X_HEADS):
        k = _dot(mn, wkv_ref[:, _cols(h)]) * scale
        kt_ref[0, _cols(h), :] = k.T.astype(BF16)
    for j in range(N_CH):
        v_ref[0, :, _cols(j)] = _dot(mn, wkv_ref[:, D_MODEL + j * CH:D_MODEL + (j + 1) * CH]).astype(BF16)


def _mixer_kernel(x_ref, kt_ref, v_ref, gain_ref, win_ref, convw_ref, wco_ref, wpool_ref,
                  pscale_ref, wxo_ref, wout_ref, o_ref,
                  h_buf, cu_buf, up_buf, z_buf, merged_buf):
    t = pl.program_id(1)
    tile = x_ref.shape[1]

    @pl.when(t == 0)
    def _():
        cu_buf[0:HALO, :] = jnp.zeros((HALO, D_MODEL), F32)
        up_buf[0:HALO, :] = jnp.zeros((HALO, D_MODEL), F32)

    h_buf[...] = _rms_norm(x_ref[0], gain_ref[...]).astype(BF16)

    def proj(col0, j):
        return _dot(h_buf[...], win_ref[:, col0 + j * CH:col0 + (j + 1) * CH])

    for j in range(N_CH):
        cu_buf[HALO:, _cols(j)] = proj(COL_C_A, j) * proj(COL_U_A, j)
    for j in range(N_CH):
        cw = convw_ref[:, _cols(j)]
        conv = cw[CONV_K - 1:CONV_K] * cu_buf[HALO:, _cols(j)]
        for k in range(CONV_K - 1):
            lag = CONV_K - 1 - k
            conv = conv + cw[k:k + 1] * cu_buf[HALO - lag:HALO - lag + tile, _cols(j)]
        z_buf[:, _cols(j)] = (proj(COL_B_A, j) * conv).astype(BF16)
    cu_buf[0:HALO, :] = cu_buf[tile:tile + HALO, :]
    for j in range(N_CH):
        y = _dot(z_buf[...], wco_ref[:, _cols(j)])
        merged_buf[:, _cols(j)] = _sigmoid(proj(COL_G_A, j)) * y

    for g in range(len(POOL_WINDOWS)):
        up_buf[HALO:, _cols(g)] = proj(COL_U_P, g)
    pos = t * tile + 1 + lax.broadcasted_iota(jnp.int32, (tile, 1), 0)
    for g, w in enumerate(POOL_WINDOWS):
        e = up_buf[:, _cols(g)]
        s = e
        k = 1
        while k < w:
            s = s + pltpu.roll(s, k, 0)
            k *= 2
        inv_cnt = 1.0 / jnp.minimum(pos, w).astype(F32)
        pooled = s[HALO:] * inv_cnt - e[HALO:]
        y = _dot(pooled.astype(BF16), wpool_ref[g]) * pscale_ref[:, _cols(g)]
        merged_buf[:, _cols(g)] += _sigmoid(proj(COL_G_P, g)) * y
    up_buf[0:HALO, :] = up_buf[tile:tile + HALO, :]

    for h0 in range(0, X_HEADS, 2):
        pair = (h0, h0 + 1)
        q = [proj(COL_Q_X, h).astype(BF16) for h in pair]
        s = [_dot(q[i], kt_ref[0, _cols(h), :]) for i, h in enumerate(pair)]
        p = [jnp.exp(si - jnp.max(si, axis=-1, keepdims=True)) for si in s]
        inv_l = [1.0 / jnp.sum(pi, axis=-1, keepdims=True) for pi in p]
        o = [_dot(p[i].astype(BF16), v_ref[0, :, _cols(h)]) for i, h in enumerate(pair)]
        for i, h in enumerate(pair):
            z_buf[:, _cols(h)] = (o[i] * inv_l[i]).astype(BF16)
    for j in range(N_CH):
        y = _dot(z_buf[...], wxo_ref[:, _cols(j)])
        merged_buf[:, _cols(j)] += _sigmoid(proj(COL_G_X, j)) * y

    h_buf[...] = merged_buf[...].astype(BF16)
    for j in range(N_CH):
        o_ref[0, :, _cols(j)] = x_ref[0, :, _cols(j)] + _dot(h_buf[...], wout_ref[:, _cols(j)])


def _ffn_kernel(x_ref, gain_ref, wg_ref, wu_ref, wd_ref, gfin_ref, o_ref, h_buf, a_buf, y_buf):
    h_buf[...] = _rms_norm(x_ref[...], gain_ref[...]).astype(BF16)
    for c in range(D_FF // CH):
        g = _dot(h_buf[...], wg_ref[:, _cols(c)])
        u = _dot(h_buf[...], wu_ref[:, _cols(c)])
        a_buf[:, _cols(c)] = (g * _sigmoid(g) * u).astype(BF16)
    for j in range(N_CH):
        y_buf[:, _cols(j)] = x_ref[:, _cols(j)] + _dot(a_buf[...], wd_ref[:, _cols(j)])
    o_ref[...] = _rms_norm(y_buf[...], gfin_ref[...])


def _resident(shape):
    return pl.BlockSpec(shape, lambda *_: (0,) * len(shape), pipeline_mode=pl.Buffered(1))


def _nbytes(shape, dtype):
    n = 1
    for s in shape:
        n *= s
    return n * jnp.dtype(dtype).itemsize


def _vmem_limit(resident, streamed, scratch):
    need = sum(resident) + 2 * sum(streamed) + 2 * sum(scratch)
    return min(need + (4 << 20), V7X_VMEM_BYTES - (4 << 20))


def kernel(x, mem, norm_mix, w_in, conv_w, w_conv_out, w_pool, pool_scale, norm_mem, w_kv,
           w_xattn_out, w_out, norm_ffn, w_gate, w_up, w_down, norm_final):
    batch, seq, d = x.shape
    assert d == D_MODEL and mem.shape == (batch, N_MEM, D_MODEL)
    assert w_in.shape == (1, D_MODEL, D_IN) and w_gate.shape == (1, D_MODEL, D_FF)
    assert seq % TOKEN_TILE == 0
    tile = TOKEN_TILE
    row = lambda a: a.reshape(1, -1).astype(F32)

    kt, v = pl.pallas_call(
        _kv_kernel,
        grid=(batch,),
        in_specs=[
            pl.BlockSpec((1, N_MEM, D_MODEL), lambda b: (b, 0, 0)),
            _resident((1, D_MODEL)),
            _resident((D_MODEL, 2 * D_MODEL)),
        ],
        out_specs=[
            pl.BlockSpec((1, D_MODEL, N_MEM), lambda b: (b, 0, 0)),
            pl.BlockSpec((1, N_MEM, D_MODEL), lambda b: (b, 0, 0)),
        ],
        out_shape=[
            jax.ShapeDtypeStruct((batch, D_MODEL, N_MEM), BF16),
            jax.ShapeDtypeStruct((batch, N_MEM, D_MODEL), BF16),
        ],
        compiler_params=pltpu.CompilerParams(dimension_semantics=("arbitrary",)),
        name="kv_proj",
    )(mem, row(norm_mem[0]), w_kv[0].astype(BF16))

    mixer_resident = [
        _nbytes((D_MODEL, D_IN), BF16), 3 * _nbytes((D_MODEL, D_MODEL), BF16),
        _nbytes((len(POOL_WINDOWS), CH, CH), BF16)]
    mixer_streamed = [2 * _nbytes((tile, D_MODEL), F32), 2 * _nbytes((N_MEM, D_MODEL), BF16)]
    mixer_scratch = [
        _nbytes((tile, D_MODEL), BF16), 2 * _nbytes((HALO + tile, D_MODEL), F32),
        _nbytes((tile, D_MODEL), BF16), _nbytes((tile, D_MODEL), F32)]
    x1 = pl.pallas_call(
        _mixer_kernel,
        grid=(batch, seq // tile),
        in_specs=[
            pl.BlockSpec((1, tile, D_MODEL), lambda b, t: (b, t, 0)),
            pl.BlockSpec((1, D_MODEL, N_MEM), lambda b, t: (b, 0, 0)),
            pl.BlockSpec((1, N_MEM, D_MODEL), lambda b, t: (b, 0, 0)),
            _resident((1, D_MODEL)),
            _resident((D_MODEL, D_IN)),
            _resident((CONV_K, D_MODEL)),
            _resident((D_MODEL, D_MODEL)),
            _resident((len(POOL_WINDOWS), CH, CH)),
            _resident((1, D_MODEL)),
            _resident((D_MODEL, D_MODEL)),
            _resident((D_MODEL, D_MODEL)),
        ],
        out_specs=pl.BlockSpec((1, tile, D_MODEL), lambda b, t: (b, t, 0)),
        out_shape=jax.ShapeDtypeStruct((batch, seq, D_MODEL), F32),
        scratch_shapes=[
            pltpu.VMEM((tile, D_MODEL), BF16),
            pltpu.VMEM((HALO + tile, D_MODEL), F32),
            pltpu.VMEM((HALO + tile, D_MODEL), F32),
            pltpu.VMEM((tile, D_MODEL), BF16),
            pltpu.VMEM((tile, D_MODEL), F32),
        ],
        compiler_params=pltpu.CompilerParams(
            dimension_semantics=("arbitrary", "arbitrary"),
            vmem_limit_bytes=_vmem_limit(mixer_resident, mixer_streamed, mixer_scratch)),
        name="mixer",
    )(x, kt, v, row(norm_mix[0]), w_in[0].astype(BF16), conv_w[0].astype(F32),
      w_conv_out[0].astype(BF16), w_pool[0].astype(BF16), row(pool_scale[0]),
      w_xattn_out[0].astype(BF16), w_out[0].astype(BF16))

    n_tok = batch * seq
    ffn_resident = [3 * _nbytes((D_MODEL, D_FF), BF16)]
    ffn_streamed = [2 * _nbytes((tile, D_MODEL), F32)]
    ffn_scratch = [_nbytes((tile, D_MODEL), BF16), _nbytes((tile, D_FF), BF16), _nbytes((tile, D_MODEL), F32)]
    out = pl.pallas_call(
        _ffn_kernel,
        grid=(n_tok // tile,),
        in_specs=[
            pl.BlockSpec((tile, D_MODEL), lambda i: (i, 0)),
            _resident((1, D_MODEL)),
            _resident((D_MODEL, D_FF)),
            _resident((D_MODEL, D_FF)),
            _resident((D_FF, D_MODEL)),
            _resident((1, D_MODEL)),
        ],
        out_specs=pl.BlockSpec((tile, D_MODEL), lambda i: (i, 0)),
        out_shape=jax.ShapeDtypeStruct((n_tok, D_MODEL), F32),
        scratch_shapes=[
            pltpu.VMEM((tile, D_MODEL), BF16),
            pltpu.VMEM((tile, D_FF), BF16),
            pltpu.VMEM((tile, D_MODEL), F32),
        ],
        compiler_params=pltpu.CompilerParams(
            dimension_semantics=("arbitrary",),
            vmem_limit_bytes=_vmem_limit(ffn_resident, ffn_streamed, ffn_scratch)),
        name="ffn",
    )(x1.reshape(n_tok, D_MODEL), row(norm_ffn[0]), w_gate[0].astype(BF16), w_up[0].astype(BF16),
      w_down[0].astype(BF16), row(norm_final))
    return out.reshape(batch, seq, D_MODEL)
```

```python
import functools

import jax
import jax.numpy as jnp
from jax import lax
from jax.experimental import pallas as pl
from jax.experimental.pallas import tpu as pltpu

F32 = jnp.float32
BF16 = jnp.bfloat16

D_MODEL = 1024
N_MEM = 256
CONV_K = 3
POOL_WINDOWS = (2, 4, 8, 16)
N_POOL = len(POOL_WINDOWS)
X_HEADS = 4
X_HEAD_DIM = D_MODEL // X_HEADS
D_FF = 2816
EPS = 1e-6

COL_B_A, COL_C_A, COL_U_A, COL_U_P, COL_Q_X, COL_G_A, COL_G_P, COL_G_X = (
    i * D_MODEL for i in range(8))
D_IN = 8 * D_MODEL

V7X_MXU_WIDTH = 256
V7X_VMEM_BYTES = 64 * 1024 * 1024
BF16_SUBLANES = 16
HALO = 16
TOKEN_TILE = 512
CH = V7X_MXU_WIDTH
N_CH = D_MODEL // CH
PREP_STEPS = 8


def _rms_norm(x, gain):
    ms = jnp.mean(x * x, axis=-1, keepdims=True)
    return (x * lax.rsqrt(ms + EPS)) * gain


def _sigmoid(x):
    return 1.0 / (1.0 + jnp.exp(-x))


def _dot(a, b):
    return jnp.dot(a, b, preferred_element_type=F32)


def _cols(j):
    return slice(j * CH, (j + 1) * CH)


def _prep_kernel(mem_ref, gain_ref, wkv_ref, win_ref, wco_ref, wxo_ref, wout_ref, wpool_ref,
                 kt_ref, v_ref, win_o, wco_o, wxo_o, wout_o, wpool_o, wkv_buf, *, batch):
    i = pl.program_id(0)

    win_o[...] = win_ref[...].astype(BF16)
    wco_o[...] = wco_ref[...].astype(BF16)
    wxo_o[...] = wxo_ref[...].astype(BF16)
    wout_o[...] = wout_ref[...].astype(BF16)
    wpool_o[...] = wpool_ref[...].astype(BF16)

    @pl.when(i == 0)
    def _():
        wkv_buf[...] = wkv_ref[...].astype(BF16)

    @pl.when(i < batch)
    def _():
        mn = _rms_norm(mem_ref[0], gain_ref[...]).astype(BF16)
        scale = X_HEAD_DIM ** -0.5
        for h in range(X_HEADS):
            k = _dot(mn, wkv_buf[:, _cols(h)]) * scale
            kt_ref[0, _cols(h), :] = k.T.astype(BF16)
        for j in range(N_CH):
            v_ref[0, :, _cols(j)] = _dot(mn, wkv_buf[:, D_MODEL + j * CH:D_MODEL + (j + 1) * CH]).astype(BF16)


def _mixer_kernel(x_ref, kt_ref, v_ref, gain_ref, win_ref, convw_ref, wco_ref, wpool_ref,
                  pscale_ref, wxo_ref, wout_ref, wg_ref, wu_ref, wd_ref,
                  o_ref, wg_o, wu_o, wd_o,
                  h_buf, cu_buf, up_buf, z_buf, merged_buf):
    t = pl.program_id(1)
    tile = x_ref.shape[1]

    @pl.when(t == 0)
    def _():
        cu_buf[0:HALO, :] = jnp.zeros((HALO, D_MODEL), F32)
        up_buf[0:HALO, :] = jnp.zeros((HALO, D_MODEL), F32)

    h_buf[...] = _rms_norm(x_ref[0], gain_ref[...]).astype(BF16)

    wg_o[...] = wg_ref[...].astype(BF16)
    wu_o[...] = wu_ref[...].astype(BF16)
    wd_o[...] = wd_ref[...].astype(BF16)

    def proj(col0, j):
        return _dot(h_buf[...], win_ref[:, col0 + j * CH:col0 + (j + 1) * CH])

    for j in range(N_CH):
        cu_buf[HALO:, _cols(j)] = proj(COL_C_A, j) * proj(COL_U_A, j)
    for j in range(N_CH):
        cw = convw_ref[:, _cols(j)]
        conv = cw[CONV_K - 1:CONV_K] * cu_buf[HALO:, _cols(j)]
        for k in range(CONV_K - 1):
            lag = CONV_K - 1 - k
            conv = conv + cw[k:k + 1] * cu_buf[HALO - lag:HALO - lag + tile, _cols(j)]
        z_buf[:, _cols(j)] = (proj(COL_B_A, j) * conv).astype(BF16)
    cu_buf[0:HALO, :] = cu_buf[tile:tile + HALO, :]
    for j in range(N_CH):
        y = _dot(z_buf[...], wco_ref[:, _cols(j)])
        merged_buf[:, _cols(j)] = _sigmoid(proj(COL_G_A, j)) * y

    for g in range(N_POOL):
        up_buf[HALO:, _cols(g)] = proj(COL_U_P, g)
    pos = t * tile + 1 + lax.broadcasted_iota(jnp.int32, (tile, 1), 0)
    for g, w in enumerate(POOL_WINDOWS):
        e = up_buf[:, _cols(g)]
        s = e
        k = 1
        while k < w:
            s = s + pltpu.roll(s, k, 0)
            k *= 2
        inv_cnt = 1.0 / jnp.minimum(pos, w).astype(F32)
        pooled = s[HALO:] * inv_cnt - e[HALO:]
        y = _dot(pooled.astype(BF16), wpool_ref[g]) * pscale_ref[:, _cols(g)]
        merged_buf[:, _cols(g)] += _sigmoid(proj(COL_G_P, g)) * y
    up_buf[0:HALO, :] = up_buf[tile:tile + HALO, :]

    for h0 in range(0, X_HEADS, 2):
        pair = (h0, h0 + 1)
        q = [proj(COL_Q_X, h).astype(BF16) for h in pair]
        s = [_dot(q[i], kt_ref[0, _cols(h), :]) for i, h in enumerate(pair)]
        p = [jnp.exp(si - jnp.max(si, axis=-1, keepdims=True)) for si in s]
        inv_l = [1.0 / jnp.sum(pi, axis=-1, keepdims=True) for pi in p]
        o = [_dot(p[i].astype(BF16), v_ref[0, :, _cols(h)]) for i, h in enumerate(pair)]
        for i, h in enumerate(pair):
            z_buf[:, _cols(h)] = (o[i] * inv_l[i]).astype(BF16)
    for j in range(N_CH):
        y = _dot(z_buf[...], wxo_ref[:, _cols(j)])
        merged_buf[:, _cols(j)] += _sigmoid(proj(COL_G_X, j)) * y

    h_buf[...] = merged_buf[...].astype(BF16)
    for j in range(N_CH):
        o_ref[0, :, _cols(j)] = x_ref[0, :, _cols(j)] + _dot(h_buf[...], wout_ref[:, _cols(j)])


def _ffn_kernel(x_ref, gain_ref, wg_ref, wu_ref, wd_ref, gfin_ref, o_ref, h_buf, a_buf, y_buf):
    h_buf[...] = _rms_norm(x_ref[...], gain_ref[...]).astype(BF16)
    for c in range(D_FF // CH):
        g = _dot(h_buf[...], wg_ref[:, _cols(c)])
        u = _dot(h_buf[...], wu_ref[:, _cols(c)])
        a_buf[:, _cols(c)] = (g * _sigmoid(g) * u).astype(BF16)
    for j in range(N_CH):
        y_buf[:, _cols(j)] = x_ref[:, _cols(j)] + _dot(a_buf[...], wd_ref[:, _cols(j)])
    o_ref[...] = _rms_norm(y_buf[...], gfin_ref[...])


def _resident(shape):
    return pl.BlockSpec(shape, lambda *_: (0,) * len(shape), pipeline_mode=pl.Buffered(1))


def _nbytes(shape, dtype):
    n = 1
    for s in shape:
        n *= s
    return n * jnp.dtype(dtype).itemsize


def _vmem_limit(resident, streamed, scratch):
    need = sum(resident) + 2 * sum(streamed) + 2 * sum(scratch)
    return min(need + (4 << 20), V7X_VMEM_BYTES - (4 << 20))


def kernel(x, mem, norm_mix, w_in, conv_w, w_conv_out, w_pool, pool_scale, norm_mem, w_kv,
           w_xattn_out, w_out, norm_ffn, w_gate, w_up, w_down, norm_final):
    batch, seq, d = x.shape
    assert d == D_MODEL and mem.shape == (batch, N_MEM, D_MODEL)
    assert w_in.shape == (1, D_MODEL, D_IN) and w_gate.shape == (1, D_MODEL, D_FF)
    assert seq % TOKEN_TILE == 0 and batch <= PREP_STEPS
    tile = TOKEN_TILE
    row = lambda a: a.reshape(1, -1).astype(F32)

    rb = D_MODEL // PREP_STEPS
    pool_split = PREP_STEPS // N_POOL
    prb = CH // pool_split
    assert rb % BF16_SUBLANES == 0 and prb % BF16_SUBLANES == 0
    last_batch = lambda i: (jnp.minimum(i, batch - 1), 0, 0)
    rows_of = lambda i: (i, 0)
    pool_rows_of = lambda i: (i // pool_split, i % pool_split, 0)
    prep_resident = [_nbytes((D_MODEL, 2 * D_MODEL), F32)]
    prep_streamed = [
        _nbytes((N_MEM, D_MODEL), F32), _nbytes((rb, D_IN), F32), 3 * _nbytes((rb, D_MODEL), F32),
        2 * _nbytes((N_MEM, D_MODEL), BF16), _nbytes((rb, D_IN), BF16), 3 * _nbytes((rb, D_MODEL), BF16)]
    prep_scratch = [_nbytes((D_MODEL, 2 * D_MODEL), BF16)]
    kt, v, w_in_b, w_co_b, w_xo_b, w_out_b, w_pool_b = pl.pallas_call(
        functools.partial(_prep_kernel, batch=batch),
        grid=(PREP_STEPS,),
        in_specs=[
            pl.BlockSpec((1, N_MEM, D_MODEL), last_batch),
            _resident((1, D_MODEL)),
            _resident((D_MODEL, 2 * D_MODEL)),
            pl.BlockSpec((rb, D_IN), rows_of),
            pl.BlockSpec((rb, D_MODEL), rows_of),
            pl.BlockSpec((rb, D_MODEL), rows_of),
            pl.BlockSpec((rb, D_MODEL), rows_of),
            pl.BlockSpec((1, prb, CH), pool_rows_of),
        ],
        out_specs=[
            pl.BlockSpec((1, D_MODEL, N_MEM), last_batch),
            pl.BlockSpec((1, N_MEM, D_MODEL), last_batch),
            pl.BlockSpec((rb, D_IN), rows_of),
            pl.BlockSpec((rb, D_MODEL), rows_of),
            pl.BlockSpec((rb, D_MODEL), rows_of),
            pl.BlockSpec((rb, D_MODEL), rows_of),
            pl.BlockSpec((1, prb, CH), pool_rows_of),
        ],
        out_shape=[
            jax.ShapeDtypeStruct((batch, D_MODEL, N_MEM), BF16),
            jax.ShapeDtypeStruct((batch, N_MEM, D_MODEL), BF16),
            jax.ShapeDtypeStruct((D_MODEL, D_IN), BF16),
            jax.ShapeDtypeStruct((D_MODEL, D_MODEL), BF16),
            jax.ShapeDtypeStruct((D_MODEL, D_MODEL), BF16),
            jax.ShapeDtypeStruct((D_MODEL, D_MODEL), BF16),
            jax.ShapeDtypeStruct((N_POOL, CH, CH), BF16),
        ],
        scratch_shapes=[pltpu.VMEM((D_MODEL, 2 * D_MODEL), BF16)],
        compiler_params=pltpu.CompilerParams(
            dimension_semantics=("arbitrary",),
            vmem_limit_bytes=_vmem_limit(prep_resident, prep_streamed, prep_scratch)),
        name="prep",
    )(mem, row(norm_mem[0]), w_kv[0], w_in[0], w_conv_out[0], w_xattn_out[0], w_out[0], w_pool[0])

    n_tiles = seq // tile
    n_steps = batch * n_tiles
    gu_rb = D_MODEL // n_steps
    d_steps = n_steps // 2
    d_rb = D_FF // d_steps
    assert gu_rb % BF16_SUBLANES == 0 and d_rb % BF16_SUBLANES == 0
    step_rows = lambda b, t: (b * n_tiles + t, 0)
    down_rows = lambda b, t: (jnp.minimum(b * n_tiles + t, d_steps - 1), 0)
    mixer_resident = [
        _nbytes((D_MODEL, D_IN), BF16), 3 * _nbytes((D_MODEL, D_MODEL), BF16),
        _nbytes((N_POOL, CH, CH), BF16)]
    mixer_streamed = [
        2 * _nbytes((tile, D_MODEL), F32), 2 * _nbytes((N_MEM, D_MODEL), BF16),
        2 * _nbytes((gu_rb, D_FF), F32), _nbytes((d_rb, D_MODEL), F32),
        2 * _nbytes((gu_rb, D_FF), BF16), _nbytes((d_rb, D_MODEL), BF16)]
    mixer_scratch = [
        _nbytes((tile, D_MODEL), BF16), 2 * _nbytes((HALO + tile, D_MODEL), F32),
        _nbytes((tile, D_MODEL), BF16), _nbytes((tile, D_MODEL), F32)]
    x1, w_gate_b, w_up_b, w_down_b = pl.pallas_call(
        _mixer_kernel,
        grid=(batch, n_tiles),
        in_specs=[
            pl.BlockSpec((1, tile, D_MODEL), lambda b, t: (b, t, 0)),
            pl.BlockSpec((1, D_MODEL, N_MEM), lambda b, t: (b, 0, 0)),
            pl.BlockSpec((1, N_MEM, D_MODEL), lambda b, t: (b, 0, 0)),
            _resident((1, D_MODEL)),
            _resident((D_MODEL, D_IN)),
            _resident((CONV_K, D_MODEL)),
            _resident((D_MODEL, D_MODEL)),
            _resident((N_POOL, CH, CH)),
            _resident((1, D_MODEL)),
            _resident((D_MODEL, D_MODEL)),
            _resident((D_MODEL, D_MODEL)),
            pl.BlockSpec((gu_rb, D_FF), step_rows),
            pl.BlockSpec((gu_rb, D_FF), step_rows),
            pl.BlockSpec((d_rb, D_MODEL), down_rows),
        ],
        out_specs=[
            pl.BlockSpec((1, tile, D_MODEL), lambda b, t: (b, t, 0)),
            pl.BlockSpec((gu_rb, D_FF), step_rows),
            pl.BlockSpec((gu_rb, D_FF), step_rows),
            pl.BlockSpec((d_rb, D_MODEL), down_rows),
        ],
        out_shape=[
            jax.ShapeDtypeStruct((batch, seq, D_MODEL), F32),
            jax.ShapeDtypeStruct((D_MODEL, D_FF), BF16),
            jax.ShapeDtypeStruct((D_MODEL, D_FF), BF16),
            jax.ShapeDtypeStruct((D_FF, D_MODEL), BF16),
        ],
        scratch_shapes=[
            pltpu.VMEM((tile, D_MODEL), BF16),
            pltpu.VMEM((HALO + tile, D_MODEL), F32),
            pltpu.VMEM((HALO + tile, D_MODEL), F32),
            pltpu.VMEM((tile, D_MODEL), BF16),
            pltpu.VMEM((tile, D_MODEL), F32),
        ],
        compiler_params=pltpu.CompilerParams(
            dimension_semantics=("arbitrary", "arbitrary"),
            vmem_limit_bytes=_vmem_limit(mixer_resident, mixer_streamed, mixer_scratch)),
        name="mixer",
    )(x, kt, v, row(norm_mix[0]), w_in_b, conv_w[0].astype(F32), w_co_b, w_pool_b,
      row(pool_scale[0]), w_xo_b, w_out_b, w_gate[0], w_up[0], w_down[0])

    n_tok = batch * seq
    ffn_resident = [3 * _nbytes((D_MODEL, D_FF), BF16)]
    ffn_streamed = [2 * _nbytes((tile, D_MODEL), F32)]
    ffn_scratch = [_nbytes((tile, D_MODEL), BF16), _nbytes((tile, D_FF), BF16), _nbytes((tile, D_MODEL), F32)]
    out = pl.pallas_call(
        _ffn_kernel,
        grid=(n_tok // tile,),
        in_specs=[
            pl.BlockSpec((tile, D_MODEL), lambda i: (i, 0)),
            _resident((1, D_MODEL)),
            _resident((D_MODEL, D_FF)),
            _resident((D_MODEL, D_FF)),
            _resident((D_FF, D_MODEL)),
            _resident((1, D_MODEL)),
        ],
        out_specs=pl.BlockSpec((tile, D_MODEL), lambda i: (i, 0)),
        out_shape=jax.ShapeDtypeStruct((n_tok, D_MODEL), F32),
        scratch_shapes=[
            pltpu.VMEM((tile, D_MODEL), BF16),
            pltpu.VMEM((tile, D_FF), BF16),
            pltpu.VMEM((tile, D_MODEL), F32),
        ],
        compiler_params=pltpu.CompilerParams(
            dimension_semantics=("arbitrary",),
            vmem_limit_bytes=_vmem_limit(ffn_resident, ffn_streamed, ffn_scratch)),
        name="ffn",
    )(x1.reshape(n_tok, D_MODEL), row(norm_ffn[0]), w_gate_b, w_up_b, w_down_b, row(norm_final))
    return out.reshape(batch, seq, D_MODEL)
```

```python
import functools

import jax
import jax.numpy as jnp
from jax import lax
from jax.experimental import pallas as pl
from jax.experimental.pallas import tpu as pltpu

F32 = jnp.float32
BF16 = jnp.bfloat16

D_MODEL = 1024
N_MEM = 256
CONV_K = 3
POOL_WINDOWS = (2, 4, 8, 16)
N_POOL = len(POOL_WINDOWS)
X_HEADS = 4
X_HEAD_DIM = D_MODEL // X_HEADS
D_FF = 2816
EPS = 1e-6

COL_B_A, COL_C_A, COL_U_A, COL_U_P, COL_Q_X, COL_G_A, COL_G_P, COL_G_X = (
    i * D_MODEL for i in range(8))
D_IN = 8 * D_MODEL

V7X_MXU_WIDTH = 256
V7X_VMEM_BYTES = 64 * 1024 * 1024
V7X_SCOPED_VMEM_BYTES = V7X_VMEM_BYTES - 4 * 1024 * 1024
BF16_SUBLANES = 16
HALO = 16
TOKEN_TILE = 512
CH = V7X_MXU_WIDTH
N_CH = D_MODEL // CH
PREP_STEPS = 8


def _rms_norm(x, gain):
    ms = jnp.mean(x * x, axis=-1, keepdims=True)
    return (x * lax.rsqrt(ms + EPS)) * gain


def _sigmoid(x):
    return 1.0 / (1.0 + jnp.exp(-x))


def _dot(a, b):
    return jnp.dot(a, b, preferred_element_type=F32)


def _cols(j):
    return slice(j * CH, (j + 1) * CH)


def _prep_kernel(mem_ref, gain_ref, wkv_ref, win_ref, wco_ref, wxo_ref, wout_ref, wpool_ref,
                 kt_ref, v_ref, win_o, wco_o, wxo_o, wout_o, wpool_o, wkv_buf, *, batch):
    i = pl.program_id(0)

    win_o[...] = win_ref[...].astype(BF16)
    wco_o[...] = wco_ref[...].astype(BF16)
    wxo_o[...] = wxo_ref[...].astype(BF16)
    wout_o[...] = wout_ref[...].astype(BF16)
    wpool_o[...] = wpool_ref[...].astype(BF16)

    @pl.when(i == 0)
    def _():
        wkv_buf[...] = wkv_ref[...].astype(BF16)

    @pl.when(i < batch)
    def _():
        mn = _rms_norm(mem_ref[0], gain_ref[...]).astype(BF16)
        scale = X_HEAD_DIM ** -0.5
        for h in range(X_HEADS):
            k = _dot(mn, wkv_buf[:, _cols(h)]) * scale
            kt_ref[0, _cols(h), :] = k.T.astype(BF16)
        for j in range(N_CH):
            v_ref[0, :, _cols(j)] = _dot(mn, wkv_buf[:, D_MODEL + j * CH:D_MODEL + (j + 1) * CH]).astype(BF16)


def _mixer_kernel(x_ref, kt_ref, v_ref, gain_ref, win_ref, convw_ref, wco_ref, wpool_ref,
                  pscale_ref, wxo_ref, wout_ref, wg_ref, wu_ref, wd_ref,
                  o_ref, wg_o, wu_o, wd_o,
                  h_buf, cu_buf, up_buf, z_buf, merged_buf):
    t = pl.program_id(1)
    tile = x_ref.shape[1]

    @pl.when(t == 0)
    def _():
        cu_buf[0:HALO, :] = jnp.zeros((HALO, D_MODEL), F32)
        up_buf[0:HALO, :] = jnp.zeros((HALO, D_MODEL), F32)

    h_buf[...] = _rms_norm(x_ref[0], gain_ref[...]).astype(BF16)

    wg_o[...] = wg_ref[...].astype(BF16)
    wu_o[...] = wu_ref[...].astype(BF16)
    wd_o[...] = wd_ref[...].astype(BF16)

    def proj(col0, j):
        return _dot(h_buf[...], win_ref[:, col0 + j * CH:col0 + (j + 1) * CH])

    for j in range(N_CH):
        cu_buf[HALO:, _cols(j)] = proj(COL_C_A, j) * proj(COL_U_A, j)
    for j in range(N_CH):
        cw = convw_ref[:, _cols(j)]
        conv = cw[CONV_K - 1:CONV_K] * cu_buf[HALO:, _cols(j)]
        for k in range(CONV_K - 1):
            lag = CONV_K - 1 - k
            conv = conv + cw[k:k + 1] * cu_buf[HALO - lag:HALO - lag + tile, _cols(j)]
        z_buf[:, _cols(j)] = (proj(COL_B_A, j) * conv).astype(BF16)
    cu_buf[0:HALO, :] = cu_buf[tile:tile + HALO, :]
    for j in range(N_CH):
        y = _dot(z_buf[...], wco_ref[:, _cols(j)])
        merged_buf[:, _cols(j)] = _sigmoid(proj(COL_G_A, j)) * y

    for g in range(N_POOL):
        up_buf[HALO:, _cols(g)] = proj(COL_U_P, g)
    pos = t * tile + 1 + lax.broadcasted_iota(jnp.int32, (tile, 1), 0)
    for g, w in enumerate(POOL_WINDOWS):
        e = up_buf[:, _cols(g)]
        s = e
        k = 1
        while k < w:
            s = s + pltpu.roll(s, k, 0)
            k *= 2
        inv_cnt = 1.0 / jnp.minimum(pos, w).astype(F32)
        pooled = s[HALO:] * inv_cnt - e[HALO:]
        y = _dot(pooled.astype(BF16), wpool_ref[g]) * pscale_ref[:, _cols(g)]
        merged_buf[:, _cols(g)] += _sigmoid(proj(COL_G_P, g)) * y
    up_buf[0:HALO, :] = up_buf[tile:tile + HALO, :]

    for h0 in range(0, X_HEADS, 2):
        pair = (h0, h0 + 1)
        q = [proj(COL_Q_X, h).astype(BF16) for h in pair]
        s = [_dot(q[i], kt_ref[0, _cols(h), :]) for i, h in enumerate(pair)]
        p = [jnp.exp(si - jnp.max(si, axis=-1, keepdims=True)) for si in s]
        inv_l = [1.0 / jnp.sum(pi, axis=-1, keepdims=True) for pi in p]
        o = [_dot(p[i].astype(BF16), v_ref[0, :, _cols(h)]) for i, h in enumerate(pair)]
        for i, h in enumerate(pair):
            z_buf[:, _cols(h)] = (o[i] * inv_l[i]).astype(BF16)
    for j in range(N_CH):
        y = _dot(z_buf[...], wxo_ref[:, _cols(j)])
        merged_buf[:, _cols(j)] += _sigmoid(proj(COL_G_X, j)) * y

    h_buf[...] = merged_buf[...].astype(BF16)
    for j in range(N_CH):
        o_ref[0, :, _cols(j)] = x_ref[0, :, _cols(j)] + _dot(h_buf[...], wout_ref[:, _cols(j)])


def _ffn_kernel(x_ref, gain_ref, wg_ref, wu_ref, wd_ref, gfin_ref, o_ref, h_buf, a_buf, y_buf):
    h_buf[...] = _rms_norm(x_ref[...], gain_ref[...]).astype(BF16)
    for c in range(D_FF // CH):
        g = _dot(h_buf[...], wg_ref[:, _cols(c)])
        u = _dot(h_buf[...], wu_ref[:, _cols(c)])
        a_buf[:, _cols(c)] = (g * _sigmoid(g) * u).astype(BF16)
    for j in range(N_CH):
        y_buf[:, _cols(j)] = x_ref[:, _cols(j)] + _dot(a_buf[...], wd_ref[:, _cols(j)])
    o_ref[...] = _rms_norm(y_buf[...], gfin_ref[...])


def _resident(shape):
    return pl.BlockSpec(shape, lambda *_: (0,) * len(shape), pipeline_mode=pl.Buffered(1))


def _nbytes(shape, dtype):
    n = 1
    for s in shape:
        n *= s
    return n * jnp.dtype(dtype).itemsize


def _vmem_limit(resident, streamed, scratch):
    need = sum(resident) + 2 * sum(streamed) + 2 * sum(scratch)
    assert need <= V7X_SCOPED_VMEM_BYTES, need
    return V7X_SCOPED_VMEM_BYTES


def kernel(x, mem, norm_mix, w_in, conv_w, w_conv_out, w_pool, pool_scale, norm_mem, w_kv,
           w_xattn_out, w_out, norm_ffn, w_gate, w_up, w_down, norm_final):
    batch, seq, d = x.shape
    assert d == D_MODEL and mem.shape == (batch, N_MEM, D_MODEL)
    assert w_in.shape == (1, D_MODEL, D_IN) and w_gate.shape == (1, D_MODEL, D_FF)
    assert seq % TOKEN_TILE == 0 and batch <= PREP_STEPS
    tile = TOKEN_TILE
    row = lambda a: a.reshape(1, -1).astype(F32)

    rb = D_MODEL // PREP_STEPS
    pool_split = PREP_STEPS // N_POOL
    prb = CH // pool_split
    assert rb % BF16_SUBLANES == 0 and prb % BF16_SUBLANES == 0
    last_batch = lambda i: (jnp.minimum(i, batch - 1), 0, 0)
    rows_of = lambda i: (i, 0)
    pool_rows_of = lambda i: (i // pool_split, i % pool_split, 0)
    prep_resident = [_nbytes((D_MODEL, 2 * D_MODEL), F32)]
    prep_streamed = [
        _nbytes((N_MEM, D_MODEL), F32), _nbytes((rb, D_IN), F32), 3 * _nbytes((rb, D_MODEL), F32),
        2 * _nbytes((N_MEM, D_MODEL), BF16), _nbytes((rb, D_IN), BF16), 3 * _nbytes((rb, D_MODEL), BF16)]
    prep_scratch = [_nbytes((D_MODEL, 2 * D_MODEL), BF16)]
    kt, v, w_in_b, w_co_b, w_xo_b, w_out_b, w_pool_b = pl.pallas_call(
        functools.partial(_prep_kernel, batch=batch),
        grid=(PREP_STEPS,),
        in_specs=[
            pl.BlockSpec((1, N_MEM, D_MODEL), last_batch),
            _resident((1, D_MODEL)),
            _resident((D_MODEL, 2 * D_MODEL)),
            pl.BlockSpec((rb, D_IN), rows_of),
            pl.BlockSpec((rb, D_MODEL), rows_of),
            pl.BlockSpec((rb, D_MODEL), rows_of),
            pl.BlockSpec((rb, D_MODEL), rows_of),
            pl.BlockSpec((1, prb, CH), pool_rows_of),
        ],
        out_specs=[
            pl.BlockSpec((1, D_MODEL, N_MEM), last_batch),
            pl.BlockSpec((1, N_MEM, D_MODEL), last_batch),
            pl.BlockSpec((rb, D_IN), rows_of),
            pl.BlockSpec((rb, D_MODEL), rows_of),
            pl.BlockSpec((rb, D_MODEL), rows_of),
            pl.BlockSpec((rb, D_MODEL), rows_of),
            pl.BlockSpec((1, prb, CH), pool_rows_of),
        ],
        out_shape=[
            jax.ShapeDtypeStruct((batch, D_MODEL, N_MEM), BF16),
            jax.ShapeDtypeStruct((batch, N_MEM, D_MODEL), BF16),
            jax.ShapeDtypeStruct((D_MODEL, D_IN), BF16),
            jax.ShapeDtypeStruct((D_MODEL, D_MODEL), BF16),
            jax.ShapeDtypeStruct((D_MODEL, D_MODEL), BF16),
            jax.ShapeDtypeStruct((D_MODEL, D_MODEL), BF16),
            jax.ShapeDtypeStruct((N_POOL, CH, CH), BF16),
        ],
        scratch_shapes=[pltpu.VMEM((D_MODEL, 2 * D_MODEL), BF16)],
        compiler_params=pltpu.CompilerParams(
            dimension_semantics=("arbitrary",),
            vmem_limit_bytes=_vmem_limit(prep_resident, prep_streamed, prep_scratch)),
        name="prep",
    )(mem, row(norm_mem[0]), w_kv[0], w_in[0], w_conv_out[0], w_xattn_out[0], w_out[0], w_pool[0])

    n_tiles = seq // tile
    n_steps = batch * n_tiles
    gu_rb = D_MODEL // n_steps
    d_steps = n_steps // 2
    d_rb = D_FF // d_steps
    assert gu_rb % BF16_SUBLANES == 0 and d_rb % BF16_SUBLANES == 0
    step_rows = lambda b, t: (b * n_tiles + t, 0)
    down_rows = lambda b, t: (jnp.minimum(b * n_tiles + t, d_steps - 1), 0)
    mixer_resident = [
        _nbytes((D_MODEL, D_IN), BF16), 3 * _nbytes((D_MODEL, D_MODEL), BF16),
        _nbytes((N_POOL, CH, CH), BF16)]
    mixer_streamed = [
        2 * _nbytes((tile, D_MODEL), F32), 2 * _nbytes((N_MEM, D_MODEL), BF16),
        2 * _nbytes((gu_rb, D_FF), F32), _nbytes((d_rb, D_MODEL), F32),
        2 * _nbytes((gu_rb, D_FF), BF16), _nbytes((d_rb, D_MODEL), BF16)]
    mixer_scratch = [
        _nbytes((tile, D_MODEL), BF16), 2 * _nbytes((HALO + tile, D_MODEL), F32),
        _nbytes((tile, D_MODEL), BF16), _nbytes((tile, D_MODEL), F32)]
    x1, w_gate_b, w_up_b, w_down_b = pl.pallas_call(
        _mixer_kernel,
        grid=(batch, n_tiles),
        in_specs=[
            pl.BlockSpec((1, tile, D_MODEL), lambda b, t: (b, t, 0)),
            pl.BlockSpec((1, D_MODEL, N_MEM), lambda b, t: (b, 0, 0)),
            pl.BlockSpec((1, N_MEM, D_MODEL), lambda b, t: (b, 0, 0)),
            _resident((1, D_MODEL)),
            _resident((D_MODEL, D_IN)),
            _resident((CONV_K, D_MODEL)),
            _resident((D_MODEL, D_MODEL)),
            _resident((N_POOL, CH, CH)),
            _resident((1, D_MODEL)),
            _resident((D_MODEL, D_MODEL)),
            _resident((D_MODEL, D_MODEL)),
            pl.BlockSpec((gu_rb, D_FF), step_rows),
            pl.BlockSpec((gu_rb, D_FF), step_rows),
            pl.BlockSpec((d_rb, D_MODEL), down_rows),
        ],
        out_specs=[
            pl.BlockSpec((1, tile, D_MODEL), lambda b, t: (b, t, 0)),
            pl.BlockSpec((gu_rb, D_FF), step_rows),
            pl.BlockSpec((gu_rb, D_FF), step_rows),
            pl.BlockSpec((d_rb, D_MODEL), down_rows),
        ],
        out_shape=[
            jax.ShapeDtypeStruct((batch, seq, D_MODEL), F32),
            jax.ShapeDtypeStruct((D_MODEL, D_FF), BF16),
            jax.ShapeDtypeStruct((D_MODEL, D_FF), BF16),
            jax.ShapeDtypeStruct((D_FF, D_MODEL), BF16),
        ],
        scratch_shapes=[
            pltpu.VMEM((tile, D_MODEL), BF16),
            pltpu.VMEM((HALO + tile, D_MODEL), F32),
            pltpu.VMEM((HALO + tile, D_MODEL), F32),
            pltpu.VMEM((tile, D_MODEL), BF16),
            pltpu.VMEM((tile, D_MODEL), F32),
        ],
        compiler_params=pltpu.CompilerParams(
            dimension_semantics=("arbitrary", "arbitrary"),
            vmem_limit_bytes=_vmem_limit(mixer_resident, mixer_streamed, mixer_scratch)),
        name="mixer",
    )(x, kt, v, row(norm_mix[0]), w_in_b, conv_w[0].astype(F32), w_co_b, w_pool_b,
      row(pool_scale[0]), w_xo_b, w_out_b, w_gate[0], w_up[0], w_down[0])

    n_tok = batch * seq
    ffn_resident = [3 * _nbytes((D_MODEL, D_FF), BF16)]
    ffn_streamed = [2 * _nbytes((tile, D_MODEL), F32)]
    ffn_scratch = [_nbytes((tile, D_MODEL), BF16), _nbytes((tile, D_FF), BF16), _nbytes((tile, D_MODEL), F32)]
    out = pl.pallas_call(
        _ffn_kernel,
        grid=(n_tok // tile,),
        in_specs=[
            pl.BlockSpec((tile, D_MODEL), lambda i: (i, 0)),
            _resident((1, D_MODEL)),
            _resident((D_MODEL, D_FF)),
            _resident((D_MODEL, D_FF)),
            _resident((D_FF, D_MODEL)),
            _resident((1, D_MODEL)),
        ],
        out_specs=pl.BlockSpec((tile, D_MODEL), lambda i: (i, 0)),
        out_shape=jax.ShapeDtypeStruct((n_tok, D_MODEL), F32),
        scratch_shapes=[
            pltpu.VMEM((tile, D_MODEL), BF16),
            pltpu.VMEM((tile, D_FF), BF16),
            pltpu.VMEM((tile, D_MODEL), F32),
        ],
        compiler_params=pltpu.CompilerParams(
            dimension_semantics=("arbitrary",),
            vmem_limit_bytes=_vmem_limit(ffn_resident, ffn_streamed, ffn_scratch)),
        name="ffn",
    )(x1.reshape(n_tok, D_MODEL), row(norm_ffn[0]), w_gate_b, w_up_b, w_down_b, row(norm_final))
    return out.reshape(batch, seq, D_MODEL)
```

```python
import functools

import jax
import jax.numpy as jnp
from jax import lax
from jax.experimental import pallas as pl
from jax.experimental.pallas import tpu as pltpu

F32 = jnp.float32
BF16 = jnp.bfloat16

D_MODEL = 1024
N_MEM = 256
CONV_K = 3
POOL_WINDOWS = (2, 4, 8, 16)
N_POOL = len(POOL_WINDOWS)
X_HEADS = 4
X_HEAD_DIM = D_MODEL // X_HEADS
D_FF = 2816
EPS = 1e-6

COL_B_A, COL_C_A, COL_U_A, COL_U_P, COL_Q_X, COL_G_A, COL_G_P, COL_G_X = (
    i * D_MODEL for i in range(8))
D_IN = 8 * D_MODEL

V7X_MXU_WIDTH = 256
V7X_VMEM_BYTES = 64 * 1024 * 1024
V7X_SCOPED_VMEM_BYTES = V7X_VMEM_BYTES - 4 * 1024 * 1024
BF16_SUBLANES = 16
HALO = 16
TOKEN_TILE = 512
STEP_TILES = 2
CH = V7X_MXU_WIDTH
N_CH = D_MODEL // CH
PREP_STEPS = 8


def _rms_norm(x, gain):
    ms = jnp.mean(x * x, axis=-1, keepdims=True)
    return (x * lax.rsqrt(ms + EPS)) * gain


def _sigmoid(x):
    return 1.0 / (1.0 + jnp.exp(-x))


def _dot(a, b):
    return jnp.dot(a, b, preferred_element_type=F32)


def _cols(j):
    return slice(j * CH, (j + 1) * CH)


def _prep_kernel(mem_ref, gain_ref, wkv_ref, win_ref, wco_ref, wxo_ref, wout_ref, wpool_ref,
                 kt_ref, v_ref, win_o, wco_o, wxo_o, wout_o, wpool_o, wkv_buf, *, batch):
    i = pl.program_id(0)

    win_o[...] = win_ref[...].astype(BF16)
    wco_o[...] = wco_ref[...].astype(BF16)
    wxo_o[...] = wxo_ref[...].astype(BF16)
    wout_o[...] = wout_ref[...].astype(BF16)
    wpool_o[...] = wpool_ref[...].astype(BF16)

    @pl.when(i == 0)
    def _():
        wkv_buf[...] = wkv_ref[...].astype(BF16)

    @pl.when(i < batch)
    def _():
        mn = _rms_norm(mem_ref[0], gain_ref[...]).astype(BF16)
        scale = X_HEAD_DIM ** -0.5
        for h in range(X_HEADS):
            k = _dot(mn, wkv_buf[:, _cols(h)]) * scale
            kt_ref[0, _cols(h), :] = k.T.astype(BF16)
        for j in range(N_CH):
            v_ref[0, :, _cols(j)] = _dot(mn, wkv_buf[:, D_MODEL + j * CH:D_MODEL + (j + 1) * CH]).astype(BF16)


def _mixer_tile(first_pos, x_ref, o_ref, h_buf, kt_ref, v_ref, gain_ref, win_ref, convw_ref,
                wco_ref, wpool_ref, pscale_ref, wxo_ref, wout_ref, cu_buf, up_buf, z_buf):
    tile = x_ref.shape[0]

    h_buf[...] = _rms_norm(x_ref[...], gain_ref[...]).astype(BF16)

    def proj(col0, j):
        return _dot(h_buf[...], win_ref[:, col0 + j * CH:col0 + (j + 1) * CH])

    for j in range(N_CH):
        cu_buf[HALO:, _cols(j)] = proj(COL_C_A, j) * proj(COL_U_A, j)
    for j in range(N_CH):
        cw = convw_ref[:, _cols(j)]
        conv = cw[CONV_K - 1:CONV_K] * cu_buf[HALO:, _cols(j)]
        for k in range(CONV_K - 1):
            lag = CONV_K - 1 - k
            conv = conv + cw[k:k + 1] * cu_buf[HALO - lag:HALO - lag + tile, _cols(j)]
        z_buf[:, _cols(j)] = (proj(COL_B_A, j) * conv).astype(BF16)
    cu_buf[0:HALO, :] = cu_buf[tile:tile + HALO, :]
    for j in range(N_CH):
        y = _dot(z_buf[...], wco_ref[:, _cols(j)])
        o_ref[:, _cols(j)] = _sigmoid(proj(COL_G_A, j)) * y

    for g in range(N_POOL):
        up_buf[HALO:, _cols(g)] = proj(COL_U_P, g)
    pos = first_pos + 1 + lax.broadcasted_iota(jnp.int32, (tile, 1), 0)
    for g, w in enumerate(POOL_WINDOWS):
        e = up_buf[:, _cols(g)]
        s = e
        k = 1
        while k < w:
            s = s + pltpu.roll(s, k, 0)
            k *= 2
        inv_cnt = 1.0 / jnp.minimum(pos, w).astype(F32)
        pooled = s[HALO:] * inv_cnt - e[HALO:]
        y = _dot(pooled.astype(BF16), wpool_ref[g]) * pscale_ref[:, _cols(g)]
        o_ref[:, _cols(g)] += _sigmoid(proj(COL_G_P, g)) * y
    up_buf[0:HALO, :] = up_buf[tile:tile + HALO, :]

    heads = range(X_HEADS)
    q = [proj(COL_Q_X, h).astype(BF16) for h in heads]
    s = [_dot(q[h], kt_ref[0, _cols(h), :]) for h in heads]
    p = [jnp.exp(s[h] - jnp.max(s[h], axis=-1, keepdims=True)) for h in heads]
    inv_l = [1.0 / jnp.sum(p[h], axis=-1, keepdims=True) for h in heads]
    o = [_dot(p[h].astype(BF16), v_ref[0, :, _cols(h)]) for h in heads]
    for h in heads:
        z_buf[:, _cols(h)] = (o[h] * inv_l[h]).astype(BF16)
    for j in range(N_CH):
        y = _dot(z_buf[...], wxo_ref[:, _cols(j)])
        o_ref[:, _cols(j)] += _sigmoid(proj(COL_G_X, j)) * y

    h_buf[...] = o_ref[...].astype(BF16)
    for j in range(N_CH):
        o_ref[:, _cols(j)] = x_ref[:, _cols(j)] + _dot(h_buf[...], wout_ref[:, _cols(j)])


def _mixer_kernel(x_ref, kt_ref, v_ref, gain_ref, win_ref, convw_ref, wco_ref, wpool_ref,
                  pscale_ref, wxo_ref, wout_ref, wg_ref, wu_ref, wd_ref,
                  o_ref, wg_o, wu_o, wd_o,
                  h_buf, cu_buf, up_buf, z_buf):
    t = pl.program_id(1)
    step_rows = x_ref.shape[1]
    tile = h_buf.shape[1]

    @pl.when(t == 0)
    def _():
        cu_buf[0:HALO, :] = jnp.zeros((HALO, D_MODEL), F32)
        up_buf[0:HALO, :] = jnp.zeros((HALO, D_MODEL), F32)

    for i in range(step_rows // tile):
        rows = pl.ds(i * tile, tile)
        _mixer_tile(t * step_rows + i * tile, x_ref.at[0, rows], o_ref.at[0, rows], h_buf.at[i],
                    kt_ref, v_ref, gain_ref, win_ref, convw_ref, wco_ref, wpool_ref, pscale_ref,
                    wxo_ref, wout_ref, cu_buf, up_buf, z_buf)

    wg_o[...] = wg_ref[...].astype(BF16)
    wu_o[...] = wu_ref[...].astype(BF16)
    wd_o[...] = wd_ref[...].astype(BF16)


def _ffn_tile(x_ref, o_ref, h_buf, a_buf, gain_ref, wg_ref, wu_ref, wd_ref, gfin_ref):
    h_buf[...] = _rms_norm(x_ref[...], gain_ref[...]).astype(BF16)
    for c in range(D_FF // CH):
        g = _dot(h_buf[...], wg_ref[:, _cols(c)])
        u = _dot(h_buf[...], wu_ref[:, _cols(c)])
        a_buf[:, _cols(c)] = (g * _sigmoid(g) * u).astype(BF16)
    for j in range(N_CH):
        o_ref[:, _cols(j)] = x_ref[:, _cols(j)] + _dot(a_buf[...], wd_ref[:, _cols(j)])
    o_ref[...] = _rms_norm(o_ref[...], gfin_ref[...])


def _ffn_kernel(x_ref, gain_ref, wg_ref, wu_ref, wd_ref, gfin_ref, o_ref, h_buf, a_buf):
    tile = h_buf.shape[1]
    for i in range(x_ref.shape[0] // tile):
        rows = pl.ds(i * tile, tile)
        _ffn_tile(x_ref.at[rows], o_ref.at[rows], h_buf.at[i], a_buf.at[i],
                  gain_ref, wg_ref, wu_ref, wd_ref, gfin_ref)


def _resident(shape):
    return pl.BlockSpec(shape, lambda *_: (0,) * len(shape), pipeline_mode=pl.Buffered(1))


def _nbytes(shape, dtype):
    n = 1
    for s in shape:
        n *= s
    return n * jnp.dtype(dtype).itemsize


def _vmem_limit(resident, streamed, scratch):
    need = sum(resident) + 2 * sum(streamed) + sum(scratch)
    assert need <= V7X_SCOPED_VMEM_BYTES, need
    return V7X_SCOPED_VMEM_BYTES


def kernel(x, mem, norm_mix, w_in, conv_w, w_conv_out, w_pool, pool_scale, norm_mem, w_kv,
           w_xattn_out, w_out, norm_ffn, w_gate, w_up, w_down, norm_final):
    batch, seq, d = x.shape
    assert d == D_MODEL and mem.shape == (batch, N_MEM, D_MODEL)
    assert w_in.shape == (1, D_MODEL, D_IN) and w_gate.shape == (1, D_MODEL, D_FF)
    tile = TOKEN_TILE
    step_rows = STEP_TILES * tile
    assert seq % step_rows == 0 and batch <= PREP_STEPS
    row = lambda a: a.reshape(1, -1).astype(F32)

    rb = D_MODEL // PREP_STEPS
    pool_split = PREP_STEPS // N_POOL
    prb = CH // pool_split
    assert rb % BF16_SUBLANES == 0 and prb % BF16_SUBLANES == 0
    last_batch = lambda i: (jnp.minimum(i, batch - 1), 0, 0)
    rows_of = lambda i: (i, 0)
    pool_rows_of = lambda i: (i // pool_split, i % pool_split, 0)
    prep_resident = [_nbytes((D_MODEL, 2 * D_MODEL), F32)]
    prep_streamed = [
        _nbytes((N_MEM, D_MODEL), F32), _nbytes((rb, D_IN), F32), 3 * _nbytes((rb, D_MODEL), F32),
        2 * _nbytes((N_MEM, D_MODEL), BF16), _nbytes((rb, D_IN), BF16), 3 * _nbytes((rb, D_MODEL), BF16)]
    prep_scratch = [_nbytes((D_MODEL, 2 * D_MODEL), BF16)]
    kt, v, w_in_b, w_co_b, w_xo_b, w_out_b, w_pool_b = pl.pallas_call(
        functools.partial(_prep_kernel, batch=batch),
        grid=(PREP_STEPS,),
        in_specs=[
            pl.BlockSpec((1, N_MEM, D_MODEL), last_batch),
            _resident((1, D_MODEL)),
            _resident((D_MODEL, 2 * D_MODEL)),
            pl.BlockSpec((rb, D_IN), rows_of),
            pl.BlockSpec((rb, D_MODEL), rows_of),
            pl.BlockSpec((rb, D_MODEL), rows_of),
            pl.BlockSpec((rb, D_MODEL), rows_of),
            pl.BlockSpec((1, prb, CH), pool_rows_of),
        ],
        out_specs=[
            pl.BlockSpec((1, D_MODEL, N_MEM), last_batch),
            pl.BlockSpec((1, N_MEM, D_MODEL), last_batch),
            pl.BlockSpec((rb, D_IN), rows_of),
            pl.BlockSpec((rb, D_MODEL), rows_of),
            pl.BlockSpec((rb, D_MODEL), rows_of),
            pl.BlockSpec((rb, D_MODEL), rows_of),
            pl.BlockSpec((1, prb, CH), pool_rows_of),
        ],
        out_shape=[
            jax.ShapeDtypeStruct((batch, D_MODEL, N_MEM), BF16),
            jax.ShapeDtypeStruct((batch, N_MEM, D_MODEL), BF16),
            jax.ShapeDtypeStruct((D_MODEL, D_IN), BF16),
            jax.ShapeDtypeStruct((D_MODEL, D_MODEL), BF16),
            jax.ShapeDtypeStruct((D_MODEL, D_MODEL), BF16),
            jax.ShapeDtypeStruct((D_MODEL, D_MODEL), BF16),
            jax.ShapeDtypeStruct((N_POOL, CH, CH), BF16),
        ],
        scratch_shapes=[pltpu.VMEM((D_MODEL, 2 * D_MODEL), BF16)],
        compiler_params=pltpu.CompilerParams(
            dimension_semantics=("arbitrary",),
            vmem_limit_bytes=_vmem_limit(prep_resident, prep_streamed, prep_scratch)),
        name="prep",
    )(mem, row(norm_mem[0]), w_kv[0], w_in[0], w_conv_out[0], w_xattn_out[0], w_out[0], w_pool[0])

    n_t = seq // step_rows
    n_steps = batch * n_t
    gu_rb = D_MODEL // n_steps
    d_rb = D_FF // n_steps
    assert gu_rb % BF16_SUBLANES == 0 and d_rb % BF16_SUBLANES == 0
    step_block = lambda b, t: (b * n_t + t, 0)
    mixer_resident = [
        _nbytes((D_MODEL, D_IN), BF16), 3 * _nbytes((D_MODEL, D_MODEL), BF16),
        _nbytes((N_POOL, CH, CH), BF16)]
    mixer_streamed = [
        2 * _nbytes((step_rows, D_MODEL), F32), 2 * _nbytes((N_MEM, D_MODEL), BF16),
        2 * _nbytes((gu_rb, D_FF), F32), _nbytes((d_rb, D_MODEL), F32),
        2 * _nbytes((gu_rb, D_FF), BF16), _nbytes((d_rb, D_MODEL), BF16)]
    mixer_scratch = [
        _nbytes((step_rows, D_MODEL), BF16), 2 * _nbytes((HALO + tile, D_MODEL), F32),
        _nbytes((tile, D_MODEL), BF16)]
    x1, w_gate_b, w_up_b, w_down_b = pl.pallas_call(
        _mixer_kernel,
        grid=(batch, n_t),
        in_specs=[
            pl.BlockSpec((1, step_rows, D_MODEL), lambda b, t: (b, t, 0)),
            pl.BlockSpec((1, D_MODEL, N_MEM), lambda b, t: (b, 0, 0)),
            pl.BlockSpec((1, N_MEM, D_MODEL), lambda b, t: (b, 0, 0)),
            _resident((1, D_MODEL)),
            _resident((D_MODEL, D_IN)),
            _resident((CONV_K, D_MODEL)),
            _resident((D_MODEL, D_MODEL)),
            _resident((N_POOL, CH, CH)),
            _resident((1, D_MODEL)),
            _resident((D_MODEL, D_MODEL)),
            _resident((D_MODEL, D_MODEL)),
            pl.BlockSpec((gu_rb, D_FF), step_block),
            pl.BlockSpec((gu_rb, D_FF), step_block),
            pl.BlockSpec((d_rb, D_MODEL), step_block),
        ],
        out_specs=[
            pl.BlockSpec((1, step_rows, D_MODEL), lambda b, t: (b, t, 0)),
            pl.BlockSpec((gu_rb, D_FF), step_block),
            pl.BlockSpec((gu_rb, D_FF), step_block),
            pl.BlockSpec((d_rb, D_MODEL), step_block),
        ],
        out_shape=[
            jax.ShapeDtypeStruct((batch, seq, D_MODEL), F32),
            jax.ShapeDtypeStruct((D_MODEL, D_FF), BF16),
            jax.ShapeDtypeStruct((D_MODEL, D_FF), BF16),
            jax.ShapeDtypeStruct((D_FF, D_MODEL), BF16),
        ],
        scratch_shapes=[
            pltpu.VMEM((STEP_TILES, tile, D_MODEL), BF16),
            pltpu.VMEM((HALO + tile, D_MODEL), F32),
            pltpu.VMEM((HALO + tile, D_MODEL), F32),
            pltpu.VMEM((tile, D_MODEL), BF16),
        ],
        compiler_params=pltpu.CompilerParams(
            dimension_semantics=("arbitrary", "arbitrary"),
            vmem_limit_bytes=_vmem_limit(mixer_resident, mixer_streamed, mixer_scratch)),
        name="mixer",
    )(x, kt, v, row(norm_mix[0]), w_in_b, conv_w[0].astype(F32), w_co_b, w_pool_b,
      row(pool_scale[0]), w_xo_b, w_out_b, w_gate[0], w_up[0], w_down[0])

    n_tok = batch * seq
    ffn_resident = [3 * _nbytes((D_MODEL, D_FF), BF16)]
    ffn_streamed = [2 * _nbytes((step_rows, D_MODEL), F32)]
    ffn_scratch = [_nbytes((step_rows, D_MODEL), BF16), _nbytes((step_rows, D_FF), BF16)]
    out = pl.pallas_call(
        _ffn_kernel,
        grid=(n_tok // step_rows,),
        in_specs=[
            pl.BlockSpec((step_rows, D_MODEL), lambda i: (i, 0)),
            _resident((1, D_MODEL)),
            _resident((D_MODEL, D_FF)),
            _resident((D_MODEL, D_FF)),
            _resident((D_FF, D_MODEL)),
            _resident((1, D_MODEL)),
        ],
        out_specs=pl.BlockSpec((step_rows, D_MODEL), lambda i: (i, 0)),
        out_shape=jax.ShapeDtypeStruct((n_tok, D_MODEL), F32),
        scratch_shapes=[
            pltpu.VMEM((STEP_TILES, tile, D_MODEL), BF16),
            pltpu.VMEM((STEP_TILES, tile, D_FF), BF16),
        ],
        compiler_params=pltpu.CompilerParams(
            dimension_semantics=("arbitrary",),
            vmem_limit_bytes=_vmem_limit(ffn_resident, ffn_streamed, ffn_scratch)),
        name="ffn",
    )(x1.reshape(n_tok, D_MODEL), row(norm_ffn[0]), w_gate_b, w_up_b, w_down_b, row(norm_final))
    return out.reshape(batch, seq, D_MODEL)
```

```python
import functools

import jax
import jax.numpy as jnp
from jax import lax
from jax.experimental import pallas as pl
from jax.experimental.pallas import tpu as pltpu

F32 = jnp.float32
BF16 = jnp.bfloat16

D_MODEL = 1024
N_MEM = 256
CONV_K = 3
POOL_WINDOWS = (2, 4, 8, 16)
N_POOL = len(POOL_WINDOWS)
X_HEADS = 4
X_HEAD_DIM = D_MODEL // X_HEADS
D_FF = 2816
EPS = 1e-6

SRC_B_A, SRC_C_A, SRC_U_A, SRC_U_P, SRC_Q_X, SRC_G_A, SRC_G_P, SRC_G_X = range(8)
D_IN = 8 * D_MODEL
MIX_SRC = (SRC_B_A, SRC_C_A, SRC_U_A, SRC_U_P, SRC_G_A, SRC_G_P, SRC_G_X)
COL_B_A, COL_C_A, COL_U_A, COL_U_P, COL_G_A, COL_G_P, COL_G_X = (
    i * D_MODEL for i in range(len(MIX_SRC)))
D_MIX = len(MIX_SRC) * D_MODEL

V7X_MXU_WIDTH = 256
V7X_VMEM_BYTES = 64 * 1024 * 1024
V7X_SCOPED_VMEM_BYTES = V7X_VMEM_BYTES - 4 * 1024 * 1024
BF16_SUBLANES = 16
HALO = 16
TOKEN_TILE = 512
MIXER_STEP_TILES = 1
FFN_STEP_TILES = 2
CH = V7X_MXU_WIDTH
N_CH = D_MODEL // CH
PREP_STEPS = 8

assert X_HEAD_DIM == CH and N_MEM == CH and D_MODEL // N_POOL == CH


def _rms_norm(x, gain):
    ms = jnp.mean(x * x, axis=-1, keepdims=True)
    return (x * lax.rsqrt(ms + EPS)) * gain


def _sigmoid(x):
    return 1.0 / (1.0 + jnp.exp(-x))


def _dot(a, b):
    return jnp.dot(a, b, preferred_element_type=F32)


def _dot_bt(a, b):
    return lax.dot_general(a, b, (((1,), (1,)), ((), ())), preferred_element_type=F32)


def _cols(j):
    return slice(j * CH, (j + 1) * CH)


def _prep_kernel(mem_ref, gain_ref, wkv_ref, wq_ref, wxo_ref, wpool_ref, pscale_ref,
                 win_ref, wco_ref, wout_ref,
                 sm_ref, vo_ref, win_o, wco_o, wout_o,
                 wkv_buf, wq_buf, wxo_buf, wpool_buf, *, batch):
    i = pl.program_id(0)

    @pl.when(i == 0)
    def _():
        wkv_buf[...] = wkv_ref[...].astype(BF16)
        wq_buf[...] = wq_ref[...].astype(BF16)
        wxo_buf[...] = wxo_ref[...].astype(BF16)
        wpool_buf[...] = wpool_ref[...].astype(BF16)

    for dst, src in enumerate(MIX_SRC):
        src_cols = slice(src * D_MODEL, (src + 1) * D_MODEL)
        dst_cols = slice(dst * D_MODEL, (dst + 1) * D_MODEL)
        if src != SRC_U_P:
            win_o[:, dst_cols] = win_ref[:, src_cols].astype(BF16)
            continue
        for g in range(N_POOL):
            w_up = win_ref[:, src * D_MODEL + g * CH:src * D_MODEL + (g + 1) * CH].astype(BF16)
            folded = _dot(w_up, wpool_buf[g]) * pscale_ref[:, _cols(g)]
            win_o[:, dst * D_MODEL + g * CH:dst * D_MODEL + (g + 1) * CH] = folded.astype(BF16)
    wco_o[...] = wco_ref[...].astype(BF16)
    wout_o[...] = wout_ref[...].astype(BF16)

    @pl.when(i < batch)
    def _():
        mn = _rms_norm(mem_ref[0], gain_ref[...]).astype(BF16)
        scale = X_HEAD_DIM ** -0.5
        for h in range(X_HEADS):
            k = (_dot(mn, wkv_buf[:, _cols(h)]) * scale).astype(BF16)
            sm_ref[0, :, _cols(h)] = _dot_bt(wq_buf[:, _cols(h)], k).astype(BF16)
            v = _dot(mn, wkv_buf[:, D_MODEL + h * CH:D_MODEL + (h + 1) * CH]).astype(BF16)
            vo_ref[0, _cols(h), :] = _dot(v, wxo_buf[_cols(h), :]).astype(BF16)


def _mixer_tile(first_pos, x_ref, o_ref, h_buf, sm_ref, vo_ref, gain_ref, win_ref, convw_ref,
                wco_ref, wout_ref, cu_buf, up_buf, z_buf):
    tile = x_ref.shape[0]

    h_buf[...] = _rms_norm(x_ref[...], gain_ref[...]).astype(BF16)

    def proj(col0, j):
        return _dot(h_buf[...], win_ref[:, col0 + j * CH:col0 + (j + 1) * CH])

    for j in range(N_CH):
        cu_buf[HALO:, _cols(j)] = proj(COL_C_A, j) * proj(COL_U_A, j)
    for j in range(N_CH):
        cw = convw_ref[:, _cols(j)]
        conv = cw[CONV_K - 1:CONV_K] * cu_buf[HALO:, _cols(j)]
        for k in range(CONV_K - 1):
            lag = CONV_K - 1 - k
            conv = conv + cw[k:k + 1] * cu_buf[HALO - lag:HALO - lag + tile, _cols(j)]
        z_buf[:, _cols(j)] = (proj(COL_B_A, j) * conv).astype(BF16)
    cu_buf[0:HALO, :] = cu_buf[tile:tile + HALO, :]
    for j in range(N_CH):
        y = _dot(z_buf[...], wco_ref[:, _cols(j)])
        o_ref[:, _cols(j)] = _sigmoid(proj(COL_G_A, j)) * y

    for g in range(N_POOL):
        up_buf[HALO:, _cols(g)] = proj(COL_U_P, g)
    pos = first_pos + 1 + lax.broadcasted_iota(jnp.int32, (tile, 1), 0)
    for g, w in enumerate(POOL_WINDOWS):
        e = up_buf[:, _cols(g)]
        s = e
        k = 1
        while k < w:
            s = s + pltpu.roll(s, k, 0)
            k *= 2
        inv_cnt = 1.0 / jnp.minimum(pos, w).astype(F32)
        y = s[HALO:] * inv_cnt - e[HALO:]
        o_ref[:, _cols(g)] += _sigmoid(proj(COL_G_P, g)) * y
    up_buf[0:HALO, :] = up_buf[tile:tile + HALO, :]

    for h in range(X_HEADS):
        s = _dot(h_buf[...], sm_ref[0, :, _cols(h)])
        p = jnp.exp(s - jnp.max(s, axis=-1, keepdims=True))
        inv_l = 1.0 / jnp.sum(p, axis=-1, keepdims=True)
        z_buf[:, _cols(h)] = (p * inv_l).astype(BF16)
    for j in range(N_CH):
        y = _dot(z_buf[...], vo_ref[0, :, _cols(j)])
        o_ref[:, _cols(j)] += _sigmoid(proj(COL_G_X, j)) * y

    h_buf[...] = o_ref[...].astype(BF16)
    for j in range(N_CH):
        o_ref[:, _cols(j)] = x_ref[:, _cols(j)] + _dot(h_buf[...], wout_ref[:, _cols(j)])


def _mixer_kernel(x_ref, sm_ref, vo_ref, gain_ref, win_ref, convw_ref, wco_ref, wout_ref,
                  wg_ref, wu_ref, wd_ref,
                  o_ref, wg_o, wu_o, wd_o,
                  h_buf, cu_buf, up_buf, z_buf):
    t = pl.program_id(1)
    step_rows = x_ref.shape[1]
    tile = h_buf.shape[1]

    @pl.when(t == 0)
    def _():
        cu_buf[0:HALO, :] = jnp.zeros((HALO, D_MODEL), F32)
        up_buf[0:HALO, :] = jnp.zeros((HALO, D_MODEL), F32)

    for i in range(step_rows // tile):
        rows = pl.ds(i * tile, tile)
        _mixer_tile(t * step_rows + i * tile, x_ref.at[0, rows], o_ref.at[0, rows], h_buf.at[i],
                    sm_ref, vo_ref, gain_ref, win_ref, convw_ref, wco_ref, wout_ref,
                    cu_buf, up_buf, z_buf)

    wg_o[...] = wg_ref[...].astype(BF16)
    wu_o[...] = wu_ref[...].astype(BF16)
    wd_o[...] = wd_ref[...].astype(BF16)


def _ffn_tile(x_ref, o_ref, h_buf, a_buf, gain_ref, wg_ref, wu_ref, wd_ref, gfin_ref):
    h_buf[...] = _rms_norm(x_ref[...], gain_ref[...]).astype(BF16)
    for c in range(D_FF // CH):
        g = _dot(h_buf[...], wg_ref[:, _cols(c)])
        u = _dot(h_buf[...], wu_ref[:, _cols(c)])
        a_buf[:, _cols(c)] = (g * _sigmoid(g) * u).astype(BF16)
    for j in range(N_CH):
        o_ref[:, _cols(j)] = x_ref[:, _cols(j)] + _dot(a_buf[...], wd_ref[:, _cols(j)])
    o_ref[...] = _rms_norm(o_ref[...], gfin_ref[...])


def _ffn_kernel(x_ref, gain_ref, wg_ref, wu_ref, wd_ref, gfin_ref, o_ref, h_buf, a_buf):
    tile = h_buf.shape[1]
    for i in range(x_ref.shape[0] // tile):
        rows = pl.ds(i * tile, tile)
        _ffn_tile(x_ref.at[rows], o_ref.at[rows], h_buf.at[i], a_buf.at[i],
                  gain_ref, wg_ref, wu_ref, wd_ref, gfin_ref)


def _resident(shape, index=None):
    index = (0,) * len(shape) if index is None else index
    return pl.BlockSpec(shape, lambda *_: index, pipeline_mode=pl.Buffered(1))


def _nbytes(shape, dtype):
    n = 1
    for s in shape:
        n *= s
    return n * jnp.dtype(dtype).itemsize


def _vmem_limit(resident, streamed, scratch):
    need = sum(resident) + 2 * sum(streamed) + sum(scratch)
    assert need <= V7X_SCOPED_VMEM_BYTES, need
    return V7X_SCOPED_VMEM_BYTES


def kernel(x, mem, norm_mix, w_in, conv_w, w_conv_out, w_pool, pool_scale, norm_mem, w_kv,
           w_xattn_out, w_out, norm_ffn, w_gate, w_up, w_down, norm_final):
    batch, seq, d = x.shape
    assert d == D_MODEL and mem.shape == (batch, N_MEM, D_MODEL)
    assert w_in.shape == (1, D_MODEL, D_IN) and w_gate.shape == (1, D_MODEL, D_FF)
    tile = TOKEN_TILE
    mixer_rows = MIXER_STEP_TILES * tile
    ffn_rows = FFN_STEP_TILES * tile
    assert seq % mixer_rows == 0 and (batch * seq) % ffn_rows == 0 and batch <= PREP_STEPS
    row = lambda a: a.reshape(1, -1).astype(F32)
    sq = (D_MODEL, D_MODEL)

    rb = D_MODEL // PREP_STEPS
    assert rb % BF16_SUBLANES == 0
    last_batch = lambda i: (jnp.minimum(i, batch - 1), 0, 0)
    rows_of = lambda i: (i, 0)
    prep_resident = [
        _nbytes((D_MODEL, 2 * D_MODEL), F32), 2 * _nbytes(sq, F32), _nbytes((N_POOL, CH, CH), F32)]
    prep_streamed = [
        _nbytes((N_MEM, D_MODEL), F32), _nbytes((rb, D_IN), F32), 2 * _nbytes((rb, D_MODEL), F32),
        2 * _nbytes(sq, BF16), _nbytes((rb, D_MIX), BF16), 2 * _nbytes((rb, D_MODEL), BF16)]
    prep_scratch = [
        _nbytes((D_MODEL, 2 * D_MODEL), BF16), 2 * _nbytes(sq, BF16), _nbytes((N_POOL, CH, CH), BF16)]
    sm, vo, w_in_b, w_co_b, w_out_b = pl.pallas_call(
        functools.partial(_prep_kernel, batch=batch),
        grid=(PREP_STEPS,),
        in_specs=[
            pl.BlockSpec((1, N_MEM, D_MODEL), last_batch),
            _resident((1, D_MODEL)),
            _resident((D_MODEL, 2 * D_MODEL)),
            _resident(sq, (0, SRC_Q_X)),
            _resident(sq),
            _resident((N_POOL, CH, CH)),
            _resident((1, D_MODEL)),
            pl.BlockSpec((rb, D_IN), rows_of),
            pl.BlockSpec((rb, D_MODEL), rows_of),
            pl.BlockSpec((rb, D_MODEL), rows_of),
        ],
        out_specs=[
            pl.BlockSpec((1,) + sq, last_batch),
            pl.BlockSpec((1,) + sq, last_batch),
            pl.BlockSpec((rb, D_MIX), rows_of),
            pl.BlockSpec((rb, D_MODEL), rows_of),
            pl.BlockSpec((rb, D_MODEL), rows_of),
        ],
        out_shape=[
            jax.ShapeDtypeStruct((batch,) + sq, BF16),
            jax.ShapeDtypeStruct((batch,) + sq, BF16),
            jax.ShapeDtypeStruct((D_MODEL, D_MIX), BF16),
            jax.ShapeDtypeStruct(sq, BF16),
            jax.ShapeDtypeStruct(sq, BF16),
        ],
        scratch_shapes=[
            pltpu.VMEM((D_MODEL, 2 * D_MODEL), BF16),
            pltpu.VMEM(sq, BF16),
            pltpu.VMEM(sq, BF16),
            pltpu.VMEM((N_POOL, CH, CH), BF16),
        ],
        compiler_params=pltpu.CompilerParams(
            dimension_semantics=("arbitrary",),
            vmem_limit_bytes=_vmem_limit(prep_resident, prep_streamed, prep_scratch)),
        name="prep",
    )(mem, row(norm_mem[0]), w_kv[0], w_in[0], w_xattn_out[0], w_pool[0], row(pool_scale[0]),
      w_in[0], w_conv_out[0], w_out[0])

    n_t = seq // mixer_rows
    n_steps = batch * n_t
    gu_rb = D_MODEL // n_steps
    d_steps = n_steps // 2
    d_rb = D_FF // d_steps
    assert gu_rb % BF16_SUBLANES == 0 and d_rb % BF16_SUBLANES == 0
    step_block = lambda b, t: (b * n_t + t, 0)
    down_block = lambda b, t: (jnp.minimum(b * n_t + t, d_steps - 1), 0)
    mixer_resident = [_nbytes((D_MODEL, D_MIX), BF16), 2 * _nbytes(sq, BF16)]
    mixer_streamed = [
        2 * _nbytes((mixer_rows, D_MODEL), F32), 2 * _nbytes(sq, BF16),
        2 * _nbytes((gu_rb, D_FF), F32), _nbytes((d_rb, D_MODEL), F32),
        2 * _nbytes((gu_rb, D_FF), BF16), _nbytes((d_rb, D_MODEL), BF16)]
    mixer_scratch = [
        _nbytes((mixer_rows, D_MODEL), BF16), 2 * _nbytes((HALO + tile, D_MODEL), F32),
        _nbytes((tile, D_MODEL), BF16)]
    x1, w_gate_b, w_up_b, w_down_b = pl.pallas_call(
        _mixer_kernel,
        grid=(batch, n_t),
        in_specs=[
            pl.BlockSpec((1, mixer_rows, D_MODEL), lambda b, t: (b, t, 0)),
            pl.BlockSpec((1,) + sq, lambda b, t: (b, 0, 0)),
            pl.BlockSpec((1,) + sq, lambda b, t: (b, 0, 0)),
            _resident((1, D_MODEL)),
            _resident((D_MODEL, D_MIX)),
            _resident((CONV_K, D_MODEL)),
            _resident(sq),
            _resident(sq),
            pl.BlockSpec((gu_rb, D_FF), step_block),
            pl.BlockSpec((gu_rb, D_FF), step_block),
            pl.BlockSpec((d_rb, D_MODEL), down_block),
        ],
        out_specs=[
            pl.BlockSpec((1, mixer_rows, D_MODEL), lambda b, t: (b, t, 0)),
            pl.BlockSpec((gu_rb, D_FF), step_block),
            pl.BlockSpec((gu_rb, D_FF), step_block),
            pl.BlockSpec((d_rb, D_MODEL), down_block),
        ],
        out_shape=[
            jax.ShapeDtypeStruct((batch, seq, D_MODEL), F32),
            jax.ShapeDtypeStruct((D_MODEL, D_FF), BF16),
            jax.ShapeDtypeStruct((D_MODEL, D_FF), BF16),
            jax.ShapeDtypeStruct((D_FF, D_MODEL), BF16),
        ],
        scratch_shapes=[
            pltpu.VMEM((MIXER_STEP_TILES, tile, D_MODEL), BF16),
            pltpu.VMEM((HALO + tile, D_MODEL), F32),
            pltpu.VMEM((HALO + tile, D_MODEL), F32),
            pltpu.VMEM((tile, D_MODEL), BF16),
        ],
        compiler_params=pltpu.CompilerParams(
            dimension_semantics=("arbitrary", "arbitrary"),
            vmem_limit_bytes=_vmem_limit(mixer_resident, mixer_streamed, mixer_scratch)),
        name="mixer",
    )(x, sm, vo, row(norm_mix[0]), w_in_b, conv_w[0].astype(F32), w_co_b, w_out_b,
      w_gate[0], w_up[0], w_down[0])

    n_tok = batch * seq
    ffn_resident = [3 * _nbytes((D_MODEL, D_FF), BF16)]
    ffn_streamed = [2 * _nbytes((ffn_rows, D_MODEL), F32)]
    ffn_scratch = [_nbytes((ffn_rows, D_MODEL), BF16), _nbytes((ffn_rows, D_FF), BF16)]
    out = pl.pallas_call(
        _ffn_kernel,
        grid=(n_tok // ffn_rows,),
        in_specs=[
            pl.BlockSpec((ffn_rows, D_MODEL), lambda i: (i, 0)),
            _resident((1, D_MODEL)),
            _resident((D_MODEL, D_FF)),
            _resident((D_MODEL, D_FF)),
            _resident((D_FF, D_MODEL)),
            _resident((1, D_MODEL)),
        ],
        out_specs=pl.BlockSpec((ffn_rows, D_MODEL), lambda i: (i, 0)),
        out_shape=jax.ShapeDtypeStruct((n_tok, D_MODEL), F32),
        scratch_shapes=[
            pltpu.VMEM((FFN_STEP_TILES, tile, D_MODEL), BF16),
            pltpu.VMEM((FFN_STEP_TILES, tile, D_FF), BF16),
        ],
        compiler_params=pltpu.CompilerParams(
            dimension_semantics=("arbitrary",),
            vmem_limit_bytes=_vmem_limit(ffn_resident, ffn_streamed, ffn_scratch)),
        name="ffn",
    )(x1.reshape(n_tok, D_MODEL), row(norm_ffn[0]), w_gate_b, w_up_b, w_down_b, row(norm_final))
    return out.reshape(batch, seq, D_MODEL)
```

```python
import functools

import jax
import jax.numpy as jnp
from jax import lax
from jax.experimental import pallas as pl
from jax.experimental.pallas import tpu as pltpu

F32 = jnp.float32
BF16 = jnp.bfloat16

D_MODEL = 1024
N_MEM = 256
CONV_K = 3
POOL_WINDOWS = (2, 4, 8, 16)
N_POOL = len(POOL_WINDOWS)
X_HEADS = 4
X_HEAD_DIM = D_MODEL // X_HEADS
D_FF = 2816
EPS = 1e-6

SRC_B_A, SRC_C_A, SRC_U_A, SRC_U_P, SRC_Q_X, SRC_G_A, SRC_G_P, SRC_G_X = range(8)
D_IN = 8 * D_MODEL
MIX_SRC = (SRC_B_A, SRC_C_A, SRC_U_A, SRC_U_P, SRC_G_A, SRC_G_P, SRC_G_X)
COL_B_A, COL_C_A, COL_U_A, COL_U_P, COL_G_A, COL_G_P, COL_G_X = (
    i * D_MODEL for i in range(len(MIX_SRC)))
D_MIX = len(MIX_SRC) * D_MODEL

V7X_MXU_WIDTH = 256
V7X_VMEM_BYTES = 64 * 1024 * 1024
V7X_SCOPED_VMEM_BYTES = V7X_VMEM_BYTES - 4 * 1024 * 1024
BF16_SUBLANES = 16
HALO = 16
TOKEN_TILE = 512
FFN_STEP_TILES = 2
CH = V7X_MXU_WIDTH
N_CH = D_MODEL // CH
PREP_STEPS = 8

assert X_HEAD_DIM == CH and N_MEM == CH and D_MODEL // N_POOL == CH


def _rms_norm(x, gain):
    ms = jnp.mean(x * x, axis=-1, keepdims=True)
    return (x * lax.rsqrt(ms + EPS)) * gain


def _sigmoid(x):
    return 1.0 / (1.0 + jnp.exp(-x))


def _dot(a, b):
    return jnp.dot(a, b, preferred_element_type=F32)


def _dot_bt(a, b):
    return lax.dot_general(a, b, (((1,), (1,)), ((), ())), preferred_element_type=F32)


def _cols(j):
    return slice(j * CH, (j + 1) * CH)


def _prep_kernel(mem_ref, gain_ref, wkv_ref, wq_ref, wxo_ref, wpool_ref, pscale_ref,
                 win_ref, wco_ref, wout_ref,
                 sm_ref, vo_ref, win_o, wco_o, wout_o,
                 wkv_buf, wq_buf, wxo_buf, wpool_buf, *, batch):
    i = pl.program_id(0)

    @pl.when(i == 0)
    def _():
        wkv_buf[...] = wkv_ref[...].astype(BF16)
        wq_buf[...] = wq_ref[...].astype(BF16)
        wxo_buf[...] = wxo_ref[...].astype(BF16)
        wpool_buf[...] = wpool_ref[...].astype(BF16)

    for dst, src in enumerate(MIX_SRC):
        src_cols = slice(src * D_MODEL, (src + 1) * D_MODEL)
        dst_cols = slice(dst * D_MODEL, (dst + 1) * D_MODEL)
        if src != SRC_U_P:
            win_o[:, dst_cols] = win_ref[:, src_cols].astype(BF16)
            continue
        for g in range(N_POOL):
            w_up = win_ref[:, src * D_MODEL + g * CH:src * D_MODEL + (g + 1) * CH].astype(BF16)
            folded = _dot(w_up, wpool_buf[g]) * pscale_ref[:, _cols(g)]
            win_o[:, dst * D_MODEL + g * CH:dst * D_MODEL + (g + 1) * CH] = folded.astype(BF16)
    wco_o[...] = wco_ref[...].astype(BF16)
    wout_o[...] = wout_ref[...].astype(BF16)

    @pl.when(i < batch)
    def _():
        mn = _rms_norm(mem_ref[0], gain_ref[...]).astype(BF16)
        scale = X_HEAD_DIM ** -0.5
        for h in range(X_HEADS):
            k = (_dot(mn, wkv_buf[:, _cols(h)]) * scale).astype(BF16)
            sm_ref[0, :, _cols(h)] = _dot_bt(wq_buf[:, _cols(h)], k).astype(BF16)
            v = _dot(mn, wkv_buf[:, D_MODEL + h * CH:D_MODEL + (h + 1) * CH]).astype(BF16)
            vo_ref[0, _cols(h), :] = _dot(v, wxo_buf[_cols(h), :]).astype(BF16)


def _mixer_kernel(x_ref, sm_ref, vo_ref, gain_ref, win_ref, convw_ref, wco_ref, wout_ref,
                  wg_ref, wu_ref, wd_ref,
                  o_ref, wg_o, wu_o, wd_o,
                  h_buf, cu_buf, up_buf, z_buf):
    t = pl.program_id(1)
    tile = x_ref.shape[1]
    x_ref = x_ref.at[0]
    o_ref = o_ref.at[0]

    @pl.when(t == 0)
    def _():
        cu_buf[0:HALO, :] = jnp.zeros((HALO, D_MODEL), F32)
        up_buf[0:HALO, :] = jnp.zeros((HALO, D_MODEL), F32)

    h_buf[...] = _rms_norm(x_ref[...], gain_ref[...]).astype(BF16)

    def proj(col0, j):
        return _dot(h_buf[...], win_ref[:, col0 + j * CH:col0 + (j + 1) * CH])

    for j in range(N_CH):
        cu_buf[HALO:, _cols(j)] = proj(COL_C_A, j) * proj(COL_U_A, j)
    for j in range(N_CH):
        cw = convw_ref[:, _cols(j)]
        conv = cw[CONV_K - 1:CONV_K] * cu_buf[HALO:, _cols(j)]
        for k in range(CONV_K - 1):
            lag = CONV_K - 1 - k
            conv = conv + cw[k:k + 1] * cu_buf[HALO - lag:HALO - lag + tile, _cols(j)]
        z_buf[:, _cols(j)] = (proj(COL_B_A, j) * conv).astype(BF16)
    cu_buf[0:HALO, :] = cu_buf[tile:tile + HALO, :]
    for j in range(N_CH):
        y = _dot(z_buf[...], wco_ref[:, _cols(j)])
        o_ref[:, _cols(j)] = _sigmoid(proj(COL_G_A, j)) * y

    for g in range(N_POOL):
        up_buf[HALO:, _cols(g)] = proj(COL_U_P, g)
    pos = t * tile + 1 + lax.broadcasted_iota(jnp.int32, (tile, 1), 0)
    scores = []
    for g, w in enumerate(POOL_WINDOWS):
        e = up_buf[:, _cols(g)]
        s = e
        k = 1
        while k < w:
            s = s + pltpu.roll(s, k, 0)
            k *= 2
        inv_cnt = 1.0 / jnp.minimum(pos, w).astype(F32)
        y = s[HALO:] * inv_cnt - e[HALO:]
        o_ref[:, _cols(g)] += _sigmoid(proj(COL_G_P, g)) * y
        scores.append(_dot(h_buf[...], sm_ref[0, :, _cols(g)]))
    up_buf[0:HALO, :] = up_buf[tile:tile + HALO, :]

    for h in range(X_HEADS):
        s = scores[h]
        p = jnp.exp(s - jnp.max(s, axis=-1, keepdims=True))
        inv_l = 1.0 / jnp.sum(p, axis=-1, keepdims=True)
        z_buf[:, _cols(h)] = (p * inv_l).astype(BF16)
    for j in range(N_CH):
        y = _dot(z_buf[...], vo_ref[0, :, _cols(j)])
        o_ref[:, _cols(j)] += _sigmoid(proj(COL_G_X, j)) * y

    h_buf[...] = o_ref[...].astype(BF16)
    for j in range(N_CH):
        o_ref[:, _cols(j)] = x_ref[:, _cols(j)] + _dot(h_buf[...], wout_ref[:, _cols(j)])

    wg_o[...] = wg_ref[...].astype(BF16)
    wu_o[...] = wu_ref[...].astype(BF16)
    wd_o[...] = wd_ref[...].astype(BF16)


def _ffn_tile(x_ref, o_ref, h_buf, a_buf, gain_ref, wg_ref, wu_ref, wd_ref, gfin_ref):
    h_buf[...] = _rms_norm(x_ref[...], gain_ref[...]).astype(BF16)
    for c in range(D_FF // CH):
        g = _dot(h_buf[...], wg_ref[:, _cols(c)])
        u = _dot(h_buf[...], wu_ref[:, _cols(c)])
        a_buf[:, _cols(c)] = (g * _sigmoid(g) * u).astype(BF16)
    for j in range(N_CH):
        o_ref[:, _cols(j)] = x_ref[:, _cols(j)] + _dot(a_buf[...], wd_ref[:, _cols(j)])
    o_ref[...] = _rms_norm(o_ref[...], gfin_ref[...])


def _ffn_kernel(x_ref, gain_ref, wg_ref, wu_ref, wd_ref, gfin_ref, o_ref, h_buf, a_buf):
    tile = h_buf.shape[1]
    for i in range(x_ref.shape[0] // tile):
        rows = pl.ds(i * tile, tile)
        _ffn_tile(x_ref.at[rows], o_ref.at[rows], h_buf.at[i], a_buf.at[i],
                  gain_ref, wg_ref, wu_ref, wd_ref, gfin_ref)


def _resident(shape, index=None):
    index = (0,) * len(shape) if index is None else index
    return pl.BlockSpec(shape, lambda *_: index, pipeline_mode=pl.Buffered(1))


def _nbytes(shape, dtype):
    n = 1
    for s in shape:
        n *= s
    return n * jnp.dtype(dtype).itemsize


def _vmem_limit(resident, streamed, scratch):
    need = sum(resident) + 2 * sum(streamed) + sum(scratch)
    assert need <= V7X_SCOPED_VMEM_BYTES, need
    return V7X_SCOPED_VMEM_BYTES


def kernel(x, mem, norm_mix, w_in, conv_w, w_conv_out, w_pool, pool_scale, norm_mem, w_kv,
           w_xattn_out, w_out, norm_ffn, w_gate, w_up, w_down, norm_final):
    batch, seq, d = x.shape
    assert d == D_MODEL and mem.shape == (batch, N_MEM, D_MODEL)
    assert w_in.shape == (1, D_MODEL, D_IN) and w_gate.shape == (1, D_MODEL, D_FF)
    tile = TOKEN_TILE
    ffn_rows = FFN_STEP_TILES * tile
    assert seq % tile == 0 and (batch * seq) % ffn_rows == 0 and batch <= PREP_STEPS
    row = lambda a: a.reshape(1, -1).astype(F32)
    sq = (D_MODEL, D_MODEL)

    rb = D_MODEL // PREP_STEPS
    assert rb % BF16_SUBLANES == 0
    last_batch = lambda i: (jnp.minimum(i, batch - 1), 0, 0)
    rows_of = lambda i: (i, 0)
    prep_resident = [
        _nbytes((D_MODEL, 2 * D_MODEL), F32), 2 * _nbytes(sq, F32), _nbytes((N_POOL, CH, CH), F32)]
    prep_streamed = [
        _nbytes((N_MEM, D_MODEL), F32), _nbytes((rb, D_IN), F32), 2 * _nbytes((rb, D_MODEL), F32),
        2 * _nbytes(sq, BF16), _nbytes((rb, D_MIX), BF16), 2 * _nbytes((rb, D_MODEL), BF16)]
    prep_scratch = [
        _nbytes((D_MODEL, 2 * D_MODEL), BF16), 2 * _nbytes(sq, BF16), _nbytes((N_POOL, CH, CH), BF16)]
    sm, vo, w_in_b, w_co_b, w_out_b = pl.pallas_call(
        functools.partial(_prep_kernel, batch=batch),
        grid=(PREP_STEPS,),
        in_specs=[
            pl.BlockSpec((1, N_MEM, D_MODEL), last_batch),
            _resident((1, D_MODEL)),
            _resident((D_MODEL, 2 * D_MODEL)),
            _resident(sq, (0, SRC_Q_X)),
            _resident(sq),
            _resident((N_POOL, CH, CH)),
            _resident((1, D_MODEL)),
            pl.BlockSpec((rb, D_IN), rows_of),
            pl.BlockSpec((rb, D_MODEL), rows_of),
            pl.BlockSpec((rb, D_MODEL), rows_of),
        ],
        out_specs=[
            pl.BlockSpec((1,) + sq, last_batch),
            pl.BlockSpec((1,) + sq, last_batch),
            pl.BlockSpec((rb, D_MIX), rows_of),
            pl.BlockSpec((rb, D_MODEL), rows_of),
            pl.BlockSpec((rb, D_MODEL), rows_of),
        ],
        out_shape=[
            jax.ShapeDtypeStruct((batch,) + sq, BF16),
            jax.ShapeDtypeStruct((batch,) + sq, BF16),
            jax.ShapeDtypeStruct((D_MODEL, D_MIX), BF16),
            jax.ShapeDtypeStruct(sq, BF16),
            jax.ShapeDtypeStruct(sq, BF16),
        ],
        scratch_shapes=[
            pltpu.VMEM((D_MODEL, 2 * D_MODEL), BF16),
            pltpu.VMEM(sq, BF16),
            pltpu.VMEM(sq, BF16),
            pltpu.VMEM((N_POOL, CH, CH), BF16),
        ],
        compiler_params=pltpu.CompilerParams(
            dimension_semantics=("arbitrary",),
            vmem_limit_bytes=_vmem_limit(prep_resident, prep_streamed, prep_scratch)),
        name="prep",
    )(mem, row(norm_mem[0]), w_kv[0], w_in[0], w_xattn_out[0], w_pool[0], row(pool_scale[0]),
      w_in[0], w_conv_out[0], w_out[0])

    n_t = seq // tile
    n_steps = batch * n_t
    gu_rb = D_MODEL // n_steps
    d_steps = n_steps // 2
    d_rb = D_FF // d_steps
    assert gu_rb % BF16_SUBLANES == 0 and d_rb % BF16_SUBLANES == 0
    step_block = lambda b, t: (b * n_t + t, 0)
    down_block = lambda b, t: (jnp.minimum(b * n_t + t, d_steps - 1), 0)
    mixer_resident = [_nbytes((D_MODEL, D_MIX), BF16), 2 * _nbytes(sq, BF16)]
    mixer_streamed = [
        2 * _nbytes((tile, D_MODEL), F32), 2 * _nbytes(sq, BF16),
        2 * _nbytes((gu_rb, D_FF), F32), _nbytes((d_rb, D_MODEL), F32),
        2 * _nbytes((gu_rb, D_FF), BF16), _nbytes((d_rb, D_MODEL), BF16)]
    mixer_scratch = [
        2 * _nbytes((tile, D_MODEL), BF16), 2 * _nbytes((HALO + tile, D_MODEL), F32)]
    x1, w_gate_b, w_up_b, w_down_b = pl.pallas_call(
        _mixer_kernel,
        grid=(batch, n_t),
        in_specs=[
            pl.BlockSpec((1, tile, D_MODEL), lambda b, t: (b, t, 0)),
            pl.BlockSpec((1,) + sq, lambda b, t: (b, 0, 0)),
            pl.BlockSpec((1,) + sq, lambda b, t: (b, 0, 0)),
            _resident((1, D_MODEL)),
            _resident((D_MODEL, D_MIX)),
            _resident((CONV_K, D_MODEL)),
            _resident(sq),
            _resident(sq),
            pl.BlockSpec((gu_rb, D_FF), step_block),
            pl.BlockSpec((gu_rb, D_FF), step_block),
            pl.BlockSpec((d_rb, D_MODEL), down_block),
        ],
        out_specs=[
            pl.BlockSpec((1, tile, D_MODEL), lambda b, t: (b, t, 0)),
            pl.BlockSpec((gu_rb, D_FF), step_block),
            pl.BlockSpec((gu_rb, D_FF), step_block),
            pl.BlockSpec((d_rb, D_MODEL), down_block),
        ],
        out_shape=[
            jax.ShapeDtypeStruct((batch, seq, D_MODEL), F32),
            jax.ShapeDtypeStruct((D_MODEL, D_FF), BF16),
            jax.ShapeDtypeStruct((D_MODEL, D_FF), BF16),
            jax.ShapeDtypeStruct((D_FF, D_MODEL), BF16),
        ],
        scratch_shapes=[
            pltpu.VMEM((tile, D_MODEL), BF16),
            pltpu.VMEM((HALO + tile, D_MODEL), F32),
            pltpu.VMEM((HALO + tile, D_MODEL), F32),
            pltpu.VMEM((tile, D_MODEL), BF16),
        ],
        compiler_params=pltpu.CompilerParams(
            dimension_semantics=("arbitrary", "arbitrary"),
            vmem_limit_bytes=_vmem_limit(mixer_resident, mixer_streamed, mixer_scratch)),
        name="mixer",
    )(x, sm, vo, row(norm_mix[0]), w_in_b, conv_w[0].astype(F32), w_co_b, w_out_b,
      w_gate[0], w_up[0], w_down[0])

    n_tok = batch * seq
    ffn_resident = [3 * _nbytes((D_MODEL, D_FF), BF16)]
    ffn_streamed = [2 * _nbytes((ffn_rows, D_MODEL), F32)]
    ffn_scratch = [_nbytes((ffn_rows, D_MODEL), BF16), _nbytes((ffn_rows, D_FF), BF16)]
    out = pl.pallas_call(
        _ffn_kernel,
        grid=(n_tok // ffn_rows,),
        in_specs=[
            pl.BlockSpec((ffn_rows, D_MODEL), lambda i: (i, 0)),
            _resident((1, D_MODEL)),
            _resident((D_MODEL, D_FF)),
            _resident((D_MODEL, D_FF)),
            _resident((D_FF, D_MODEL)),
            _resident((1, D_MODEL)),
        ],
        out_specs=pl.BlockSpec((ffn_rows, D_MODEL), lambda i: (i, 0)),
        out_shape=jax.ShapeDtypeStruct((n_tok, D_MODEL), F32),
        scratch_shapes=[
            pltpu.VMEM((FFN_STEP_TILES, tile, D_MODEL), BF16),
            pltpu.VMEM((FFN_STEP_TILES, tile, D_FF), BF16),
        ],
        compiler_params=pltpu.CompilerParams(
            dimension_semantics=("arbitrary",),
            vmem_limit_bytes=_vmem_limit(ffn_resident, ffn_streamed, ffn_scratch)),
        name="ffn",
    )(x1.reshape(n_tok, D_MODEL), row(norm_ffn[0]), w_gate_b, w_up_b, w_down_b, row(norm_final))
    return out.reshape(batch, seq, D_MODEL)
```

```python
import functools

import jax
import jax.numpy as jnp
from jax import lax
from jax.experimental import pallas as pl
from jax.experimental.pallas import tpu as pltpu

F32 = jnp.float32
BF16 = jnp.bfloat16

D_MODEL = 1024
N_MEM = 256
CONV_K = 3
POOL_WINDOWS = (2, 4, 8, 16)
N_POOL = len(POOL_WINDOWS)
X_HEADS = 4
X_HEAD_DIM = D_MODEL // X_HEADS
D_FF = 2816
EPS = 1e-6

SRC_B_A, SRC_C_A, SRC_U_A, SRC_U_P, SRC_Q_X, SRC_G_A, SRC_G_P, SRC_G_X = range(8)
D_IN = 8 * D_MODEL
MIX_SRC = (SRC_B_A, SRC_C_A, SRC_U_A, SRC_U_P, SRC_G_A, SRC_G_P, SRC_G_X)
COL_B_A, COL_C_A, COL_U_A, COL_U_P, COL_G_A, COL_G_P, COL_G_X = (
    i * D_MODEL for i in range(len(MIX_SRC)))
D_MIX = len(MIX_SRC) * D_MODEL

V7X_MXU_WIDTH = 256
V7X_VMEM_BYTES = 64 * 1024 * 1024
V7X_SCOPED_VMEM_BYTES = V7X_VMEM_BYTES - 4 * 1024 * 1024
BF16_SUBLANES = 16
HALO = 16
TOKEN_TILE = 512
FFN_STEP_TILES = 2
CH = V7X_MXU_WIDTH
N_CH = D_MODEL // CH
LEAD_STEPS = 8

assert X_HEAD_DIM == CH and N_MEM == CH and D_MODEL // N_POOL == CH


def _rms_norm(x, gain):
    ms = jnp.mean(x * x, axis=-1, keepdims=True)
    return (x * lax.rsqrt(ms + EPS)) * gain


def _sigmoid(x):
    return 1.0 / (1.0 + jnp.exp(-x))


def _dot(a, b):
    return jnp.dot(a, b, preferred_element_type=F32)


def _dot_bt(a, b):
    return lax.dot_general(a, b, (((1,), (1,)), ((), ())), preferred_element_type=F32)


def _cols(j):
    return slice(j * CH, (j + 1) * CH)


def _prep_kernel(mem_ref, gain_ref, wkv_ref, wq_ref, wxo_ref, sm_ref, vo_ref,
                 wkv_buf, wq_buf, wxo_buf):
    @pl.when(pl.program_id(0) == 0)
    def _():
        wkv_buf[...] = wkv_ref[...].astype(BF16)
        wq_buf[...] = wq_ref[...].astype(BF16)
        wxo_buf[...] = wxo_ref[...].astype(BF16)

    mn = _rms_norm(mem_ref[0], gain_ref[...]).astype(BF16)
    scale = X_HEAD_DIM ** -0.5
    for h in range(X_HEADS):
        k = (_dot(mn, wkv_buf[:, _cols(h)]) * scale).astype(BF16)
        sm_ref[0, :, _cols(h)] = _dot_bt(wq_buf[:, _cols(h)], k).astype(BF16)
        v = _dot(mn, wkv_buf[:, D_MODEL + h * CH:D_MODEL + (h + 1) * CH]).astype(BF16)
        vo_ref[0, _cols(h), :] = _dot(v, wxo_buf[_cols(h), :]).astype(BF16)


def _convert_mixer_weights(i, win_ref, wco_ref, wout_ref, wpool_ref, pscale_ref, win_b, wco_b, wout_b):
    rb = win_ref.shape[0]
    rows = pl.ds(pl.multiple_of(i * rb, rb), rb)
    for dst, src in enumerate(MIX_SRC):
        if src != SRC_U_P:
            win_b[rows, dst * D_MODEL:(dst + 1) * D_MODEL] = (
                win_ref[:, src * D_MODEL:(src + 1) * D_MODEL].astype(BF16))
            continue
        for g in range(N_POOL):
            w_up = win_ref[:, src * D_MODEL + g * CH:src * D_MODEL + (g + 1) * CH].astype(BF16)
            folded = _dot(w_up, wpool_ref[g].astype(BF16)) * pscale_ref[:, _cols(g)]
            win_b[rows, dst * D_MODEL + g * CH:dst * D_MODEL + (g + 1) * CH] = folded.astype(BF16)
    wco_b[rows, :] = wco_ref[...].astype(BF16)
    wout_b[rows, :] = wout_ref[...].astype(BF16)


def _mixer_kernel(x_ref, sm_ref, vo_ref, gain_ref, convw_ref,
                  win_f32, wco_f32, wout_f32, wpool_ref, pscale_ref,
                  wg_ref, wu_ref, wd_ref,
                  o_ref, wg_o, wu_o, wd_o,
                  win_ref, wco_ref, wout_ref, h_buf, cu_buf, up_buf, z_buf, *, lead, n_t):
    s = pl.program_id(0)

    @pl.when(s < lead)
    def _():
        _convert_mixer_weights(s, win_f32, wco_f32, wout_f32, wpool_ref, pscale_ref,
                               win_ref, wco_ref, wout_ref)

    @pl.when(s >= lead)
    def _():
        _mixer_tile(lax.rem(s - lead, n_t), x_ref.at[0], o_ref.at[0], sm_ref, vo_ref, gain_ref,
                    convw_ref, win_ref, wco_ref, wout_ref, wg_ref, wu_ref, wd_ref,
                    wg_o, wu_o, wd_o, h_buf, cu_buf, up_buf, z_buf)


def _mixer_tile(t, x_ref, o_ref, sm_ref, vo_ref, gain_ref, convw_ref, win_ref, wco_ref, wout_ref,
                wg_ref, wu_ref, wd_ref, wg_o, wu_o, wd_o, h_buf, cu_buf, up_buf, z_buf):
    tile = x_ref.shape[0]

    @pl.when(t == 0)
    def _():
        cu_buf[0:HALO, :] = jnp.zeros((HALO, D_MODEL), F32)
        up_buf[0:HALO, :] = jnp.zeros((HALO, D_MODEL), F32)

    h_buf[...] = _rms_norm(x_ref[...], gain_ref[...]).astype(BF16)

    def proj(col0, j):
        return _dot(h_buf[...], win_ref[:, col0 + j * CH:col0 + (j + 1) * CH])

    for j in range(N_CH):
        cu_buf[HALO:, _cols(j)] = proj(COL_C_A, j) * proj(COL_U_A, j)
    for j in range(N_CH):
        cw = convw_ref[:, _cols(j)]
        conv = cw[CONV_K - 1:CONV_K] * cu_buf[HALO:, _cols(j)]
        for k in range(CONV_K - 1):
            lag = CONV_K - 1 - k
            conv = conv + cw[k:k + 1] * cu_buf[HALO - lag:HALO - lag + tile, _cols(j)]
        z_buf[:, _cols(j)] = (proj(COL_B_A, j) * conv).astype(BF16)
    cu_buf[0:HALO, :] = cu_buf[tile:tile + HALO, :]
    for j in range(N_CH):
        y = _dot(z_buf[...], wco_ref[:, _cols(j)])
        o_ref[:, _cols(j)] = _sigmoid(proj(COL_G_A, j)) * y

    for g in range(N_POOL):
        up_buf[HALO:, _cols(g)] = proj(COL_U_P, g)
    pos = t * tile + 1 + lax.broadcasted_iota(jnp.int32, (tile, 1), 0)
    scores = []
    for g, w in enumerate(POOL_WINDOWS):
        e = up_buf[:, _cols(g)]
        s = e
        k = 1
        while k < w:
            s = s + pltpu.roll(s, k, 0)
            k *= 2
        inv_cnt = 1.0 / jnp.minimum(pos, w).astype(F32)
        y = s[HALO:] * inv_cnt - e[HALO:]
        o_ref[:, _cols(g)] += _sigmoid(proj(COL_G_P, g)) * y
        scores.append(_dot(h_buf[...], sm_ref[0, :, _cols(g)]))
    up_buf[0:HALO, :] = up_buf[tile:tile + HALO, :]

    for h in range(X_HEADS):
        s = scores[h]
        p = jnp.exp(s - jnp.max(s, axis=-1, keepdims=True))
        inv_l = 1.0 / jnp.sum(p, axis=-1, keepdims=True)
        z_buf[:, _cols(h)] = (p * inv_l).astype(BF16)
    for j in range(N_CH):
        y = _dot(z_buf[...], vo_ref[0, :, _cols(j)])
        o_ref[:, _cols(j)] += _sigmoid(proj(COL_G_X, j)) * y

    h_buf[...] = o_ref[...].astype(BF16)
    for j in range(N_CH):
        o_ref[:, _cols(j)] = x_ref[:, _cols(j)] + _dot(h_buf[...], wout_ref[:, _cols(j)])

    wg_o[...] = wg_ref[...].astype(BF16)
    wu_o[...] = wu_ref[...].astype(BF16)
    wd_o[...] = wd_ref[...].astype(BF16)


def _ffn_tile(x_ref, o_ref, h_buf, a_buf, gain_ref, wg_ref, wu_ref, wd_ref, gfin_ref):
    h_buf[...] = _rms_norm(x_ref[...], gain_ref[...]).astype(BF16)
    for c in range(D_FF // CH):
        g = _dot(h_buf[...], wg_ref[:, _cols(c)])
        u = _dot(h_buf[...], wu_ref[:, _cols(c)])
        a_buf[:, _cols(c)] = (g * _sigmoid(g) * u).astype(BF16)
    for j in range(N_CH):
        o_ref[:, _cols(j)] = x_ref[:, _cols(j)] + _dot(a_buf[...], wd_ref[:, _cols(j)])
    o_ref[...] = _rms_norm(o_ref[...], gfin_ref[...])


def _ffn_kernel(x_ref, gain_ref, wg_ref, wu_ref, wd_ref, gfin_ref, o_ref, h_buf, a_buf):
    tile = h_buf.shape[1]
    for i in range(x_ref.shape[0] // tile):
        rows = pl.ds(i * tile, tile)
        _ffn_tile(x_ref.at[rows], o_ref.at[rows], h_buf.at[i], a_buf.at[i],
                  gain_ref, wg_ref, wu_ref, wd_ref, gfin_ref)


def _resident(shape, index=None):
    index = (0,) * len(shape) if index is None else index
    return pl.BlockSpec(shape, lambda *_: index, pipeline_mode=pl.Buffered(1))


def _nbytes(shape, dtype):
    n = 1
    for s in shape:
        n *= s
    return n * jnp.dtype(dtype).itemsize


def _vmem_limit(resident, streamed, scratch):
    need = sum(resident) + 2 * sum(streamed) + sum(scratch)
    assert need <= V7X_SCOPED_VMEM_BYTES, need
    return V7X_SCOPED_VMEM_BYTES


def kernel(x, mem, norm_mix, w_in, conv_w, w_conv_out, w_pool, pool_scale, norm_mem, w_kv,
           w_xattn_out, w_out, norm_ffn, w_gate, w_up, w_down, norm_final):
    batch, seq, d = x.shape
    assert d == D_MODEL and mem.shape == (batch, N_MEM, D_MODEL)
    assert w_in.shape == (1, D_MODEL, D_IN) and w_gate.shape == (1, D_MODEL, D_FF)
    tile = TOKEN_TILE
    ffn_rows = FFN_STEP_TILES * tile
    assert seq % tile == 0 and (batch * seq) % ffn_rows == 0
    row = lambda a: a.reshape(1, -1).astype(F32)
    sq = (D_MODEL, D_MODEL)

    prep_resident = [_nbytes((D_MODEL, 2 * D_MODEL), F32), 2 * _nbytes(sq, F32)]
    prep_streamed = [_nbytes((N_MEM, D_MODEL), F32), 2 * _nbytes(sq, BF16)]
    prep_scratch = [_nbytes((D_MODEL, 2 * D_MODEL), BF16), 2 * _nbytes(sq, BF16)]
    sm, vo = pl.pallas_call(
        _prep_kernel,
        grid=(batch,),
        in_specs=[
            pl.BlockSpec((1, N_MEM, D_MODEL), lambda b: (b, 0, 0)),
            _resident((1, D_MODEL)),
            _resident((D_MODEL, 2 * D_MODEL)),
            _resident(sq, (0, SRC_Q_X)),
            _resident(sq),
        ],
        out_specs=[
            pl.BlockSpec((1,) + sq, lambda b: (b, 0, 0)),
            pl.BlockSpec((1,) + sq, lambda b: (b, 0, 0)),
        ],
        out_shape=[
            jax.ShapeDtypeStruct((batch,) + sq, BF16),
            jax.ShapeDtypeStruct((batch,) + sq, BF16),
        ],
        scratch_shapes=[
            pltpu.VMEM((D_MODEL, 2 * D_MODEL), BF16),
            pltpu.VMEM(sq, BF16),
            pltpu.VMEM(sq, BF16),
        ],
        compiler_params=pltpu.CompilerParams(
            dimension_semantics=("arbitrary",),
            vmem_limit_bytes=_vmem_limit(prep_resident, prep_streamed, prep_scratch)),
        name="prep",
    )(mem, row(norm_mem[0]), w_kv[0], w_in[0], w_xattn_out[0])

    lead = LEAD_STEPS
    rb = D_MODEL // lead
    n_t = seq // tile
    n_steps = batch * n_t
    gu_rb = D_MODEL // n_steps
    d_steps = n_steps // 2
    d_rb = D_FF // d_steps
    assert rb % BF16_SUBLANES == 0 and gu_rb % BF16_SUBLANES == 0 and d_rb % BF16_SUBLANES == 0
    tok = lambda s: jnp.maximum(s - lead, 0)
    tile_block = lambda s: (tok(s) // n_t, tok(s) % n_t, 0)
    batch_block = lambda s: (tok(s) // n_t, 0, 0)
    lead_block = lambda s: (jnp.minimum(s, lead - 1), 0)
    step_block = lambda s: (tok(s), 0)
    down_block = lambda s: (jnp.minimum(tok(s), d_steps - 1), 0)
    mixer_resident = [_nbytes((N_POOL, CH, CH), F32)]
    mixer_streamed = [
        2 * _nbytes((tile, D_MODEL), F32), 2 * _nbytes(sq, BF16),
        _nbytes((rb, D_IN), F32), 2 * _nbytes((rb, D_MODEL), F32),
        2 * _nbytes((gu_rb, D_FF), F32), _nbytes((d_rb, D_MODEL), F32),
        2 * _nbytes((gu_rb, D_FF), BF16), _nbytes((d_rb, D_MODEL), BF16)]
    mixer_scratch = [
        _nbytes((D_MODEL, D_MIX), BF16), 2 * _nbytes(sq, BF16),
        2 * _nbytes((tile, D_MODEL), BF16), 2 * _nbytes((HALO + tile, D_MODEL), F32)]
    x1, w_gate_b, w_up_b, w_down_b = pl.pallas_call(
        functools.partial(_mixer_kernel, lead=lead, n_t=n_t),
        grid=(lead + n_steps,),
        in_specs=[
            pl.BlockSpec((1, tile, D_MODEL), tile_block),
            pl.BlockSpec((1,) + sq, batch_block),
            pl.BlockSpec((1,) + sq, batch_block),
            _resident((1, D_MODEL)),
            _resident((CONV_K, D_MODEL)),
            pl.BlockSpec((rb, D_IN), lead_block),
            pl.BlockSpec((rb, D_MODEL), lead_block),
            pl.BlockSpec((rb, D_MODEL), lead_block),
            _resident((N_POOL, CH, CH)),
            _resident((1, D_MODEL)),
            pl.BlockSpec((gu_rb, D_FF), step_block),
            pl.BlockSpec((gu_rb, D_FF), step_block),
            pl.BlockSpec((d_rb, D_MODEL), down_block),
        ],
        out_specs=[
            pl.BlockSpec((1, tile, D_MODEL), tile_block),
            pl.BlockSpec((gu_rb, D_FF), step_block),
            pl.BlockSpec((gu_rb, D_FF), step_block),
            pl.BlockSpec((d_rb, D_MODEL), down_block),
        ],
        out_shape=[
            jax.ShapeDtypeStruct((batch, seq, D_MODEL), F32),
            jax.ShapeDtypeStruct((D_MODEL, D_FF), BF16),
            jax.ShapeDtypeStruct((D_MODEL, D_FF), BF16),
            jax.ShapeDtypeStruct((D_FF, D_MODEL), BF16),
        ],
        scratch_shapes=[
            pltpu.VMEM((D_MODEL, D_MIX), BF16),
            pltpu.VMEM(sq, BF16),
            pltpu.VMEM(sq, BF16),
            pltpu.VMEM((tile, D_MODEL), BF16),
            pltpu.VMEM((HALO + tile, D_MODEL), F32),
            pltpu.VMEM((HALO + tile, D_MODEL), F32),
            pltpu.VMEM((tile, D_MODEL), BF16),
        ],
        compiler_params=pltpu.CompilerParams(
            dimension_semantics=("arbitrary",),
            vmem_limit_bytes=_vmem_limit(mixer_resident, mixer_streamed, mixer_scratch)),
        name="mixer",
    )(x, sm, vo, row(norm_mix[0]), conv_w[0].astype(F32), w_in[0], w_conv_out[0], w_out[0],
      w_pool[0], row(pool_scale[0]), w_gate[0], w_up[0], w_down[0])

    n_tok = batch * seq
    ffn_resident = [3 * _nbytes((D_MODEL, D_FF), BF16)]
    ffn_streamed = [2 * _nbytes((ffn_rows, D_MODEL), F32)]
    ffn_scratch = [_nbytes((ffn_rows, D_MODEL), BF16), _nbytes((ffn_rows, D_FF), BF16)]
    out = pl.pallas_call(
        _ffn_kernel,
        grid=(n_tok // ffn_rows,),
        in_specs=[
            pl.BlockSpec((ffn_rows, D_MODEL), lambda i: (i, 0)),
            _resident((1, D_MODEL)),
            _resident((D_MODEL, D_FF)),
            _resident((D_MODEL, D_FF)),
            _resident((D_FF, D_MODEL)),
            _resident((1, D_MODEL)),
        ],
        out_specs=pl.BlockSpec((ffn_rows, D_MODEL), lambda i: (i, 0)),
        out_shape=jax.ShapeDtypeStruct((n_tok, D_MODEL), F32),
        scratch_shapes=[
            pltpu.VMEM((FFN_STEP_TILES, tile, D_MODEL), BF16),
            pltpu.VMEM((FFN_STEP_TILES, tile, D_FF), BF16),
        ],
        compiler_params=pltpu.CompilerParams(
            dimension_semantics=("arbitrary",),
            vmem_limit_bytes=_vmem_limit(ffn_resident, ffn_streamed, ffn_scratch)),
        name="ffn",
    )(x1.reshape(n_tok, D_MODEL), row(norm_ffn[0]), w_gate_b, w_up_b, w_down_b, row(norm_final))
    return out.reshape(batch, seq, D_MODEL)
```

```python
import functools

import jax
import jax.numpy as jnp
from jax import lax
from jax.experimental import pallas as pl
from jax.experimental.pallas import tpu as pltpu

F32 = jnp.float32
BF16 = jnp.bfloat16

D_MODEL = 1024
N_MEM = 256
CONV_K = 3
POOL_WINDOWS = (2, 4, 8, 16)
N_POOL = len(POOL_WINDOWS)
X_HEADS = 4
X_HEAD_DIM = D_MODEL // X_HEADS
D_FF = 2816
EPS = 1e-6

SRC_B_A, SRC_C_A, SRC_U_A, SRC_U_P, SRC_Q_X, SRC_G_A, SRC_G_P, SRC_G_X = range(8)
D_IN = 8 * D_MODEL
MIX_SRC = (SRC_B_A, SRC_C_A, SRC_U_A, SRC_U_P, SRC_G_A, SRC_G_P, SRC_G_X)
COL_B_A, COL_C_A, COL_U_A, COL_U_P, COL_G_A, COL_G_P, COL_G_X = (
    i * D_MODEL for i in range(len(MIX_SRC)))
D_MIX = len(MIX_SRC) * D_MODEL

V7X_MXU_WIDTH = 256
V7X_VMEM_BYTES = 64 * 1024 * 1024
V7X_SCOPED_VMEM_BYTES = V7X_VMEM_BYTES - 4 * 1024 * 1024
BF16_SUBLANES = 16
HALO = 16
TOKEN_TILE = 512
FFN_STEP_TILES = 2
CH = V7X_MXU_WIDTH
N_CH = D_MODEL // CH
LEAD_STEPS = 8

assert X_HEAD_DIM == CH and N_MEM == CH and D_MODEL // N_POOL == CH


def _inv_rms(x):
    return lax.rsqrt(jnp.mean(x * x, axis=-1, keepdims=True) + EPS)


def _rms_norm(x, gain):
    return (x * _inv_rms(x)) * gain


def _sigmoid(x):
    return 1.0 / (1.0 + jnp.exp(-x))


def _dot(a, b):
    return jnp.dot(a, b, preferred_element_type=F32)


def _dot_bt(a, b):
    return lax.dot_general(a, b, (((1,), (1,)), ((), ())), preferred_element_type=F32)


def _cols(j):
    return slice(j * CH, (j + 1) * CH)


def _prep_kernel(mem_ref, gain_ref, wkv_ref, wq_ref, gmix_ref, wxo_ref, sm_ref, vo_ref,
                 wkv_buf, wq_buf, wxo_buf):
    @pl.when(pl.program_id(0) == 0)
    def _():
        wkv_buf[...] = wkv_ref[...].astype(BF16)
        wq_buf[...] = (wq_ref[...] * gmix_ref[...]).astype(BF16)
        wxo_buf[...] = wxo_ref[...].astype(BF16)

    mn = _rms_norm(mem_ref[0], gain_ref[...]).astype(BF16)
    scale = X_HEAD_DIM ** -0.5
    heads = range(X_HEADS)
    k = [(_dot(mn, wkv_buf[:, _cols(h)]) * scale).astype(BF16) for h in heads]
    v = [_dot(mn, wkv_buf[:, D_MODEL + h * CH:D_MODEL + (h + 1) * CH]).astype(BF16) for h in heads]
    for h in heads:
        sm_ref[0, :, _cols(h)] = _dot_bt(wq_buf[:, _cols(h)], k[h]).astype(BF16)
    for h in heads:
        vo_ref[0, _cols(h), :] = _dot(v[h], wxo_buf[_cols(h), :]).astype(BF16)


def _convert_mixer_weights(i, win_ref, gmix_ref, wco_ref, wout_ref, wpool_ref, pscale_ref,
                           win_b, wco_b, wout_b):
    rb = win_ref.shape[0]
    rows = pl.ds(pl.multiple_of(i * rb, rb), rb)
    gain = gmix_ref[...]
    for dst, src in enumerate(MIX_SRC):
        if src != SRC_U_P:
            win_b[rows, dst * D_MODEL:(dst + 1) * D_MODEL] = (
                win_ref[:, src * D_MODEL:(src + 1) * D_MODEL] * gain).astype(BF16)
            continue
        for g in range(N_POOL):
            w_up = (win_ref[:, src * D_MODEL + g * CH:src * D_MODEL + (g + 1) * CH] * gain).astype(BF16)
            folded = _dot(w_up, wpool_ref[g].astype(BF16)) * pscale_ref[:, _cols(g)]
            win_b[rows, dst * D_MODEL + g * CH:dst * D_MODEL + (g + 1) * CH] = folded.astype(BF16)
    wco_b[rows, :] = wco_ref[...].astype(BF16)
    wout_b[rows, :] = wout_ref[...].astype(BF16)


def _mixer_kernel(x_ref, sm_ref, vo_ref, convw_ref,
                  win_f32, gmix_ref, wco_f32, wout_f32, wpool_ref, pscale_ref,
                  wg_ref, wu_ref, wd_ref, gffn_ref,
                  o_ref, wg_o, wu_o, wd_o,
                  win_ref, wco_ref, wout_ref, h_buf, r_buf, cu_buf, up_buf, z_buf, *, lead, n_t):
    s = pl.program_id(0)

    @pl.when(s < lead)
    def _():
        _convert_mixer_weights(s, win_f32, gmix_ref, wco_f32, wout_f32, wpool_ref, pscale_ref,
                               win_ref, wco_ref, wout_ref)

    @pl.when(s >= lead)
    def _():
        _mixer_tile(lax.rem(s - lead, n_t), x_ref.at[0], o_ref.at[0], sm_ref, vo_ref,
                    convw_ref, win_ref, wco_ref, wout_ref, wg_ref, wu_ref, wd_ref, gffn_ref,
                    wg_o, wu_o, wd_o, h_buf, r_buf, cu_buf, up_buf, z_buf)


def _mixer_tile(t, x_ref, o_ref, sm_ref, vo_ref, convw_ref, win_ref, wco_ref, wout_ref,
                wg_ref, wu_ref, wd_ref, gffn_ref, wg_o, wu_o, wd_o, h_buf, r_buf, cu_buf, up_buf, z_buf):
    tile = x_ref.shape[0]

    @pl.when(t == 0)
    def _():
        cu_buf[0:HALO, :] = jnp.zeros((HALO, D_MODEL), F32)
        up_buf[0:HALO, :] = jnp.zeros((HALO, D_MODEL), F32)

    h_buf[...] = x_ref[...].astype(BF16)
    r_buf[...] = jnp.broadcast_to(_inv_rms(x_ref[...]), r_buf.shape)

    def proj(col0, j):
        return _dot(h_buf[...], win_ref[:, col0 + j * CH:col0 + (j + 1) * CH]) * r_buf[...]

    for j in range(N_CH):
        cu_buf[HALO:, _cols(j)] = proj(COL_C_A, j) * proj(COL_U_A, j)
    for j in range(N_CH):
        cw = convw_ref[:, _cols(j)]
        conv = cw[CONV_K - 1:CONV_K] * cu_buf[HALO:, _cols(j)]
        for k in range(CONV_K - 1):
            lag = CONV_K - 1 - k
            conv = conv + cw[k:k + 1] * cu_buf[HALO - lag:HALO - lag + tile, _cols(j)]
        z_buf[:, _cols(j)] = (proj(COL_B_A, j) * conv).astype(BF16)
    cu_buf[0:HALO, :] = cu_buf[tile:tile + HALO, :]
    for j in range(N_CH):
        y = _dot(z_buf[...], wco_ref[:, _cols(j)])
        o_ref[:, _cols(j)] = _sigmoid(proj(COL_G_A, j)) * y

    for g in range(N_POOL):
        up_buf[HALO:, _cols(g)] = proj(COL_U_P, g)
    pos = t * tile + 1 + lax.broadcasted_iota(jnp.int32, (tile, 1), 0)
    scores = []
    for g, w in enumerate(POOL_WINDOWS):
        e = up_buf[:, _cols(g)]
        s = e
        k = 1
        while k < w:
            s = s + pltpu.roll(s, k, 0)
            k *= 2
        inv_cnt = 1.0 / jnp.minimum(pos, w).astype(F32)
        y = s[HALO:] * inv_cnt - e[HALO:]
        o_ref[:, _cols(g)] += _sigmoid(proj(COL_G_P, g)) * y
        scores.append(_dot(h_buf[...], sm_ref[0, :, _cols(g)]) * r_buf[...])
    up_buf[0:HALO, :] = up_buf[tile:tile + HALO, :]

    for h in range(X_HEADS):
        s = scores[h]
        p = jnp.exp(s - jnp.max(s, axis=-1, keepdims=True))
        inv_l = 1.0 / jnp.sum(p, axis=-1, keepdims=True)
        z_buf[:, _cols(h)] = (p * inv_l).astype(BF16)
    for j in range(N_CH):
        y = _dot(z_buf[...], vo_ref[0, :, _cols(j)])
        o_ref[:, _cols(j)] += _sigmoid(proj(COL_G_X, j)) * y

    h_buf[...] = o_ref[...].astype(BF16)
    for j in range(N_CH):
        o_ref[:, _cols(j)] = x_ref[:, _cols(j)] + _dot(h_buf[...], wout_ref[:, _cols(j)])

    wg_o[...] = (wg_ref[...] * gffn_ref[...]).astype(BF16)
    wu_o[...] = (wu_ref[...] * gffn_ref[...]).astype(BF16)
    wd_o[...] = wd_ref[...].astype(BF16)


def _ffn_tile(x_ref, o_ref, h_buf, r_buf, a_buf, wg_ref, wu_ref, wd_ref, gfin_ref):
    h_buf[...] = x_ref[...].astype(BF16)
    r_buf[...] = jnp.broadcast_to(_inv_rms(x_ref[...]), r_buf.shape)
    for c in range(D_FF // CH):
        g = _dot(h_buf[...], wg_ref[:, _cols(c)]) * r_buf[...]
        u = _dot(h_buf[...], wu_ref[:, _cols(c)]) * r_buf[...]
        a_buf[:, _cols(c)] = (g * _sigmoid(g) * u).astype(BF16)
    for j in range(N_CH):
        o_ref[:, _cols(j)] = x_ref[:, _cols(j)] + _dot(a_buf[...], wd_ref[:, _cols(j)])
    o_ref[...] = _rms_norm(o_ref[...], gfin_ref[...])


def _ffn_kernel(x_ref, wg_ref, wu_ref, wd_ref, gfin_ref, o_ref, h_buf, r_buf, a_buf):
    tile = h_buf.shape[1]
    for i in range(x_ref.shape[0] // tile):
        rows = pl.ds(i * tile, tile)
        _ffn_tile(x_ref.at[rows], o_ref.at[rows], h_buf.at[i], r_buf.at[i], a_buf.at[i],
                  wg_ref, wu_ref, wd_ref, gfin_ref)


def _resident(shape, index=None):
    index = (0,) * len(shape) if index is None else index
    return pl.BlockSpec(shape, lambda *_: index, pipeline_mode=pl.Buffered(1))


def _nbytes(shape, dtype):
    n = 1
    for s in shape:
        n *= s
    return n * jnp.dtype(dtype).itemsize


def _vmem_limit(resident, streamed, scratch):
    need = sum(resident) + 2 * sum(streamed) + sum(scratch)
    assert need <= V7X_SCOPED_VMEM_BYTES, need
    return V7X_SCOPED_VMEM_BYTES


def kernel(x, mem, norm_mix, w_in, conv_w, w_conv_out, w_pool, pool_scale, norm_mem, w_kv,
           w_xattn_out, w_out, norm_ffn, w_gate, w_up, w_down, norm_final):
    batch, seq, d = x.shape
    assert d == D_MODEL and mem.shape == (batch, N_MEM, D_MODEL)
    assert w_in.shape == (1, D_MODEL, D_IN) and w_gate.shape == (1, D_MODEL, D_FF)
    tile = TOKEN_TILE
    ffn_rows = FFN_STEP_TILES * tile
    assert seq % tile == 0 and (batch * seq) % ffn_rows == 0
    row = lambda a: a.reshape(1, -1).astype(F32)
    col = lambda a: a.reshape(-1, 1).astype(F32)
    sq = (D_MODEL, D_MODEL)

    prep_resident = [_nbytes((D_MODEL, 2 * D_MODEL), F32), 2 * _nbytes(sq, F32)]
    prep_streamed = [_nbytes((N_MEM, D_MODEL), F32), 2 * _nbytes(sq, BF16)]
    prep_scratch = [_nbytes((D_MODEL, 2 * D_MODEL), BF16), 2 * _nbytes(sq, BF16)]
    sm, vo = pl.pallas_call(
        _prep_kernel,
        grid=(batch,),
        in_specs=[
            pl.BlockSpec((1, N_MEM, D_MODEL), lambda b: (b, 0, 0)),
            _resident((1, D_MODEL)),
            _resident((D_MODEL, 2 * D_MODEL)),
            _resident(sq, (0, SRC_Q_X)),
            _resident((D_MODEL, 1)),
            _resident(sq),
        ],
        out_specs=[
            pl.BlockSpec((1,) + sq, lambda b: (b, 0, 0)),
            pl.BlockSpec((1,) + sq, lambda b: (b, 0, 0)),
        ],
        out_shape=[
            jax.ShapeDtypeStruct((batch,) + sq, BF16),
            jax.ShapeDtypeStruct((batch,) + sq, BF16),
        ],
        scratch_shapes=[
            pltpu.VMEM((D_MODEL, 2 * D_MODEL), BF16),
            pltpu.VMEM(sq, BF16),
            pltpu.VMEM(sq, BF16),
        ],
        compiler_params=pltpu.CompilerParams(
            dimension_semantics=("arbitrary",),
            vmem_limit_bytes=_vmem_limit(prep_resident, prep_streamed, prep_scratch)),
        name="prep",
    )(mem, row(norm_mem[0]), w_kv[0], w_in[0], col(norm_mix[0]), w_xattn_out[0])

    lead = LEAD_STEPS
    rb = D_MODEL // lead
    n_t = seq // tile
    n_steps = batch * n_t
    gu_rb = D_MODEL // n_steps
    d_steps = n_steps // 2
    d_rb = D_FF // d_steps
    assert rb % BF16_SUBLANES == 0 and gu_rb % BF16_SUBLANES == 0 and d_rb % BF16_SUBLANES == 0
    tok = lambda s: jnp.maximum(s - lead, 0)
    tile_block = lambda s: (tok(s) // n_t, tok(s) % n_t, 0)
    batch_block = lambda s: (tok(s) // n_t, 0, 0)
    lead_block = lambda s: (jnp.minimum(s, lead - 1), 0)
    step_block = lambda s: (tok(s), 0)
    down_block = lambda s: (jnp.minimum(tok(s), d_steps - 1), 0)
    mixer_resident = [_nbytes((N_POOL, CH, CH), F32)]
    mixer_streamed = [
        2 * _nbytes((tile, D_MODEL), F32), 2 * _nbytes(sq, BF16),
        _nbytes((rb, D_IN), F32), 2 * _nbytes((rb, D_MODEL), F32),
        2 * _nbytes((gu_rb, D_FF), F32), _nbytes((d_rb, D_MODEL), F32),
        2 * _nbytes((gu_rb, D_FF), BF16), _nbytes((d_rb, D_MODEL), BF16)]
    mixer_scratch = [
        _nbytes((D_MODEL, D_MIX), BF16), 2 * _nbytes(sq, BF16),
        2 * _nbytes((tile, D_MODEL), BF16), 2 * _nbytes((HALO + tile, D_MODEL), F32),
        _nbytes((tile, CH), F32)]
    x1, w_gate_b, w_up_b, w_down_b = pl.pallas_call(
        functools.partial(_mixer_kernel, lead=lead, n_t=n_t),
        grid=(lead + n_steps,),
        in_specs=[
            pl.BlockSpec((1, tile, D_MODEL), tile_block),
            pl.BlockSpec((1,) + sq, batch_block),
            pl.BlockSpec((1,) + sq, batch_block),
            _resident((CONV_K, D_MODEL)),
            pl.BlockSpec((rb, D_IN), lead_block),
            pl.BlockSpec((rb, 1), lead_block),
            pl.BlockSpec((rb, D_MODEL), lead_block),
            pl.BlockSpec((rb, D_MODEL), lead_block),
            _resident((N_POOL, CH, CH)),
            _resident((1, D_MODEL)),
            pl.BlockSpec((gu_rb, D_FF), step_block),
            pl.BlockSpec((gu_rb, D_FF), step_block),
            pl.BlockSpec((d_rb, D_MODEL), down_block),
            pl.BlockSpec((gu_rb, 1), step_block),
        ],
        out_specs=[
            pl.BlockSpec((1, tile, D_MODEL), tile_block),
            pl.BlockSpec((gu_rb, D_FF), step_block),
            pl.BlockSpec((gu_rb, D_FF), step_block),
            pl.BlockSpec((d_rb, D_MODEL), down_block),
        ],
        out_shape=[
            jax.ShapeDtypeStruct((batch, seq, D_MODEL), F32),
            jax.ShapeDtypeStruct((D_MODEL, D_FF), BF16),
            jax.ShapeDtypeStruct((D_MODEL, D_FF), BF16),
            jax.ShapeDtypeStruct((D_FF, D_MODEL), BF16),
        ],
        scratch_shapes=[
            pltpu.VMEM((D_MODEL, D_MIX), BF16),
            pltpu.VMEM(sq, BF16),
            pltpu.VMEM(sq, BF16),
            pltpu.VMEM((tile, D_MODEL), BF16),
            pltpu.VMEM((tile, CH), F32),
            pltpu.VMEM((HALO + tile, D_MODEL), F32),
            pltpu.VMEM((HALO + tile, D_MODEL), F32),
            pltpu.VMEM((tile, D_MODEL), BF16),
        ],
        compiler_params=pltpu.CompilerParams(
            dimension_semantics=("arbitrary",),
            vmem_limit_bytes=_vmem_limit(mixer_resident, mixer_streamed, mixer_scratch)),
        name="mixer",
    )(x, sm, vo, conv_w[0].astype(F32), w_in[0], col(norm_mix[0]), w_conv_out[0], w_out[0],
      w_pool[0], row(pool_scale[0]), w_gate[0], w_up[0], w_down[0], col(norm_ffn[0]))

    n_tok = batch * seq
    ffn_resident = [3 * _nbytes((D_MODEL, D_FF), BF16)]
    ffn_streamed = [2 * _nbytes((ffn_rows, D_MODEL), F32)]
    ffn_scratch = [
        _nbytes((ffn_rows, D_MODEL), BF16), _nbytes((ffn_rows, D_FF), BF16), _nbytes((ffn_rows, CH), F32)]
    out = pl.pallas_call(
        _ffn_kernel,
        grid=(n_tok // ffn_rows,),
        in_specs=[
            pl.BlockSpec((ffn_rows, D_MODEL), lambda i: (i, 0)),
            _resident((D_MODEL, D_FF)),
            _resident((D_MODEL, D_FF)),
            _resident((D_FF, D_MODEL)),
            _resident((1, D_MODEL)),
        ],
        out_specs=pl.BlockSpec((ffn_rows, D_MODEL), lambda i: (i, 0)),
        out_shape=jax.ShapeDtypeStruct((n_tok, D_MODEL), F32),
        scratch_shapes=[
            pltpu.VMEM((FFN_STEP_TILES, tile, D_MODEL), BF16),
            pltpu.VMEM((FFN_STEP_TILES, tile, CH), F32),
            pltpu.VMEM((FFN_STEP_TILES, tile, D_FF), BF16),
        ],
        compiler_params=pltpu.CompilerParams(
            dimension_semantics=("arbitrary",),
            vmem_limit_bytes=_vmem_limit(ffn_resident, ffn_streamed, ffn_scratch)),
        name="ffn",
    )(x1.reshape(n_tok, D_MODEL), w_gate_b, w_up_b, w_down_b, row(norm_final))
    return out.reshape(batch, seq, D_MODEL)
```

```python
import functools

import jax
import jax.numpy as jnp
from jax import lax
from jax.experimental import pallas as pl
from jax.experimental.pallas import tpu as pltpu

F32 = jnp.float32
BF16 = jnp.bfloat16

D_MODEL = 1024
N_MEM = 256
CONV_K = 3
POOL_WINDOWS = (2, 4, 8, 16)
N_POOL = len(POOL_WINDOWS)
X_HEADS = 4
X_HEAD_DIM = D_MODEL // X_HEADS
D_FF = 2816
EPS = 1e-6

SRC_B_A, SRC_C_A, SRC_U_A, SRC_U_P, SRC_Q_X, SRC_G_A, SRC_G_P, SRC_G_X = range(8)
D_IN = 8 * D_MODEL
MIX_SRC = (SRC_B_A, SRC_C_A, SRC_U_A, SRC_U_P, SRC_G_A, SRC_G_P, SRC_G_X)
COL_B_A, COL_C_A, COL_U_A, COL_U_P, COL_G_A, COL_G_P, COL_G_X = (
    i * D_MODEL for i in range(len(MIX_SRC)))
D_MIX = len(MIX_SRC) * D_MODEL

V7X_MXU_WIDTH = 256
V7X_VMEM_BYTES = 64 * 1024 * 1024
V7X_SCOPED_VMEM_BYTES = V7X_VMEM_BYTES - 4 * 1024 * 1024
BF16_SUBLANES = 16
HALO = 16
TOKEN_TILE = 512
FFN_STEP_TILES = 2
CH = V7X_MXU_WIDTH
N_CH = D_MODEL // CH
LEAD_STEPS = 8

assert X_HEAD_DIM == CH and N_MEM == CH and D_MODEL // N_POOL == CH


def _inv_rms(x):
    return lax.rsqrt(jnp.mean(x * x, axis=-1, keepdims=True) + EPS)


def _rms_norm(x, gain):
    return (x * _inv_rms(x)) * gain


def _sigmoid(x):
    return 1.0 / (1.0 + jnp.exp(-x))


def _dot(a, b):
    return jnp.dot(a, b, preferred_element_type=F32)


def _dot_bt(a, b):
    return lax.dot_general(a, b, (((1,), (1,)), ((), ())), preferred_element_type=F32)


def _cols(j):
    return slice(j * CH, (j + 1) * CH)


def _prep_kernel(mem_ref, gain_ref, wkv_ref, wq_ref, wxo_ref, sm_ref, vo_ref,
                 wkv_buf, wq_buf, wxo_buf):
    @pl.when(pl.program_id(0) == 0)
    def _():
        wkv_buf[...] = wkv_ref[...].astype(BF16)
        wq_buf[...] = wq_ref[...].astype(BF16)
        wxo_buf[...] = wxo_ref[...].astype(BF16)

    mn = _rms_norm(mem_ref[0], gain_ref[...]).astype(BF16)
    scale = X_HEAD_DIM ** -0.5
    heads = range(X_HEADS)
    k = [(_dot(mn, wkv_buf[:, _cols(h)]) * scale).astype(BF16) for h in heads]
    v = [_dot(mn, wkv_buf[:, D_MODEL + h * CH:D_MODEL + (h + 1) * CH]).astype(BF16) for h in heads]
    for h in heads:
        sm_ref[0, :, _cols(h)] = _dot_bt(wq_buf[:, _cols(h)], k[h]).astype(BF16)
    for h in heads:
        vo_ref[0, _cols(h), :] = _dot(v[h], wxo_buf[_cols(h), :]).astype(BF16)


def _convert_mixer_weights(i, win_ref, wco_ref, wout_ref, wpool_ref, pscale_ref, win_b, wco_b, wout_b):
    rb = win_ref.shape[0]
    rows = pl.ds(pl.multiple_of(i * rb, rb), rb)
    for dst, src in enumerate(MIX_SRC):
        if src != SRC_U_P:
            win_b[rows, dst * D_MODEL:(dst + 1) * D_MODEL] = (
                win_ref[:, src * D_MODEL:(src + 1) * D_MODEL].astype(BF16))
            continue
        for g in range(N_POOL):
            w_up = win_ref[:, src * D_MODEL + g * CH:src * D_MODEL + (g + 1) * CH].astype(BF16)
            folded = _dot(w_up, wpool_ref[g].astype(BF16)) * pscale_ref[:, _cols(g)]
            win_b[rows, dst * D_MODEL + g * CH:dst * D_MODEL + (g + 1) * CH] = folded.astype(BF16)
    wco_b[rows, :] = wco_ref[...].astype(BF16)
    wout_b[rows, :] = wout_ref[...].astype(BF16)


def _mixer_kernel(x_ref, sm_ref, vo_ref, gain_ref, convw_ref,
                  win_f32, wco_f32, wout_f32, wpool_ref, pscale_ref,
                  wg_ref, wu_ref, wd_ref,
                  o_ref, wg_o, wu_o, wd_o,
                  win_ref, wco_ref, wout_ref, h_buf, cu_buf, up_buf, z_buf, *, lead, n_t):
    s = pl.program_id(0)

    @pl.when(s < lead)
    def _():
        _convert_mixer_weights(s, win_f32, wco_f32, wout_f32, wpool_ref, pscale_ref,
                               win_ref, wco_ref, wout_ref)

    @pl.when(s >= lead)
    def _():
        _mixer_tile(lax.rem(s - lead, n_t), x_ref.at[0], o_ref.at[0], sm_ref, vo_ref, gain_ref,
                    convw_ref, win_ref, wco_ref, wout_ref, wg_ref, wu_ref, wd_ref,
                    wg_o, wu_o, wd_o, h_buf, cu_buf, up_buf, z_buf)


def _mixer_tile(t, x_ref, o_ref, sm_ref, vo_ref, gain_ref, convw_ref, win_ref, wco_ref, wout_ref,
                wg_ref, wu_ref, wd_ref, wg_o, wu_o, wd_o, h_buf, cu_buf, up_buf, z_buf):
    tile = x_ref.shape[0]

    @pl.when(t == 0)
    def _():
        cu_buf[0:HALO, :] = jnp.zeros((HALO, D_MODEL), F32)
        up_buf[0:HALO, :] = jnp.zeros((HALO, D_MODEL), F32)

    h_buf[...] = _rms_norm(x_ref[...], gain_ref[...]).astype(BF16)

    def proj(col0, j):
        return _dot(h_buf[...], win_ref[:, col0 + j * CH:col0 + (j + 1) * CH])

    for j in range(N_CH):
        cu_buf[HALO:, _cols(j)] = proj(COL_C_A, j) * proj(COL_U_A, j)
    for j in range(N_CH):
        cw = convw_ref[:, _cols(j)]
        conv = cw[CONV_K - 1:CONV_K] * cu_buf[HALO:, _cols(j)]
        for k in range(CONV_K - 1):
            lag = CONV_K - 1 - k
            conv = conv + cw[k:k + 1] * cu_buf[HALO - lag:HALO - lag + tile, _cols(j)]
        z_buf[:, _cols(j)] = (proj(COL_B_A, j) * conv).astype(BF16)
    cu_buf[0:HALO, :] = cu_buf[tile:tile + HALO, :]
    for j in range(N_CH):
        y = _dot(z_buf[...], wco_ref[:, _cols(j)])
        o_ref[:, _cols(j)] = _sigmoid(proj(COL_G_A, j)) * y

    for g in range(N_POOL):
        up_buf[HALO:, _cols(g)] = proj(COL_U_P, g)
    pos = t * tile + 1 + lax.broadcasted_iota(jnp.int32, (tile, 1), 0)
    scores = []
    for g, w in enumerate(POOL_WINDOWS):
        e = up_buf[:, _cols(g)]
        s = e
        k = 1
        while k < w:
            s = s + pltpu.roll(s, k, 0)
            k *= 2
        inv_cnt = 1.0 / jnp.minimum(pos, w).astype(F32)
        y = s[HALO:] * inv_cnt - e[HALO:]
        o_ref[:, _cols(g)] += _sigmoid(proj(COL_G_P, g)) * y
        scores.append(_dot(h_buf[...], sm_ref[0, :, _cols(g)]))
    up_buf[0:HALO, :] = up_buf[tile:tile + HALO, :]

    for h in range(X_HEADS):
        s = scores[h]
        p = jnp.exp(s - jnp.max(s, axis=-1, keepdims=True))
        inv_l = 1.0 / jnp.sum(p, axis=-1, keepdims=True)
        z_buf[:, _cols(h)] = (p * inv_l).astype(BF16)
    for j in range(N_CH):
        y = _dot(z_buf[...], vo_ref[0, :, _cols(j)])
        o_ref[:, _cols(j)] += _sigmoid(proj(COL_G_X, j)) * y

    h_buf[...] = o_ref[...].astype(BF16)
    for j in range(N_CH):
        o_ref[:, _cols(j)] = x_ref[:, _cols(j)] + _dot(h_buf[...], wout_ref[:, _cols(j)])

    wg_o[...] = wg_ref[...].astype(BF16)
    wu_o[...] = wu_ref[...].astype(BF16)
    wd_o[...] = wd_ref[...].astype(BF16)


def _ffn_tile(x_ref, o_ref, h_buf, r_buf, a_buf, gain_ref, wg_ref, wu_ref, wd_ref, gfin_ref):
    h_buf[...] = (x_ref[...] * gain_ref[...]).astype(BF16)
    r_buf[...] = jnp.broadcast_to(_inv_rms(x_ref[...]), r_buf.shape)
    for c in range(D_FF // CH):
        g = _dot(h_buf[...], wg_ref[:, _cols(c)]) * r_buf[...]
        u = _dot(h_buf[...], wu_ref[:, _cols(c)]) * r_buf[...]
        a_buf[:, _cols(c)] = (g * _sigmoid(g) * u).astype(BF16)
    for j in range(N_CH):
        o_ref[:, _cols(j)] = x_ref[:, _cols(j)] + _dot(a_buf[...], wd_ref[:, _cols(j)])
    o_ref[...] = _rms_norm(o_ref[...], gfin_ref[...])


def _ffn_kernel(x_ref, gain_ref, wg_ref, wu_ref, wd_ref, gfin_ref, o_ref, h_buf, r_buf, a_buf):
    tile = h_buf.shape[1]
    for i in range(x_ref.shape[0] // tile):
        rows = pl.ds(i * tile, tile)
        _ffn_tile(x_ref.at[rows], o_ref.at[rows], h_buf.at[i], r_buf.at[i], a_buf.at[i],
                  gain_ref, wg_ref, wu_ref, wd_ref, gfin_ref)


def _resident(shape, index=None):
    index = (0,) * len(shape) if index is None else index
    return pl.BlockSpec(shape, lambda *_: index, pipeline_mode=pl.Buffered(1))


def _nbytes(shape, dtype):
    n = 1
    for s in shape:
        n *= s
    return n * jnp.dtype(dtype).itemsize


def _vmem_limit(resident, streamed, scratch):
    need = sum(resident) + 2 * sum(streamed) + sum(scratch)
    assert need <= V7X_SCOPED_VMEM_BYTES, need
    return V7X_SCOPED_VMEM_BYTES


def kernel(x, mem, norm_mix, w_in, conv_w, w_conv_out, w_pool, pool_scale, norm_mem, w_kv,
           w_xattn_out, w_out, norm_ffn, w_gate, w_up, w_down, norm_final):
    batch, seq, d = x.shape
    assert d == D_MODEL and mem.shape == (batch, N_MEM, D_MODEL)
    assert w_in.shape == (1, D_MODEL, D_IN) and w_gate.shape == (1, D_MODEL, D_FF)
    tile = TOKEN_TILE
    ffn_rows = FFN_STEP_TILES * tile
    assert seq % tile == 0 and (batch * seq) % ffn_rows == 0
    row = lambda a: a.reshape(1, -1).astype(F32)
    sq = (D_MODEL, D_MODEL)

    prep_resident = [_nbytes((D_MODEL, 2 * D_MODEL), F32), 2 * _nbytes(sq, F32)]
    prep_streamed = [_nbytes((N_MEM, D_MODEL), F32), 2 * _nbytes(sq, BF16)]
    prep_scratch = [_nbytes((D_MODEL, 2 * D_MODEL), BF16), 2 * _nbytes(sq, BF16)]
    sm, vo = pl.pallas_call(
        _prep_kernel,
        grid=(batch,),
        in_specs=[
            pl.BlockSpec((1, N_MEM, D_MODEL), lambda b: (b, 0, 0)),
            _resident((1, D_MODEL)),
            _resident((D_MODEL, 2 * D_MODEL)),
            _resident(sq, (0, SRC_Q_X)),
            _resident(sq),
        ],
        out_specs=[
            pl.BlockSpec((1,) + sq, lambda b: (b, 0, 0)),
            pl.BlockSpec((1,) + sq, lambda b: (b, 0, 0)),
        ],
        out_shape=[
            jax.ShapeDtypeStruct((batch,) + sq, BF16),
            jax.ShapeDtypeStruct((batch,) + sq, BF16),
        ],
        scratch_shapes=[
            pltpu.VMEM((D_MODEL, 2 * D_MODEL), BF16),
            pltpu.VMEM(sq, BF16),
            pltpu.VMEM(sq, BF16),
        ],
        compiler_params=pltpu.CompilerParams(
            dimension_semantics=("arbitrary",),
            vmem_limit_bytes=_vmem_limit(prep_resident, prep_streamed, prep_scratch)),
        name="prep",
    )(mem, row(norm_mem[0]), w_kv[0], w_in[0], w_xattn_out[0])

    lead = LEAD_STEPS
    rb = D_MODEL // lead
    n_t = seq // tile
    n_steps = batch * n_t
    gu_rb = D_MODEL // n_steps
    d_steps = n_steps // 2
    d_rb = D_FF // d_steps
    assert rb % BF16_SUBLANES == 0 and gu_rb % BF16_SUBLANES == 0 and d_rb % BF16_SUBLANES == 0
    tok = lambda s: jnp.maximum(s - lead, 0)
    tile_block = lambda s: (tok(s) // n_t, tok(s) % n_t, 0)
    batch_block = lambda s: (tok(s) // n_t, 0, 0)
    lead_block = lambda s: (jnp.minimum(s, lead - 1), 0)
    step_block = lambda s: (tok(s), 0)
    down_block = lambda s: (jnp.minimum(tok(s), d_steps - 1), 0)
    mixer_resident = [_nbytes((N_POOL, CH, CH), F32)]
    mixer_streamed = [
        2 * _nbytes((tile, D_MODEL), F32), 2 * _nbytes(sq, BF16),
        _nbytes((rb, D_IN), F32), 2 * _nbytes((rb, D_MODEL), F32),
        2 * _nbytes((gu_rb, D_FF), F32), _nbytes((d_rb, D_MODEL), F32),
        2 * _nbytes((gu_rb, D_FF), BF16), _nbytes((d_rb, D_MODEL), BF16)]
    mixer_scratch = [
        _nbytes((D_MODEL, D_MIX), BF16), 2 * _nbytes(sq, BF16),
        2 * _nbytes((tile, D_MODEL), BF16), 2 * _nbytes((HALO + tile, D_MODEL), F32)]
    x1, w_gate_b, w_up_b, w_down_b = pl.pallas_call(
        functools.partial(_mixer_kernel, lead=lead, n_t=n_t),
        grid=(lead + n_steps,),
        in_specs=[
            pl.BlockSpec((1, tile, D_MODEL), tile_block),
            pl.BlockSpec((1,) + sq, batch_block),
            pl.BlockSpec((1,) + sq, batch_block),
            _resident((1, D_MODEL)),
            _resident((CONV_K, D_MODEL)),
            pl.BlockSpec((rb, D_IN), lead_block),
            pl.BlockSpec((rb, D_MODEL), lead_block),
            pl.BlockSpec((rb, D_MODEL), lead_block),
            _resident((N_POOL, CH, CH)),
            _resident((1, D_MODEL)),
            pl.BlockSpec((gu_rb, D_FF), step_block),
            pl.BlockSpec((gu_rb, D_FF), step_block),
            pl.BlockSpec((d_rb, D_MODEL), down_block),
        ],
        out_specs=[
            pl.BlockSpec((1, tile, D_MODEL), tile_block),
            pl.BlockSpec((gu_rb, D_FF), step_block),
            pl.BlockSpec((gu_rb, D_FF), step_block),
            pl.BlockSpec((d_rb, D_MODEL), down_block),
        ],
        out_shape=[
            jax.ShapeDtypeStruct((batch, seq, D_MODEL), F32),
            jax.ShapeDtypeStruct((D_MODEL, D_FF), BF16),
            jax.ShapeDtypeStruct((D_MODEL, D_FF), BF16),
            jax.ShapeDtypeStruct((D_FF, D_MODEL), BF16),
        ],
        scratch_shapes=[
            pltpu.VMEM((D_MODEL, D_MIX), BF16),
            pltpu.VMEM(sq, BF16),
            pltpu.VMEM(sq, BF16),
            pltpu.VMEM((tile, D_MODEL), BF16),
            pltpu.VMEM((HALO + tile, D_MODEL), F32),
            pltpu.VMEM((HALO + tile, D_MODEL), F32),
            pltpu.VMEM((tile, D_MODEL), BF16),
        ],
        compiler_params=pltpu.CompilerParams(
            dimension_semantics=("arbitrary",),
            vmem_limit_bytes=_vmem_limit(mixer_resident, mixer_streamed, mixer_scratch)),
        name="mixer",
    )(x, sm, vo, row(norm_mix[0]), conv_w[0].astype(F32), w_in[0], w_conv_out[0], w_out[0],
      w_pool[0], row(pool_scale[0]), w_gate[0], w_up[0], w_down[0])

    n_tok = batch * seq
    ffn_resident = [3 * _nbytes((D_MODEL, D_FF), BF16)]
    ffn_streamed = [2 * _nbytes((ffn_rows, D_MODEL), F32)]
    ffn_scratch = [
        _nbytes((ffn_rows, D_MODEL), BF16), _nbytes((ffn_rows, D_FF), BF16), _nbytes((ffn_rows, CH), F32)]
    out = pl.pallas_call(
        _ffn_kernel,
        grid=(n_tok // ffn_rows,),
        in_specs=[
            pl.BlockSpec((ffn_rows, D_MODEL), lambda i: (i, 0)),
            _resident((1, D_MODEL)),
            _resident((D_MODEL, D_FF)),
            _resident((D_MODEL, D_FF)),
            _resident((D_FF, D_MODEL)),
            _resident((1, D_MODEL)),
        ],
        out_specs=pl.BlockSpec((ffn_rows, D_MODEL), lambda i: (i, 0)),
        out_shape=jax.ShapeDtypeStruct((n_tok, D_MODEL), F32),
        scratch_shapes=[
            pltpu.VMEM((FFN_STEP_TILES, tile, D_MODEL), BF16),
            pltpu.VMEM((FFN_STEP_TILES, tile, CH), F32),
            pltpu.VMEM((FFN_STEP_TILES, tile, D_FF), BF16),
        ],
        compiler_params=pltpu.CompilerParams(
            dimension_semantics=("arbitrary",),
            vmem_limit_bytes=_vmem_limit(ffn_resident, ffn_streamed, ffn_scratch)),
        name="ffn",
    )(x1.reshape(n_tok, D_MODEL), row(norm_ffn[0]), w_gate_b, w_up_b, w_down_b, row(norm_final))
    return out.reshape(batch, seq, D_MODEL)
```

```python
import functools

import jax
import jax.numpy as jnp
from jax import lax
from jax.experimental import pallas as pl
from jax.experimental.pallas import tpu as pltpu

F32 = jnp.float32
BF16 = jnp.bfloat16

D_MODEL = 1024
N_MEM = 256
CONV_K = 3
POOL_WINDOWS = (2, 4, 8, 16)
N_POOL = len(POOL_WINDOWS)
X_HEADS = 4
X_HEAD_DIM = D_MODEL // X_HEADS
D_FF = 2816
EPS = 1e-6

SRC_B_A, SRC_C_A, SRC_U_A, SRC_U_P, SRC_Q_X, SRC_G_A, SRC_G_P, SRC_G_X = range(8)
D_IN = 8 * D_MODEL
MIX_SRC = (SRC_B_A, SRC_C_A, SRC_U_A, SRC_U_P, SRC_G_A, SRC_G_P, SRC_G_X)
COL_B_A, COL_C_A, COL_U_A, COL_U_P, COL_G_A, COL_G_P, COL_G_X = (
    i * D_MODEL for i in range(len(MIX_SRC)))
D_MIX = len(MIX_SRC) * D_MODEL

V7X_MXU_WIDTH = 256
V7X_VMEM_BYTES = 64 * 1024 * 1024
V7X_SCOPED_VMEM_BYTES = V7X_VMEM_BYTES - 4 * 1024 * 1024
BF16_SUBLANES = 16
HALO = 16
TOKEN_TILE = 512
FFN_STEP_TILES = 2
CH = V7X_MXU_WIDTH
N_CH = D_MODEL // CH
LEAD_STEPS = 8

assert X_HEAD_DIM == CH and N_MEM == CH and D_MODEL // N_POOL == CH


def _rms_norm(x, gain):
    ms = jnp.mean(x * x, axis=-1, keepdims=True)
    return (x * lax.rsqrt(ms + EPS)) * gain


NEG_LOG2_E = -1.4426950408889634


def _sigmoid(x):
    return 1.0 / (1.0 + jnp.exp2(x * NEG_LOG2_E))


def _dot(a, b):
    return jnp.dot(a, b, preferred_element_type=F32)


def _dot_bt(a, b):
    return lax.dot_general(a, b, (((1,), (1,)), ((), ())), preferred_element_type=F32)


def _cols(j):
    return slice(j * CH, (j + 1) * CH)


def _prep_kernel(mem_ref, gain_ref, wkv_ref, wq_ref, wxo_ref, sm_ref, vo_ref,
                 wkv_buf, wq_buf, wxo_buf):
    @pl.when(pl.program_id(0) == 0)
    def _():
        wkv_buf[...] = wkv_ref[...].astype(BF16)
        wq_buf[...] = wq_ref[...].astype(BF16)
        wxo_buf[...] = wxo_ref[...].astype(BF16)

    mn = _rms_norm(mem_ref[0], gain_ref[...]).astype(BF16)
    scale = X_HEAD_DIM ** -0.5 * -NEG_LOG2_E
    heads = range(X_HEADS)
    k = [(_dot(mn, wkv_buf[:, _cols(h)]) * scale).astype(BF16) for h in heads]
    v = [_dot(mn, wkv_buf[:, D_MODEL + h * CH:D_MODEL + (h + 1) * CH]).astype(BF16) for h in heads]
    for h in heads:
        sm_ref[0, :, _cols(h)] = _dot_bt(wq_buf[:, _cols(h)], k[h]).astype(BF16)
    for h in heads:
        vo_ref[0, _cols(h), :] = _dot(v[h], wxo_buf[_cols(h), :]).astype(BF16)


def _convert_mixer_weights(i, win_ref, wco_ref, wout_ref, wpool_ref, pscale_ref, win_b, wco_b, wout_b):
    rb = win_ref.shape[0]
    rows = pl.ds(pl.multiple_of(i * rb, rb), rb)
    for dst, src in enumerate(MIX_SRC):
        if src != SRC_U_P:
            win_b[rows, dst * D_MODEL:(dst + 1) * D_MODEL] = (
                win_ref[:, src * D_MODEL:(src + 1) * D_MODEL].astype(BF16))
            continue
        for g in range(N_POOL):
            w_up = win_ref[:, src * D_MODEL + g * CH:src * D_MODEL + (g + 1) * CH].astype(BF16)
            folded = _dot(w_up, wpool_ref[g].astype(BF16)) * pscale_ref[:, _cols(g)]
            win_b[rows, dst * D_MODEL + g * CH:dst * D_MODEL + (g + 1) * CH] = folded.astype(BF16)
    wco_b[rows, :] = wco_ref[...].astype(BF16)
    wout_b[rows, :] = wout_ref[...].astype(BF16)


def _mixer_kernel(x_ref, sm_ref, vo_ref, gain_ref, convw_ref,
                  win_f32, wco_f32, wout_f32, wpool_ref, pscale_ref,
                  wg_ref, wu_ref, wd_ref,
                  o_ref, wg_o, wu_o, wd_o,
                  win_ref, wco_ref, wout_ref, h_buf, cu_buf, up_buf, z_buf, *, lead, n_t):
    s = pl.program_id(0)

    @pl.when(s < lead)
    def _():
        _convert_mixer_weights(s, win_f32, wco_f32, wout_f32, wpool_ref, pscale_ref,
                               win_ref, wco_ref, wout_ref)

    @pl.when(s >= lead)
    def _():
        _mixer_tile(lax.rem(s - lead, n_t), x_ref.at[0], o_ref.at[0], sm_ref, vo_ref, gain_ref,
                    convw_ref, win_ref, wco_ref, wout_ref, wg_ref, wu_ref, wd_ref,
                    wg_o, wu_o, wd_o, h_buf, cu_buf, up_buf, z_buf)


def _mixer_tile(t, x_ref, o_ref, sm_ref, vo_ref, gain_ref, convw_ref, win_ref, wco_ref, wout_ref,
                wg_ref, wu_ref, wd_ref, wg_o, wu_o, wd_o, h_buf, cu_buf, up_buf, z_buf):
    tile = x_ref.shape[0]

    @pl.when(t == 0)
    def _():
        cu_buf[0:HALO, :] = jnp.zeros((HALO, D_MODEL), F32)
        up_buf[0:HALO, :] = jnp.zeros((HALO, D_MODEL), F32)

    h_buf[...] = _rms_norm(x_ref[...], gain_ref[...]).astype(BF16)

    def proj(col0, j):
        return _dot(h_buf[...], win_ref[:, col0 + j * CH:col0 + (j + 1) * CH])

    for j in range(N_CH):
        cu_buf[HALO:, _cols(j)] = proj(COL_C_A, j) * proj(COL_U_A, j)
    for j in range(N_CH):
        cw = convw_ref[:, _cols(j)]
        conv = cw[CONV_K - 1:CONV_K] * cu_buf[HALO:, _cols(j)]
        for k in range(CONV_K - 1):
            lag = CONV_K - 1 - k
            conv = conv + cw[k:k + 1] * cu_buf[HALO - lag:HALO - lag + tile, _cols(j)]
        z_buf[:, _cols(j)] = (proj(COL_B_A, j) * conv).astype(BF16)
    cu_buf[0:HALO, :] = cu_buf[tile:tile + HALO, :]
    for j in range(N_CH):
        y = _dot(z_buf[...], wco_ref[:, _cols(j)])
        o_ref[:, _cols(j)] = _sigmoid(proj(COL_G_A, j)) * y

    for g in range(N_POOL):
        up_buf[HALO:, _cols(g)] = proj(COL_U_P, g)
    pos = t * tile + 1 + lax.broadcasted_iota(jnp.int32, (tile, 1), 0)
    scores = []
    for g, w in enumerate(POOL_WINDOWS):
        e = up_buf[:, _cols(g)]
        s = e
        k = 1
        while k < w:
            s = s + pltpu.roll(s, k, 0)
            k *= 2
        inv_cnt = 1.0 / jnp.minimum(pos, w).astype(F32)
        y = s[HALO:] * inv_cnt - e[HALO:]
        o_ref[:, _cols(g)] += _sigmoid(proj(COL_G_P, g)) * y
        scores.append(_dot(h_buf[...], sm_ref[0, :, _cols(g)]))
    up_buf[0:HALO, :] = up_buf[tile:tile + HALO, :]

    for h in range(X_HEADS):
        s = scores[h]
        p = jnp.exp2(s - jnp.max(s, axis=-1, keepdims=True))
        inv_l = 1.0 / jnp.sum(p, axis=-1, keepdims=True)
        z_buf[:, _cols(h)] = (p * inv_l).astype(BF16)
    for j in range(N_CH):
        y = _dot(z_buf[...], vo_ref[0, :, _cols(j)])
        o_ref[:, _cols(j)] += _sigmoid(proj(COL_G_X, j)) * y

    h_buf[...] = o_ref[...].astype(BF16)
    for j in range(N_CH):
        o_ref[:, _cols(j)] = x_ref[:, _cols(j)] + _dot(h_buf[...], wout_ref[:, _cols(j)])

    wg_o[...] = wg_ref[...].astype(BF16)
    wu_o[...] = wu_ref[...].astype(BF16)
    wd_o[...] = wd_ref[...].astype(BF16)


def _ffn_tile(x_ref, o_ref, h_buf, a_buf, gain_ref, wg_ref, wu_ref, wd_ref, gfin_ref):
    h_buf[...] = _rms_norm(x_ref[...], gain_ref[...]).astype(BF16)
    for c in range(D_FF // CH):
        g = _dot(h_buf[...], wg_ref[:, _cols(c)])
        u = _dot(h_buf[...], wu_ref[:, _cols(c)])
        a_buf[:, _cols(c)] = (g * _sigmoid(g) * u).astype(BF16)
    for j in range(N_CH):
        o_ref[:, _cols(j)] = x_ref[:, _cols(j)] + _dot(a_buf[...], wd_ref[:, _cols(j)])
    o_ref[...] = _rms_norm(o_ref[...], gfin_ref[...])


def _ffn_kernel(x_ref, gain_ref, wg_ref, wu_ref, wd_ref, gfin_ref, o_ref, h_buf, a_buf):
    tile = h_buf.shape[1]
    for i in range(x_ref.shape[0] // tile):
        rows = pl.ds(i * tile, tile)
        _ffn_tile(x_ref.at[rows], o_ref.at[rows], h_buf.at[i], a_buf.at[i],
                  gain_ref, wg_ref, wu_ref, wd_ref, gfin_ref)


def _resident(shape, index=None):
    index = (0,) * len(shape) if index is None else index
    return pl.BlockSpec(shape, lambda *_: index, pipeline_mode=pl.Buffered(1))


def _nbytes(shape, dtype):
    n = 1
    for s in shape:
        n *= s
    return n * jnp.dtype(dtype).itemsize


def _vmem_limit(resident, streamed, scratch):
    need = sum(resident) + 2 * sum(streamed) + sum(scratch)
    assert need <= V7X_SCOPED_VMEM_BYTES, need
    return V7X_SCOPED_VMEM_BYTES


def kernel(x, mem, norm_mix, w_in, conv_w, w_conv_out, w_pool, pool_scale, norm_mem, w_kv,
           w_xattn_out, w_out, norm_ffn, w_gate, w_up, w_down, norm_final):
    batch, seq, d = x.shape
    assert d == D_MODEL and mem.shape == (batch, N_MEM, D_MODEL)
    assert w_in.shape == (1, D_MODEL, D_IN) and w_gate.shape == (1, D_MODEL, D_FF)
    tile = TOKEN_TILE
    ffn_rows = FFN_STEP_TILES * tile
    assert seq % tile == 0 and (batch * seq) % ffn_rows == 0
    row = lambda a: a.reshape(1, -1).astype(F32)
    sq = (D_MODEL, D_MODEL)

    prep_resident = [_nbytes((D_MODEL, 2 * D_MODEL), F32), 2 * _nbytes(sq, F32)]
    prep_streamed = [_nbytes((N_MEM, D_MODEL), F32), 2 * _nbytes(sq, BF16)]
    prep_scratch = [_nbytes((D_MODEL, 2 * D_MODEL), BF16), 2 * _nbytes(sq, BF16)]
    sm, vo = pl.pallas_call(
        _prep_kernel,
        grid=(batch,),
        in_specs=[
            pl.BlockSpec((1, N_MEM, D_MODEL), lambda b: (b, 0, 0)),
            _resident((1, D_MODEL)),
            _resident((D_MODEL, 2 * D_MODEL)),
            _resident(sq, (0, SRC_Q_X)),
            _resident(sq),
        ],
        out_specs=[
            pl.BlockSpec((1,) + sq, lambda b: (b, 0, 0)),
            pl.BlockSpec((1,) + sq, lambda b: (b, 0, 0)),
        ],
        out_shape=[
            jax.ShapeDtypeStruct((batch,) + sq, BF16),
            jax.ShapeDtypeStruct((batch,) + sq, BF16),
        ],
        scratch_shapes=[
            pltpu.VMEM((D_MODEL, 2 * D_MODEL), BF16),
            pltpu.VMEM(sq, BF16),
            pltpu.VMEM(sq, BF16),
        ],
        compiler_params=pltpu.CompilerParams(
            dimension_semantics=("arbitrary",),
            vmem_limit_bytes=_vmem_limit(prep_resident, prep_streamed, prep_scratch)),
        name="prep",
    )(mem, row(norm_mem[0]), w_kv[0], w_in[0], w_xattn_out[0])

    lead = LEAD_STEPS
    rb = D_MODEL // lead
    n_t = seq // tile
    n_steps = batch * n_t
    gu_rb = D_MODEL // n_steps
    d_steps = n_steps // 2
    d_rb = D_FF // d_steps
    assert rb % BF16_SUBLANES == 0 and gu_rb % BF16_SUBLANES == 0 and d_rb % BF16_SUBLANES == 0
    tok = lambda s: jnp.maximum(s - lead, 0)
    tile_block = lambda s: (tok(s) // n_t, tok(s) % n_t, 0)
    batch_block = lambda s: (tok(s) // n_t, 0, 0)
    lead_block = lambda s: (jnp.minimum(s, lead - 1), 0)
    step_block = lambda s: (tok(s), 0)
    down_block = lambda s: (jnp.minimum(tok(s), d_steps - 1), 0)
    mixer_resident = [_nbytes((N_POOL, CH, CH), F32)]
    mixer_streamed = [
        2 * _nbytes((tile, D_MODEL), F32), 2 * _nbytes(sq, BF16),
        _nbytes((rb, D_IN), F32), 2 * _nbytes((rb, D_MODEL), F32),
        2 * _nbytes((gu_rb, D_FF), F32), _nbytes((d_rb, D_MODEL), F32),
        2 * _nbytes((gu_rb, D_FF), BF16), _nbytes((d_rb, D_MODEL), BF16)]
    mixer_scratch = [
        _nbytes((D_MODEL, D_MIX), BF16), 2 * _nbytes(sq, BF16),
        2 * _nbytes((tile, D_MODEL), BF16), 2 * _nbytes((HALO + tile, D_MODEL), F32)]
    x1, w_gate_b, w_up_b, w_down_b = pl.pallas_call(
        functools.partial(_mixer_kernel, lead=lead, n_t=n_t),
        grid=(lead + n_steps,),
        in_specs=[
            pl.BlockSpec((1, tile, D_MODEL), tile_block),
            pl.BlockSpec((1,) + sq, batch_block),
            pl.BlockSpec((1,) + sq, batch_block),
            _resident((1, D_MODEL)),
            _resident((CONV_K, D_MODEL)),
            pl.BlockSpec((rb, D_IN), lead_block),
            pl.BlockSpec((rb, D_MODEL), lead_block),
            pl.BlockSpec((rb, D_MODEL), lead_block),
            _resident((N_POOL, CH, CH)),
            _resident((1, D_MODEL)),
            pl.BlockSpec((gu_rb, D_FF), step_block),
            pl.BlockSpec((gu_rb, D_FF), step_block),
            pl.BlockSpec((d_rb, D_MODEL), down_block),
        ],
        out_specs=[
            pl.BlockSpec((1, tile, D_MODEL), tile_block),
            pl.BlockSpec((gu_rb, D_FF), step_block),
            pl.BlockSpec((gu_rb, D_FF), step_block),
            pl.BlockSpec((d_rb, D_MODEL), down_block),
        ],
        out_shape=[
            jax.ShapeDtypeStruct((batch, seq, D_MODEL), F32),
            jax.ShapeDtypeStruct((D_MODEL, D_FF), BF16),
            jax.ShapeDtypeStruct((D_MODEL, D_FF), BF16),
            jax.ShapeDtypeStruct((D_FF, D_MODEL), BF16),
        ],
        scratch_shapes=[
            pltpu.VMEM((D_MODEL, D_MIX), BF16),
            pltpu.VMEM(sq, BF16),
            pltpu.VMEM(sq, BF16),
            pltpu.VMEM((tile, D_MODEL), BF16),
            pltpu.VMEM((HALO + tile, D_MODEL), F32),
            pltpu.VMEM((HALO + tile, D_MODEL), F32),
            pltpu.VMEM((tile, D_MODEL), BF16),
        ],
        compiler_params=pltpu.CompilerParams(
            dimension_semantics=("arbitrary",),
            vmem_limit_bytes=_vmem_limit(mixer_resident, mixer_streamed, mixer_scratch)),
        name="mixer",
    )(x, sm, vo, row(norm_mix[0]), conv_w[0].astype(F32), w_in[0], w_conv_out[0], w_out[0],
      w_pool[0], row(pool_scale[0]), w_gate[0], w_up[0], w_down[0])

    n_tok = batch * seq
    ffn_resident = [3 * _nbytes((D_MODEL, D_FF), BF16)]
    ffn_streamed = [2 * _nbytes((ffn_rows, D_MODEL), F32)]
    ffn_scratch = [_nbytes((ffn_rows, D_MODEL), BF16), _nbytes((ffn_rows, D_FF), BF16)]
    out = pl.pallas_call(
        _ffn_kernel,
        grid=(n_tok // ffn_rows,),
        in_specs=[
            pl.BlockSpec((ffn_rows, D_MODEL), lambda i: (i, 0)),
            _resident((1, D_MODEL)),
            _resident((D_MODEL, D_FF)),
            _resident((D_MODEL, D_FF)),
            _resident((D_FF, D_MODEL)),
            _resident((1, D_MODEL)),
        ],
        out_specs=pl.BlockSpec((ffn_rows, D_MODEL), lambda i: (i, 0)),
        out_shape=jax.ShapeDtypeStruct((n_tok, D_MODEL), F32),
        scratch_shapes=[
            pltpu.VMEM((FFN_STEP_TILES, tile, D_MODEL), BF16),
            pltpu.VMEM((FFN_STEP_TILES, tile, D_FF), BF16),
        ],
        compiler_params=pltpu.CompilerParams(
            dimension_semantics=("arbitrary",),
            vmem_limit_bytes=_vmem_limit(ffn_resident, ffn_streamed, ffn_scratch)),
        name="ffn",
    )(x1.reshape(n_tok, D_MODEL), row(norm_ffn[0]), w_gate_b, w_up_b, w_down_b, row(norm_final))
    return out.reshape(batch, seq, D_MODEL)
```

```python
import functools

import jax
import jax.numpy as jnp
from jax import lax
from jax.experimental import pallas as pl
from jax.experimental.pallas import tpu as pltpu

F32 = jnp.float32
BF16 = jnp.bfloat16

D_MODEL = 1024
N_MEM = 256
CONV_K = 3
POOL_WINDOWS = (2, 4, 8, 16)
N_POOL = len(POOL_WINDOWS)
X_HEADS = 4
X_HEAD_DIM = D_MODEL // X_HEADS
D_FF = 2816
EPS = 1e-6

SRC_B_A, SRC_C_A, SRC_U_A, SRC_U_P, SRC_Q_X, SRC_G_A, SRC_G_P, SRC_G_X = range(8)
D_IN = 8 * D_MODEL
MIX_SRC = (SRC_B_A, SRC_C_A, SRC_U_A, SRC_U_P, SRC_G_A, SRC_G_P, SRC_G_X)
COL_B_A, COL_C_A, COL_U_A, COL_U_P, COL_G_A, COL_G_P, COL_G_X = (
    i * D_MODEL for i in range(len(MIX_SRC)))
D_MIX = len(MIX_SRC) * D_MODEL

V7X_MXU_WIDTH = 256
V7X_VMEM_BYTES = 64 * 1024 * 1024
V7X_SCOPED_VMEM_BYTES = V7X_VMEM_BYTES - 4 * 1024 * 1024
BF16_SUBLANES = 16
HALO = 16
TOKEN_TILE = 512
FFN_TILE = 256
FFN_STEP_TILES = 4
FFN_EARLY_CHUNKS = 1
CH = V7X_MXU_WIDTH
N_CH = D_MODEL // CH
LEAD_STEPS = 8

assert X_HEAD_DIM == CH and N_MEM == CH and D_MODEL // N_POOL == CH


def _rms_norm(x, gain):
    ms = jnp.mean(x * x, axis=-1, keepdims=True)
    return (x * lax.rsqrt(ms + EPS)) * gain


NEG_LOG2_E = -1.4426950408889634


def _sigmoid(x):
    return 1.0 / (1.0 + jnp.exp2(x * NEG_LOG2_E))


def _dot(a, b):
    return jnp.dot(a, b, preferred_element_type=F32)


def _dot_bt(a, b):
    return lax.dot_general(a, b, (((1,), (1,)), ((), ())), preferred_element_type=F32)


def _cols(j):
    return slice(j * CH, (j + 1) * CH)


def _prep_kernel(mem_ref, gain_ref, wkv_ref, wq_ref, wxo_ref, sm_ref, vo_ref,
                 wkv_buf, wq_buf, wxo_buf):
    @pl.when(pl.program_id(0) == 0)
    def _():
        wkv_buf[...] = wkv_ref[...].astype(BF16)
        wq_buf[...] = wq_ref[...].astype(BF16)
        wxo_buf[...] = wxo_ref[...].astype(BF16)

    mn = _rms_norm(mem_ref[0], gain_ref[...]).astype(BF16)
    scale = X_HEAD_DIM ** -0.5 * -NEG_LOG2_E
    heads = range(X_HEADS)
    k = [(_dot(mn, wkv_buf[:, _cols(h)]) * scale).astype(BF16) for h in heads]
    v = [_dot(mn, wkv_buf[:, D_MODEL + h * CH:D_MODEL + (h + 1) * CH]).astype(BF16) for h in heads]
    for h in heads:
        sm_ref[0, :, _cols(h)] = _dot_bt(wq_buf[:, _cols(h)], k[h]).astype(BF16)
    for h in heads:
        vo_ref[0, _cols(h), :] = _dot(v[h], wxo_buf[_cols(h), :]).astype(BF16)


def _convert_mixer_weights(i, win_ref, wco_ref, wout_ref, wpool_ref, pscale_ref, win_b, wco_b, wout_b):
    rb = win_ref.shape[0]
    rows = pl.ds(pl.multiple_of(i * rb, rb), rb)
    for dst, src in enumerate(MIX_SRC):
        if src != SRC_U_P:
            win_b[rows, dst * D_MODEL:(dst + 1) * D_MODEL] = (
                win_ref[:, src * D_MODEL:(src + 1) * D_MODEL].astype(BF16))
            continue
        for g in range(N_POOL):
            w_up = win_ref[:, src * D_MODEL + g * CH:src * D_MODEL + (g + 1) * CH].astype(BF16)
            folded = _dot(w_up, wpool_ref[g].astype(BF16)) * pscale_ref[:, _cols(g)]
            win_b[rows, dst * D_MODEL + g * CH:dst * D_MODEL + (g + 1) * CH] = folded.astype(BF16)
    wco_b[rows, :] = wco_ref[...].astype(BF16)
    wout_b[rows, :] = wout_ref[...].astype(BF16)


def _mixer_kernel(x_ref, sm_ref, vo_ref, gain_ref, convw_ref,
                  win_f32, wco_f32, wout_f32, wpool_ref, pscale_ref,
                  wg_ref, wu_ref, wd_ref,
                  o_ref, wg_o, wu_o, wd_o,
                  win_ref, wco_ref, wout_ref, h_buf, cu_buf, up_buf, z_buf, *, lead, n_t):
    s = pl.program_id(0)

    @pl.when(s < lead)
    def _():
        _convert_mixer_weights(s, win_f32, wco_f32, wout_f32, wpool_ref, pscale_ref,
                               win_ref, wco_ref, wout_ref)

    @pl.when(s >= lead)
    def _():
        _mixer_tile(lax.rem(s - lead, n_t), x_ref.at[0], o_ref.at[0], sm_ref, vo_ref, gain_ref,
                    convw_ref, win_ref, wco_ref, wout_ref, wg_ref, wu_ref, wd_ref,
                    wg_o, wu_o, wd_o, h_buf, cu_buf, up_buf, z_buf)


def _mixer_tile(t, x_ref, o_ref, sm_ref, vo_ref, gain_ref, convw_ref, win_ref, wco_ref, wout_ref,
                wg_ref, wu_ref, wd_ref, wg_o, wu_o, wd_o, h_buf, cu_buf, up_buf, z_buf):
    tile = x_ref.shape[0]

    @pl.when(t == 0)
    def _():
        cu_buf[0:HALO, :] = jnp.zeros((HALO, D_MODEL), F32)
        up_buf[0:HALO, :] = jnp.zeros((HALO, D_MODEL), F32)

    h_buf[...] = _rms_norm(x_ref[...], gain_ref[...]).astype(BF16)

    def proj(col0, j):
        return _dot(h_buf[...], win_ref[:, col0 + j * CH:col0 + (j + 1) * CH])

    for j in range(N_CH):
        cu_buf[HALO:, _cols(j)] = proj(COL_C_A, j) * proj(COL_U_A, j)
    for j in range(N_CH):
        cw = convw_ref[:, _cols(j)]
        conv = cw[CONV_K - 1:CONV_K] * cu_buf[HALO:, _cols(j)]
        for k in range(CONV_K - 1):
            lag = CONV_K - 1 - k
            conv = conv + cw[k:k + 1] * cu_buf[HALO - lag:HALO - lag + tile, _cols(j)]
        z_buf[:, _cols(j)] = (proj(COL_B_A, j) * conv).astype(BF16)
    cu_buf[0:HALO, :] = cu_buf[tile:tile + HALO, :]
    for j in range(N_CH):
        y = _dot(z_buf[...], wco_ref[:, _cols(j)])
        o_ref[:, _cols(j)] = _sigmoid(proj(COL_G_A, j)) * y

    for g in range(N_POOL):
        up_buf[HALO:, _cols(g)] = proj(COL_U_P, g)
    pos = t * tile + 1 + lax.broadcasted_iota(jnp.int32, (tile, 1), 0)
    scores = []
    for g, w in enumerate(POOL_WINDOWS):
        e = up_buf[:, _cols(g)]
        s = e
        k = 1
        while k < w:
            s = s + pltpu.roll(s, k, 0)
            k *= 2
        inv_cnt = 1.0 / jnp.minimum(pos, w).astype(F32)
        y = s[HALO:] * inv_cnt - e[HALO:]
        o_ref[:, _cols(g)] += _sigmoid(proj(COL_G_P, g)) * y
        scores.append(_dot(h_buf[...], sm_ref[0, :, _cols(g)]))
    up_buf[0:HALO, :] = up_buf[tile:tile + HALO, :]

    for h in range(X_HEADS):
        s = scores[h]
        p = jnp.exp2(s - jnp.max(s, axis=-1, keepdims=True))
        inv_l = 1.0 / jnp.sum(p, axis=-1, keepdims=True)
        z_buf[:, _cols(h)] = (p * inv_l).astype(BF16)
    for j in range(N_CH):
        y = _dot(z_buf[...], vo_ref[0, :, _cols(j)])
        o_ref[:, _cols(j)] += _sigmoid(proj(COL_G_X, j)) * y

    h_buf[...] = o_ref[...].astype(BF16)
    for j in range(N_CH):
        o_ref[:, _cols(j)] = x_ref[:, _cols(j)] + _dot(h_buf[...], wout_ref[:, _cols(j)])

    wg_o[...] = wg_ref[...].astype(BF16)
    wu_o[...] = wu_ref[...].astype(BF16)
    wd_o[...] = wd_ref[...].astype(BF16)


def _ffn_kernel(x_ref, gain_ref, wg_ref, wu_ref, wd_ref, gfin_ref, o_ref, h_buf, a_buf):
    tile = h_buf.shape[1]
    n_tiles = x_ref.shape[0] // tile
    rows = [pl.ds(i * tile, tile) for i in range(n_tiles)]

    def norm(i):
        h_buf[i % 2] = _rms_norm(x_ref[rows[i], :], gain_ref[...]).astype(BF16)

    def gate_up(i, chunks):
        for c in chunks:
            g = _dot(h_buf[i % 2], wg_ref[:, _cols(c)])
            u = _dot(h_buf[i % 2], wu_ref[:, _cols(c)])
            a_buf[i % 2, :, _cols(c)] = (g * _sigmoid(g) * u).astype(BF16)

    def down_and_final_norm(i):
        for j in range(N_CH):
            o_ref[rows[i], _cols(j)] = x_ref[rows[i], _cols(j)] + _dot(a_buf[i % 2], wd_ref[:, _cols(j)])
        o_ref[rows[i], :] = _rms_norm(o_ref[rows[i], :], gfin_ref[...])

    n_chunks = D_FF // CH
    norm(0)
    gate_up(0, range(n_chunks))
    for i in range(n_tiles):
        if i + 1 < n_tiles:
            norm(i + 1)
            gate_up(i + 1, range(FFN_EARLY_CHUNKS))
        down_and_final_norm(i)
        if i + 1 < n_tiles:
            gate_up(i + 1, range(FFN_EARLY_CHUNKS, n_chunks))


def _resident(shape, index=None):
    index = (0,) * len(shape) if index is None else index
    return pl.BlockSpec(shape, lambda *_: index, pipeline_mode=pl.Buffered(1))


def _nbytes(shape, dtype):
    n = 1
    for s in shape:
        n *= s
    return n * jnp.dtype(dtype).itemsize


def _vmem_limit(resident, streamed, scratch):
    need = sum(resident) + 2 * sum(streamed) + sum(scratch)
    assert need <= V7X_SCOPED_VMEM_BYTES, need
    return V7X_SCOPED_VMEM_BYTES


def kernel(x, mem, norm_mix, w_in, conv_w, w_conv_out, w_pool, pool_scale, norm_mem, w_kv,
           w_xattn_out, w_out, norm_ffn, w_gate, w_up, w_down, norm_final):
    batch, seq, d = x.shape
    assert d == D_MODEL and mem.shape == (batch, N_MEM, D_MODEL)
    assert w_in.shape == (1, D_MODEL, D_IN) and w_gate.shape == (1, D_MODEL, D_FF)
    tile = TOKEN_TILE
    ffn_rows = FFN_STEP_TILES * FFN_TILE
    assert seq % tile == 0 and (batch * seq) % ffn_rows == 0
    row = lambda a: a.reshape(1, -1).astype(F32)
    sq = (D_MODEL, D_MODEL)

    prep_resident = [_nbytes((D_MODEL, 2 * D_MODEL), F32), 2 * _nbytes(sq, F32)]
    prep_streamed = [_nbytes((N_MEM, D_MODEL), F32), 2 * _nbytes(sq, BF16)]
    prep_scratch = [_nbytes((D_MODEL, 2 * D_MODEL), BF16), 2 * _nbytes(sq, BF16)]
    sm, vo = pl.pallas_call(
        _prep_kernel,
        grid=(batch,),
        in_specs=[
            pl.BlockSpec((1, N_MEM, D_MODEL), lambda b: (b, 0, 0)),
            _resident((1, D_MODEL)),
            _resident((D_MODEL, 2 * D_MODEL)),
            _resident(sq, (0, SRC_Q_X)),
            _resident(sq),
        ],
        out_specs=[
            pl.BlockSpec((1,) + sq, lambda b: (b, 0, 0)),
            pl.BlockSpec((1,) + sq, lambda b: (b, 0, 0)),
        ],
        out_shape=[
            jax.ShapeDtypeStruct((batch,) + sq, BF16),
            jax.ShapeDtypeStruct((batch,) + sq, BF16),
        ],
        scratch_shapes=[
            pltpu.VMEM((D_MODEL, 2 * D_MODEL), BF16),
            pltpu.VMEM(sq, BF16),
            pltpu.VMEM(sq, BF16),
        ],
        compiler_params=pltpu.CompilerParams(
            dimension_semantics=("arbitrary",),
            vmem_limit_bytes=_vmem_limit(prep_resident, prep_streamed, prep_scratch)),
        name="prep",
    )(mem, row(norm_mem[0]), w_kv[0], w_in[0], w_xattn_out[0])

    lead = LEAD_STEPS
    rb = D_MODEL // lead
    n_t = seq // tile
    n_steps = batch * n_t
    gu_rb = D_MODEL // n_steps
    d_steps = n_steps // 2
    d_rb = D_FF // d_steps
    assert rb % BF16_SUBLANES == 0 and gu_rb % BF16_SUBLANES == 0 and d_rb % BF16_SUBLANES == 0
    tok = lambda s: jnp.maximum(s - lead, 0)
    tile_block = lambda s: (tok(s) // n_t, tok(s) % n_t, 0)
    batch_block = lambda s: (tok(s) // n_t, 0, 0)
    lead_block = lambda s: (jnp.minimum(s, lead - 1), 0)
    step_block = lambda s: (tok(s), 0)
    down_block = lambda s: (jnp.minimum(tok(s), d_steps - 1), 0)
    mixer_resident = [_nbytes((N_POOL, CH, CH), F32)]
    mixer_streamed = [
        2 * _nbytes((tile, D_MODEL), F32), 2 * _nbytes(sq, BF16),
        _nbytes((rb, D_IN), F32), 2 * _nbytes((rb, D_MODEL), F32),
        2 * _nbytes((gu_rb, D_FF), F32), _nbytes((d_rb, D_MODEL), F32),
        2 * _nbytes((gu_rb, D_FF), BF16), _nbytes((d_rb, D_MODEL), BF16)]
    mixer_scratch = [
        _nbytes((D_MODEL, D_MIX), BF16), 2 * _nbytes(sq, BF16),
        2 * _nbytes((tile, D_MODEL), BF16), 2 * _nbytes((HALO + tile, D_MODEL), F32)]
    x1, w_gate_b, w_up_b, w_down_b = pl.pallas_call(
        functools.partial(_mixer_kernel, lead=lead, n_t=n_t),
        grid=(lead + n_steps,),
        in_specs=[
            pl.BlockSpec((1, tile, D_MODEL), tile_block),
            pl.BlockSpec((1,) + sq, batch_block),
            pl.BlockSpec((1,) + sq, batch_block),
            _resident((1, D_MODEL)),
            _resident((CONV_K, D_MODEL)),
            pl.BlockSpec((rb, D_IN), lead_block),
            pl.BlockSpec((rb, D_MODEL), lead_block),
            pl.BlockSpec((rb, D_MODEL), lead_block),
            _resident((N_POOL, CH, CH)),
            _resident((1, D_MODEL)),
            pl.BlockSpec((gu_rb, D_FF), step_block),
            pl.BlockSpec((gu_rb, D_FF), step_block),
            pl.BlockSpec((d_rb, D_MODEL), down_block),
        ],
        out_specs=[
            pl.BlockSpec((1, tile, D_MODEL), tile_block),
            pl.BlockSpec((gu_rb, D_FF), step_block),
            pl.BlockSpec((gu_rb, D_FF), step_block),
            pl.BlockSpec((d_rb, D_MODEL), down_block),
        ],
        out_shape=[
            jax.ShapeDtypeStruct((batch, seq, D_MODEL), F32),
            jax.ShapeDtypeStruct((D_MODEL, D_FF), BF16),
            jax.ShapeDtypeStruct((D_MODEL, D_FF), BF16),
            jax.ShapeDtypeStruct((D_FF, D_MODEL), BF16),
        ],
        scratch_shapes=[
            pltpu.VMEM((D_MODEL, D_MIX), BF16),
            pltpu.VMEM(sq, BF16),
            pltpu.VMEM(sq, BF16),
            pltpu.VMEM((tile, D_MODEL), BF16),
            pltpu.VMEM((HALO + tile, D_MODEL), F32),
            pltpu.VMEM((HALO + tile, D_MODEL), F32),
            pltpu.VMEM((tile, D_MODEL), BF16),
        ],
        compiler_params=pltpu.CompilerParams(
            dimension_semantics=("arbitrary",),
            vmem_limit_bytes=_vmem_limit(mixer_resident, mixer_streamed, mixer_scratch)),
        name="mixer",
    )(x, sm, vo, row(norm_mix[0]), conv_w[0].astype(F32), w_in[0], w_conv_out[0], w_out[0],
      w_pool[0], row(pool_scale[0]), w_gate[0], w_up[0], w_down[0])

    n_tok = batch * seq
    ffn_resident = [3 * _nbytes((D_MODEL, D_FF), BF16)]
    ffn_streamed = [2 * _nbytes((ffn_rows, D_MODEL), F32)]
    ffn_scratch = [2 * _nbytes((FFN_TILE, D_MODEL), BF16), 2 * _nbytes((FFN_TILE, D_FF), BF16)]
    out = pl.pallas_call(
        _ffn_kernel,
        grid=(n_tok // ffn_rows,),
        in_specs=[
            pl.BlockSpec((ffn_rows, D_MODEL), lambda i: (i, 0)),
            _resident((1, D_MODEL)),
            _resident((D_MODEL, D_FF)),
            _resident((D_MODEL, D_FF)),
            _resident((D_FF, D_MODEL)),
            _resident((1, D_MODEL)),
        ],
        out_specs=pl.BlockSpec((ffn_rows, D_MODEL), lambda i: (i, 0)),
        out_shape=jax.ShapeDtypeStruct((n_tok, D_MODEL), F32),
        scratch_shapes=[
            pltpu.VMEM((2, FFN_TILE, D_MODEL), BF16),
            pltpu.VMEM((2, FFN_TILE, D_FF), BF16),
        ],
        compiler_params=pltpu.CompilerParams(
            dimension_semantics=("arbitrary",),
            vmem_limit_bytes=_vmem_limit(ffn_resident, ffn_streamed, ffn_scratch)),
        name="ffn",
    )(x1.reshape(n_tok, D_MODEL), row(norm_ffn[0]), w_gate_b, w_up_b, w_down_b, row(norm_final))
    return out.reshape(batch, seq, D_MODEL)
```

```python
import functools

import jax
import jax.numpy as jnp
from jax import lax
from jax.experimental import pallas as pl
from jax.experimental.pallas import tpu as pltpu

F32 = jnp.float32
BF16 = jnp.bfloat16

D_MODEL = 1024
N_MEM = 256
CONV_K = 3
POOL_WINDOWS = (2, 4, 8, 16)
N_POOL = len(POOL_WINDOWS)
X_HEADS = 4
X_HEAD_DIM = D_MODEL // X_HEADS
D_FF = 2816
EPS = 1e-6

SRC_B_A, SRC_C_A, SRC_U_A, SRC_U_P, SRC_Q_X, SRC_G_A, SRC_G_P, SRC_G_X = range(8)
D_IN = 8 * D_MODEL
MIX_SRC = (SRC_B_A, SRC_C_A, SRC_U_A, SRC_U_P, SRC_G_A, SRC_G_P, SRC_G_X)
COL_B_A, COL_C_A, COL_U_A, COL_U_P, COL_G_A, COL_G_P, COL_G_X = (
    i * D_MODEL for i in range(len(MIX_SRC)))
D_MIX = len(MIX_SRC) * D_MODEL

V7X_MXU_WIDTH = 256
V7X_VMEM_BYTES = 64 * 1024 * 1024
V7X_SCOPED_VMEM_BYTES = V7X_VMEM_BYTES - 4 * 1024 * 1024
BF16_SUBLANES = 16
HALO = 16
TOKEN_TILE = 512
FFN_TILE = 256
FFN_STEP_TILES = 4
FFN_EARLY_CHUNKS = 1
CH = V7X_MXU_WIDTH
N_CH = D_MODEL // CH
LEAD_STEPS = 8

assert X_HEAD_DIM == CH and N_MEM == CH and D_MODEL // N_POOL == CH


def _rms_norm(x, gain):
    ms = jnp.mean(x * x, axis=-1, keepdims=True)
    return (x * lax.rsqrt(ms + EPS)) * gain


NEG_LOG2_E = -1.4426950408889634


def _sigmoid(x):
    return 1.0 / (1.0 + jnp.exp2(x * NEG_LOG2_E))


def _dot(a, b):
    return jnp.dot(a, b, preferred_element_type=F32)


def _dot_bt(a, b):
    return lax.dot_general(a, b, (((1,), (1,)), ((), ())), preferred_element_type=F32)


def _cols(j):
    return slice(j * CH, (j + 1) * CH)


def _prep_kernel(mem_ref, gain_ref, wk_ref, wv_ref, wq_ref, wxo_ref, sm_ref, vo_ref, mn_buf):
    batch = mem_ref.shape[0]

    @pl.when(pl.program_id(0) == 0)
    def _():
        for b in range(batch):
            mn_buf[b] = _rms_norm(mem_ref[b], gain_ref[...]).astype(BF16)

    scale = X_HEAD_DIM ** -0.5 * -NEG_LOG2_E
    wk = wk_ref[...].astype(BF16)
    wv = wv_ref[...].astype(BF16)
    wq = wq_ref[...].astype(BF16)
    wxo = wxo_ref[...].astype(BF16)
    k = [(_dot(mn_buf[b], wk) * scale).astype(BF16) for b in range(batch)]
    v = [_dot(mn_buf[b], wv).astype(BF16) for b in range(batch)]
    for b in range(batch):
        sm_ref[b] = _dot_bt(wq, k[b]).astype(BF16)
    for b in range(batch):
        vo_ref[b] = _dot(v[b], wxo).astype(BF16)


def _convert_mixer_weights(i, win_ref, wco_ref, wout_ref, wpool_ref, pscale_ref, win_b, wco_b, wout_b):
    rb = win_ref.shape[0]
    rows = pl.ds(pl.multiple_of(i * rb, rb), rb)
    for dst, src in enumerate(MIX_SRC):
        if src != SRC_U_P:
            win_b[rows, dst * D_MODEL:(dst + 1) * D_MODEL] = (
                win_ref[:, src * D_MODEL:(src + 1) * D_MODEL].astype(BF16))
            continue
        for g in range(N_POOL):
            w_up = win_ref[:, src * D_MODEL + g * CH:src * D_MODEL + (g + 1) * CH].astype(BF16)
            folded = _dot(w_up, wpool_ref[g].astype(BF16)) * pscale_ref[:, _cols(g)]
            win_b[rows, dst * D_MODEL + g * CH:dst * D_MODEL + (g + 1) * CH] = folded.astype(BF16)
    wco_b[rows, :] = wco_ref[...].astype(BF16)
    wout_b[rows, :] = wout_ref[...].astype(BF16)


def _mixer_kernel(x_ref, sm_ref, vo_ref, gain_ref, convw_ref,
                  win_f32, wco_f32, wout_f32, wpool_ref, pscale_ref,
                  wg_ref, wu_ref, wd_ref,
                  o_ref, wg_o, wu_o, wd_o,
                  win_ref, wco_ref, wout_ref, h_buf, cu_buf, up_buf, z_buf, *, lead, n_t):
    s = pl.program_id(0)

    @pl.when(s < lead)
    def _():
        _convert_mixer_weights(s, win_f32, wco_f32, wout_f32, wpool_ref, pscale_ref,
                               win_ref, wco_ref, wout_ref)

    @pl.when(s >= lead)
    def _():
        _mixer_tile(lax.rem(s - lead, n_t), x_ref.at[0], o_ref.at[0], sm_ref, vo_ref, gain_ref,
                    convw_ref, win_ref, wco_ref, wout_ref, wg_ref, wu_ref, wd_ref,
                    wg_o, wu_o, wd_o, h_buf, cu_buf, up_buf, z_buf)


def _mixer_tile(t, x_ref, o_ref, sm_ref, vo_ref, gain_ref, convw_ref, win_ref, wco_ref, wout_ref,
                wg_ref, wu_ref, wd_ref, wg_o, wu_o, wd_o, h_buf, cu_buf, up_buf, z_buf):
    tile = x_ref.shape[0]

    @pl.when(t == 0)
    def _():
        cu_buf[0:HALO, :] = jnp.zeros((HALO, D_MODEL), F32)
        up_buf[0:HALO, :] = jnp.zeros((HALO, D_MODEL), F32)

    h_buf[...] = _rms_norm(x_ref[...], gain_ref[...]).astype(BF16)

    def proj(col0, j):
        return _dot(h_buf[...], win_ref[:, col0 + j * CH:col0 + (j + 1) * CH])

    for j in range(N_CH):
        cu_buf[HALO:, _cols(j)] = proj(COL_C_A, j) * proj(COL_U_A, j)
    for j in range(N_CH):
        cw = convw_ref[:, _cols(j)]
        conv = cw[CONV_K - 1:CONV_K] * cu_buf[HALO:, _cols(j)]
        for k in range(CONV_K - 1):
            lag = CONV_K - 1 - k
            conv = conv + cw[k:k + 1] * cu_buf[HALO - lag:HALO - lag + tile, _cols(j)]
        z_buf[:, _cols(j)] = (proj(COL_B_A, j) * conv).astype(BF16)
    cu_buf[0:HALO, :] = cu_buf[tile:tile + HALO, :]
    for j in range(N_CH):
        y = _dot(z_buf[...], wco_ref[:, _cols(j)])
        o_ref[:, _cols(j)] = _sigmoid(proj(COL_G_A, j)) * y

    for g in range(N_POOL):
        up_buf[HALO:, _cols(g)] = proj(COL_U_P, g)
    pos = t * tile + 1 + lax.broadcasted_iota(jnp.int32, (tile, 1), 0)
    scores = []
    for g, w in enumerate(POOL_WINDOWS):
        e = up_buf[:, _cols(g)]
        s = e
        k = 1
        while k < w:
            s = s + pltpu.roll(s, k, 0)
            k *= 2
        inv_cnt = 1.0 / jnp.minimum(pos, w).astype(F32)
        y = s[HALO:] * inv_cnt - e[HALO:]
        o_ref[:, _cols(g)] += _sigmoid(proj(COL_G_P, g)) * y
        scores.append(_dot(h_buf[...], sm_ref[0, :, _cols(g)]))
    up_buf[0:HALO, :] = up_buf[tile:tile + HALO, :]

    for h in range(X_HEADS):
        s = scores[h]
        p = jnp.exp2(s - jnp.max(s, axis=-1, keepdims=True))
        inv_l = 1.0 / jnp.sum(p, axis=-1, keepdims=True)
        z_buf[:, _cols(h)] = (p * inv_l).astype(BF16)
    for j in range(N_CH):
        y = _dot(z_buf[...], vo_ref[0, :, _cols(j)])
        o_ref[:, _cols(j)] += _sigmoid(proj(COL_G_X, j)) * y

    h_buf[...] = o_ref[...].astype(BF16)
    for j in range(N_CH):
        o_ref[:, _cols(j)] = x_ref[:, _cols(j)] + _dot(h_buf[...], wout_ref[:, _cols(j)])

    wg_o[...] = wg_ref[...].astype(BF16)
    wu_o[...] = wu_ref[...].astype(BF16)
    wd_o[...] = wd_ref[...].astype(BF16)


def _ffn_kernel(x_ref, gain_ref, wg_ref, wu_ref, wd_ref, gfin_ref, o_ref, h_buf, a_buf):
    tile = h_buf.shape[1]
    n_tiles = x_ref.shape[0] // tile
    rows = [pl.ds(i * tile, tile) for i in range(n_tiles)]

    def norm(i):
        h_buf[i % 2] = _rms_norm(x_ref[rows[i], :], gain_ref[...]).astype(BF16)

    def gate_up(i, chunks):
        for c in chunks:
            g = _dot(h_buf[i % 2], wg_ref[:, _cols(c)])
            u = _dot(h_buf[i % 2], wu_ref[:, _cols(c)])
            a_buf[i % 2, :, _cols(c)] = (g * _sigmoid(g) * u).astype(BF16)

    def down_and_final_norm(i):
        for j in range(N_CH):
            o_ref[rows[i], _cols(j)] = x_ref[rows[i], _cols(j)] + _dot(a_buf[i % 2], wd_ref[:, _cols(j)])
        o_ref[rows[i], :] = _rms_norm(o_ref[rows[i], :], gfin_ref[...])

    n_chunks = D_FF // CH
    norm(0)
    gate_up(0, range(n_chunks))
    for i in range(n_tiles):
        if i + 1 < n_tiles:
            norm(i + 1)
            gate_up(i + 1, range(FFN_EARLY_CHUNKS))
        down_and_final_norm(i)
        if i + 1 < n_tiles:
            gate_up(i + 1, range(FFN_EARLY_CHUNKS, n_chunks))


def _resident(shape, index=None):
    index = (0,) * len(shape) if index is None else index
    return pl.BlockSpec(shape, lambda *_: index, pipeline_mode=pl.Buffered(1))


def _nbytes(shape, dtype):
    n = 1
    for s in shape:
        n *= s
    return n * jnp.dtype(dtype).itemsize


def _vmem_limit(resident, streamed, scratch):
    need = sum(resident) + 2 * sum(streamed) + sum(scratch)
    assert need <= V7X_SCOPED_VMEM_BYTES, need
    return V7X_SCOPED_VMEM_BYTES


def kernel(x, mem, norm_mix, w_in, conv_w, w_conv_out, w_pool, pool_scale, norm_mem, w_kv,
           w_xattn_out, w_out, norm_ffn, w_gate, w_up, w_down, norm_final):
    batch, seq, d = x.shape
    assert d == D_MODEL and mem.shape == (batch, N_MEM, D_MODEL)
    assert w_in.shape == (1, D_MODEL, D_IN) and w_gate.shape == (1, D_MODEL, D_FF)
    tile = TOKEN_TILE
    ffn_rows = FFN_STEP_TILES * FFN_TILE
    assert seq % tile == 0 and (batch * seq) % ffn_rows == 0
    row = lambda a: a.reshape(1, -1).astype(F32)
    sq = (D_MODEL, D_MODEL)

    head_cols = (D_MODEL, X_HEAD_DIM)
    prep_resident = [_nbytes((batch, N_MEM, D_MODEL), F32)]
    prep_streamed = [4 * _nbytes(head_cols, F32), 2 * _nbytes((batch,) + head_cols, BF16)]
    prep_scratch = [_nbytes((batch, N_MEM, D_MODEL), BF16)]
    sm, vo = pl.pallas_call(
        _prep_kernel,
        grid=(X_HEADS,),
        in_specs=[
            _resident((batch, N_MEM, D_MODEL)),
            _resident((1, D_MODEL)),
            pl.BlockSpec(head_cols, lambda h: (0, h)),
            pl.BlockSpec(head_cols, lambda h: (0, X_HEADS + h)),
            pl.BlockSpec(head_cols, lambda h: (0, SRC_Q_X * X_HEADS + h)),
            pl.BlockSpec((X_HEAD_DIM, D_MODEL), lambda h: (h, 0)),
        ],
        out_specs=[
            pl.BlockSpec((batch, D_MODEL, N_MEM), lambda h: (0, 0, h)),
            pl.BlockSpec((batch, N_MEM, D_MODEL), lambda h: (0, h, 0)),
        ],
        out_shape=[
            jax.ShapeDtypeStruct((batch,) + sq, BF16),
            jax.ShapeDtypeStruct((batch,) + sq, BF16),
        ],
        scratch_shapes=[pltpu.VMEM((batch, N_MEM, D_MODEL), BF16)],
        compiler_params=pltpu.CompilerParams(
            dimension_semantics=("arbitrary",),
            vmem_limit_bytes=_vmem_limit(prep_resident, prep_streamed, prep_scratch)),
        name="prep",
    )(mem, row(norm_mem[0]), w_kv[0], w_kv[0], w_in[0], w_xattn_out[0])

    lead = LEAD_STEPS
    rb = D_MODEL // lead
    n_t = seq // tile
    n_steps = batch * n_t
    gu_rb = D_MODEL // n_steps
    d_steps = n_steps // 2
    d_rb = D_FF // d_steps
    assert rb % BF16_SUBLANES == 0 and gu_rb % BF16_SUBLANES == 0 and d_rb % BF16_SUBLANES == 0
    tok = lambda s: jnp.maximum(s - lead, 0)
    tile_block = lambda s: (tok(s) // n_t, tok(s) % n_t, 0)
    batch_block = lambda s: (tok(s) // n_t, 0, 0)
    lead_block = lambda s: (jnp.minimum(s, lead - 1), 0)
    step_block = lambda s: (tok(s), 0)
    down_block = lambda s: (jnp.minimum(tok(s), d_steps - 1), 0)
    mixer_resident = [_nbytes((N_POOL, CH, CH), F32)]
    mixer_streamed = [
        2 * _nbytes((tile, D_MODEL), F32), 2 * _nbytes(sq, BF16),
        _nbytes((rb, D_IN), F32), 2 * _nbytes((rb, D_MODEL), F32),
        2 * _nbytes((gu_rb, D_FF), F32), _nbytes((d_rb, D_MODEL), F32),
        2 * _nbytes((gu_rb, D_FF), BF16), _nbytes((d_rb, D_MODEL), BF16)]
    mixer_scratch = [
        _nbytes((D_MODEL, D_MIX), BF16), 2 * _nbytes(sq, BF16),
        2 * _nbytes((tile, D_MODEL), BF16), 2 * _nbytes((HALO + tile, D_MODEL), F32)]
    x1, w_gate_b, w_up_b, w_down_b = pl.pallas_call(
        functools.partial(_mixer_kernel, lead=lead, n_t=n_t),
        grid=(lead + n_steps,),
        in_specs=[
            pl.BlockSpec((1, tile, D_MODEL), tile_block),
            pl.BlockSpec((1,) + sq, batch_block),
            pl.BlockSpec((1,) + sq, batch_block),
            _resident((1, D_MODEL)),
            _resident((CONV_K, D_MODEL)),
            pl.BlockSpec((rb, D_IN), lead_block),
            pl.BlockSpec((rb, D_MODEL), lead_block),
            pl.BlockSpec((rb, D_MODEL), lead_block),
            _resident((N_POOL, CH, CH)),
            _resident((1, D_MODEL)),
            pl.BlockSpec((gu_rb, D_FF), step_block),
            pl.BlockSpec((gu_rb, D_FF), step_block),
            pl.BlockSpec((d_rb, D_MODEL), down_block),
        ],
        out_specs=[
            pl.BlockSpec((1, tile, D_MODEL), tile_block),
            pl.BlockSpec((gu_rb, D_FF), step_block),
            pl.BlockSpec((gu_rb, D_FF), step_block),
            pl.BlockSpec((d_rb, D_MODEL), down_block),
        ],
        out_shape=[
            jax.ShapeDtypeStruct((batch, seq, D_MODEL), F32),
            jax.ShapeDtypeStruct((D_MODEL, D_FF), BF16),
            jax.ShapeDtypeStruct((D_MODEL, D_FF), BF16),
            jax.ShapeDtypeStruct((D_FF, D_MODEL), BF16),
        ],
        scratch_shapes=[
            pltpu.VMEM((D_MODEL, D_MIX), BF16),
            pltpu.VMEM(sq, BF16),
            pltpu.VMEM(sq, BF16),
            pltpu.VMEM((tile, D_MODEL), BF16),
            pltpu.VMEM((HALO + tile, D_MODEL), F32),
            pltpu.VMEM((HALO + tile, D_MODEL), F32),
            pltpu.VMEM((tile, D_MODEL), BF16),
        ],
        compiler_params=pltpu.CompilerParams(
            dimension_semantics=("arbitrary",),
            vmem_limit_bytes=_vmem_limit(mixer_resident, mixer_streamed, mixer_scratch)),
        name="mixer",
    )(x, sm, vo, row(norm_mix[0]), conv_w[0].astype(F32), w_in[0], w_conv_out[0], w_out[0],
      w_pool[0], row(pool_scale[0]), w_gate[0], w_up[0], w_down[0])

    n_tok = batch * seq
    ffn_resident = [3 * _nbytes((D_MODEL, D_FF), BF16)]
    ffn_streamed = [2 * _nbytes((ffn_rows, D_MODEL), F32)]
    ffn_scratch = [2 * _nbytes((FFN_TILE, D_MODEL), BF16), 2 * _nbytes((FFN_TILE, D_FF), BF16)]
    out = pl.pallas_call(
        _ffn_kernel,
        grid=(n_tok // ffn_rows,),
        in_specs=[
            pl.BlockSpec((ffn_rows, D_MODEL), lambda i: (i, 0)),
            _resident((1, D_MODEL)),
            _resident((D_MODEL, D_FF)),
            _resident((D_MODEL, D_FF)),
            _resident((D_FF, D_MODEL)),
            _resident((1, D_MODEL)),
        ],
        out_specs=pl.BlockSpec((ffn_rows, D_MODEL), lambda i: (i, 0)),
        out_shape=jax.ShapeDtypeStruct((n_tok, D_MODEL), F32),
        scratch_shapes=[
            pltpu.VMEM((2, FFN_TILE, D_MODEL), BF16),
            pltpu.VMEM((2, FFN_TILE, D_FF), BF16),
        ],
        compiler_params=pltpu.CompilerParams(
            dimension_semantics=("arbitrary",),
            vmem_limit_bytes=_vmem_limit(ffn_resident, ffn_streamed, ffn_scratch)),
        name="ffn",
    )(x1.reshape(n_tok, D_MODEL), row(norm_ffn[0]), w_gate_b, w_up_b, w_down_b, row(norm_final))
    return out.reshape(batch, seq, D_MODEL)
```

```python
import functools

import jax
import jax.numpy as jnp
from jax import lax
from jax.experimental import pallas as pl
from jax.experimental.pallas import tpu as pltpu

F32 = jnp.float32
BF16 = jnp.bfloat16

D_MODEL = 1024
N_MEM = 256
CONV_K = 3
POOL_WINDOWS = (2, 4, 8, 16)
N_POOL = len(POOL_WINDOWS)
X_HEADS = 4
X_HEAD_DIM = D_MODEL // X_HEADS
D_FF = 2816
EPS = 1e-6
NEG_LOG2_E = -1.4426950408889634

SRC_B_A, SRC_C_A, SRC_U_A, SRC_U_P, SRC_Q_X, SRC_G_A, SRC_G_P, SRC_G_X = range(8)
D_IN = 8 * D_MODEL
MIX_SRC = (SRC_B_A, SRC_C_A, SRC_U_A, SRC_U_P, SRC_G_A, SRC_G_P, SRC_G_X)
COL_B_A, COL_C_A, COL_U_A, COL_U_P, COL_G_A, COL_G_P, COL_G_X = (
    i * D_MODEL for i in range(len(MIX_SRC)))
D_MIX = len(MIX_SRC) * D_MODEL

V7X_MXU_WIDTH = 256
V7X_VMEM_BYTES = 64 * 1024 * 1024
V7X_SCOPED_VMEM_BYTES = V7X_VMEM_BYTES - 4 * 1024 * 1024
BF16_SUBLANES = 16
HALO = 16
TOKEN_TILE = 512
FFN_TILE = 256
FFN_STEP_TILES = 4
FFN_EARLY_CHUNKS = 1
CH = V7X_MXU_WIDTH
N_CH = D_MODEL // CH
LEAD_STEPS = 8

assert X_HEAD_DIM == CH and N_MEM == CH and D_MODEL // N_POOL == CH


def _rms_norm(x, gain):
    ms = jnp.mean(x * x, axis=-1, keepdims=True)
    return (x * lax.rsqrt(ms + EPS)) * gain


def _sigmoid(x):
    return 1.0 / (1.0 + jnp.exp2(x * NEG_LOG2_E))


def _dot(a, b):
    return jnp.dot(a, b, preferred_element_type=F32)


def _dot_bt(a, b):
    return lax.dot_general(a, b, (((1,), (1,)), ((), ())), preferred_element_type=F32)


def _cols(j):
    return slice(j * CH, (j + 1) * CH)


def _prep_kernel(mem_ref, gain_ref, wkv_ref, wq_ref, wxo_ref, sm_ref, vo_ref,
                 wkv_buf, wq_buf, wxo_buf):
    @pl.when(pl.program_id(0) == 0)
    def _():
        wkv_buf[...] = wkv_ref[...].astype(BF16)
        wq_buf[...] = wq_ref[...].astype(BF16)
        wxo_buf[...] = wxo_ref[...].astype(BF16)

    mn = _rms_norm(mem_ref[0], gain_ref[...]).astype(BF16)
    scale = X_HEAD_DIM ** -0.5 * -NEG_LOG2_E
    heads = range(X_HEADS)
    k = [(_dot(mn, wkv_buf[:, _cols(h)]) * scale).astype(BF16) for h in heads]
    v = [_dot(mn, wkv_buf[:, D_MODEL + h * CH:D_MODEL + (h + 1) * CH]).astype(BF16) for h in heads]
    for h in heads:
        sm_ref[0, :, _cols(h)] = _dot_bt(wq_buf[:, _cols(h)], k[h]).astype(BF16)
    for h in heads:
        vo_ref[0, _cols(h), :] = _dot(v[h], wxo_buf[_cols(h), :]).astype(BF16)


def _convert_mixer_weights(i, win_ref, wco_ref, wout_ref, wpool_ref, pscale_ref, win_b, wco_b, wout_b):
    rb = win_ref.shape[0]
    rows = pl.ds(pl.multiple_of(i * rb, rb), rb)
    for dst, src in enumerate(MIX_SRC):
        if src != SRC_U_P:
            win_b[rows, dst * D_MODEL:(dst + 1) * D_MODEL] = (
                win_ref[:, src * D_MODEL:(src + 1) * D_MODEL].astype(BF16))
            continue
        for g in range(N_POOL):
            w_up = win_ref[:, src * D_MODEL + g * CH:src * D_MODEL + (g + 1) * CH].astype(BF16)
            folded = _dot(w_up, wpool_ref[g].astype(BF16)) * pscale_ref[:, _cols(g)]
            win_b[rows, dst * D_MODEL + g * CH:dst * D_MODEL + (g + 1) * CH] = folded.astype(BF16)
    wco_b[rows, :] = wco_ref[...].astype(BF16)
    wout_b[rows, :] = wout_ref[...].astype(BF16)


def _mixer_kernel(x_ref, sm_ref, vo_ref, gain_ref, convw_ref,
                  win_f32, wco_f32, wout_f32, wpool_ref, pscale_ref,
                  wg_ref, wu_ref, wd_ref,
                  o_ref, wg_o, wu_o, wd_o,
                  win_ref, wco_ref, wout_ref, h_buf, cu_buf, up_buf, z_buf, *, lead, n_t):
    s = pl.program_id(0)

    @pl.when(s < lead)
    def _():
        _convert_mixer_weights(s, win_f32, wco_f32, wout_f32, wpool_ref, pscale_ref,
                               win_ref, wco_ref, wout_ref)

    @pl.when(s >= lead)
    def _():
        _mixer_tile(lax.rem(s - lead, jnp.int32(n_t)), x_ref.at[0], o_ref.at[0], sm_ref, vo_ref, gain_ref,
                    convw_ref, win_ref, wco_ref, wout_ref, wg_ref, wu_ref, wd_ref,
                    wg_o, wu_o, wd_o, h_buf, cu_buf, up_buf, z_buf)


def _mixer_tile(t, x_ref, o_ref, sm_ref, vo_ref, gain_ref, convw_ref, win_ref, wco_ref, wout_ref,
                wg_ref, wu_ref, wd_ref, wg_o, wu_o, wd_o, h_buf, cu_buf, up_buf, z_buf):
    tile = x_ref.shape[0]

    @pl.when(t == 0)
    def _():
        cu_buf[0:HALO, :] = jnp.zeros((HALO, D_MODEL), F32)
        up_buf[0:HALO, :] = jnp.zeros((HALO, D_MODEL), F32)

    h_buf[...] = _rms_norm(x_ref[...], gain_ref[...]).astype(BF16)

    def proj(col0, j):
        return _dot(h_buf[...], win_ref[:, col0 + j * CH:col0 + (j + 1) * CH])

    for j in range(N_CH):
        cu_buf[HALO:, _cols(j)] = proj(COL_C_A, j) * proj(COL_U_A, j)
    for j in range(N_CH):
        cw = convw_ref[:, _cols(j)]
        conv = cw[CONV_K - 1:CONV_K] * cu_buf[HALO:, _cols(j)]
        for k in range(CONV_K - 1):
            lag = CONV_K - 1 - k
            conv = conv + cw[k:k + 1] * cu_buf[HALO - lag:HALO - lag + tile, _cols(j)]
        z_buf[:, _cols(j)] = (proj(COL_B_A, j) * conv).astype(BF16)
    cu_buf[0:HALO, :] = cu_buf[tile:tile + HALO, :]
    for j in range(N_CH):
        y = _dot(z_buf[...], wco_ref[:, _cols(j)])
        o_ref[:, _cols(j)] = _sigmoid(proj(COL_G_A, j)) * y

    for g in range(N_POOL):
        up_buf[HALO:, _cols(g)] = proj(COL_U_P, g)
    pos = t * tile + 1 + lax.broadcasted_iota(jnp.int32, (tile, 1), 0)
    scores = []
    for g, w in enumerate(POOL_WINDOWS):
        e = up_buf[:, _cols(g)]
        s = e
        k = 1
        while k < w:
            s = s + pltpu.roll(s, k, 0)
            k *= 2
        inv_cnt = 1.0 / jnp.minimum(pos, w).astype(F32)
        y = s[HALO:] * inv_cnt - e[HALO:]
        o_ref[:, _cols(g)] += _sigmoid(proj(COL_G_P, g)) * y
        scores.append(_dot(h_buf[...], sm_ref[0, :, _cols(g)]))
    up_buf[0:HALO, :] = up_buf[tile:tile + HALO, :]

    for h in range(X_HEADS):
        s = scores[h]
        p = jnp.exp2(s - jnp.max(s, axis=-1, keepdims=True))
        inv_l = 1.0 / jnp.sum(p, axis=-1, keepdims=True)
        z_buf[:, _cols(h)] = (p * inv_l).astype(BF16)
    for j in range(N_CH):
        y = _dot(z_buf[...], vo_ref[0, :, _cols(j)])
        o_ref[:, _cols(j)] += _sigmoid(proj(COL_G_X, j)) * y

    h_buf[...] = o_ref[...].astype(BF16)
    for j in range(N_CH):
        o_ref[:, _cols(j)] = x_ref[:, _cols(j)] + _dot(h_buf[...], wout_ref[:, _cols(j)])

    wg_o[...] = wg_ref[...].astype(BF16)
    wu_o[...] = wu_ref[...].astype(BF16)
    wd_o[...] = wd_ref[...].astype(BF16)


def _ffn_kernel(x_ref, gain_ref, wg_ref, wu_ref, wd_ref, gfin_ref, o_ref, h_buf, a_buf):
    tile = h_buf.shape[1]
    n_tiles = x_ref.shape[0] // tile
    rows = [pl.ds(i * tile, tile) for i in range(n_tiles)]

    def norm(i):
        h_buf[i % 2] = _rms_norm(x_ref[rows[i], :], gain_ref[...]).astype(BF16)

    def gate_up(i, chunks):
        for c in chunks:
            g = _dot(h_buf[i % 2], wg_ref[:, _cols(c)])
            u = _dot(h_buf[i % 2], wu_ref[:, _cols(c)])
            a_buf[i % 2, :, _cols(c)] = (g * _sigmoid(g) * u).astype(BF16)

    def down_and_final_norm(i):
        for j in range(N_CH):
            o_ref[rows[i], _cols(j)] = x_ref[rows[i], _cols(j)] + _dot(a_buf[i % 2], wd_ref[:, _cols(j)])
        o_ref[rows[i], :] = _rms_norm(o_ref[rows[i], :], gfin_ref[...])

    n_chunks = D_FF // CH
    norm(0)
    gate_up(0, range(n_chunks))
    for i in range(n_tiles):
        if i + 1 < n_tiles:
            norm(i + 1)
            gate_up(i + 1, range(FFN_EARLY_CHUNKS))
        down_and_final_norm(i)
        if i + 1 < n_tiles:
            gate_up(i + 1, range(FFN_EARLY_CHUNKS, n_chunks))


def _resident(shape, index=None):
    index = (0,) * len(shape) if index is None else index
    return pl.BlockSpec(shape, lambda *_: index, pipeline_mode=pl.Buffered(1))


def _nbytes(shape, dtype):
    n = 1
    for s in shape:
        n *= s
    return n * jnp.dtype(dtype).itemsize


def _vmem_limit(resident, streamed, scratch):
    need = sum(resident) + 2 * sum(streamed) + sum(scratch)
    assert need <= V7X_SCOPED_VMEM_BYTES, need
    return V7X_SCOPED_VMEM_BYTES


def kernel(x, mem, norm_mix, w_in, conv_w, w_conv_out, w_pool, pool_scale, norm_mem, w_kv,
           w_xattn_out, w_out, norm_ffn, w_gate, w_up, w_down, norm_final):
    batch, seq, d = x.shape
    assert d == D_MODEL and mem.shape == (batch, N_MEM, D_MODEL)
    assert w_in.shape == (1, D_MODEL, D_IN) and w_gate.shape == (1, D_MODEL, D_FF)
    tile = TOKEN_TILE
    ffn_rows = FFN_STEP_TILES * FFN_TILE
    assert seq % tile == 0 and (batch * seq) % ffn_rows == 0
    row = lambda a: a.reshape(1, -1).astype(F32)
    sq = (D_MODEL, D_MODEL)

    prep_resident = [_nbytes((D_MODEL, 2 * D_MODEL), F32), 2 * _nbytes(sq, F32)]
    prep_streamed = [_nbytes((N_MEM, D_MODEL), F32), 2 * _nbytes(sq, BF16)]
    prep_scratch = [_nbytes((D_MODEL, 2 * D_MODEL), BF16), 2 * _nbytes(sq, BF16)]
    sm, vo = pl.pallas_call(
        _prep_kernel,
        grid=(batch,),
        in_specs=[
            pl.BlockSpec((1, N_MEM, D_MODEL), lambda b: (b, 0, 0)),
            _resident((1, D_MODEL)),
            _resident((D_MODEL, 2 * D_MODEL)),
            _resident(sq, (0, SRC_Q_X)),
            _resident(sq),
        ],
        out_specs=[
            pl.BlockSpec((1,) + sq, lambda b: (b, 0, 0)),
            pl.BlockSpec((1,) + sq, lambda b: (b, 0, 0)),
        ],
        out_shape=[
            jax.ShapeDtypeStruct((batch,) + sq, BF16),
            jax.ShapeDtypeStruct((batch,) + sq, BF16),
        ],
        scratch_shapes=[
            pltpu.VMEM((D_MODEL, 2 * D_MODEL), BF16),
            pltpu.VMEM(sq, BF16),
            pltpu.VMEM(sq, BF16),
        ],
        compiler_params=pltpu.CompilerParams(
            dimension_semantics=("arbitrary",),
            vmem_limit_bytes=_vmem_limit(prep_resident, prep_streamed, prep_scratch)),
        name="prep",
    )(mem, row(norm_mem[0]), w_kv[0], w_in[0], w_xattn_out[0])

    lead = LEAD_STEPS
    rb = D_MODEL // lead
    n_t = seq // tile
    n_steps = batch * n_t
    gu_rb = D_MODEL // n_steps
    d_steps = n_steps // 2
    d_rb = D_FF // d_steps
    assert rb % BF16_SUBLANES == 0 and gu_rb % BF16_SUBLANES == 0 and d_rb % BF16_SUBLANES == 0
    tok = lambda s: jnp.maximum(s - lead, 0)
    tile_block = lambda s: (tok(s) // n_t, tok(s) % n_t, 0)
    batch_block = lambda s: (tok(s) // n_t, 0, 0)
    lead_block = lambda s: (jnp.minimum(s, lead - 1), 0)
    step_block = lambda s: (tok(s), 0)
    down_block = lambda s: (jnp.minimum(tok(s), d_steps - 1), 0)
    mixer_resident = [_nbytes((N_POOL, CH, CH), F32)]
    mixer_streamed = [
        2 * _nbytes((tile, D_MODEL), F32), 2 * _nbytes(sq, BF16),
        _nbytes((rb, D_IN), F32), 2 * _nbytes((rb, D_MODEL), F32),
        2 * _nbytes((gu_rb, D_FF), F32), _nbytes((d_rb, D_MODEL), F32),
        2 * _nbytes((gu_rb, D_FF), BF16), _nbytes((d_rb, D_MODEL), BF16)]
    mixer_scratch = [
        _nbytes((D_MODEL, D_MIX), BF16), 2 * _nbytes(sq, BF16),
        2 * _nbytes((tile, D_MODEL), BF16), 2 * _nbytes((HALO + tile, D_MODEL), F32)]
    x1, w_gate_b, w_up_b, w_down_b = pl.pallas_call(
        functools.partial(_mixer_kernel, lead=lead, n_t=n_t),
        grid=(lead + n_steps,),
        in_specs=[
            pl.BlockSpec((1, tile, D_MODEL), tile_block),
            pl.BlockSpec((1,) + sq, batch_block),
            pl.BlockSpec((1,) + sq, batch_block),
            _resident((1, D_MODEL)),
            _resident((CONV_K, D_MODEL)),
            pl.BlockSpec((rb, D_IN), lead_block),
            pl.BlockSpec((rb, D_MODEL), lead_block),
            pl.BlockSpec((rb, D_MODEL), lead_block),
            _resident((N_POOL, CH, CH)),
            _resident((1, D_MODEL)),
            pl.BlockSpec((gu_rb, D_FF), step_block),
            pl.BlockSpec((gu_rb, D_FF), step_block),
            pl.BlockSpec((d_rb, D_MODEL), down_block),
        ],
        out_specs=[
            pl.BlockSpec((1, tile, D_MODEL), tile_block),
            pl.BlockSpec((gu_rb, D_FF), step_block),
            pl.BlockSpec((gu_rb, D_FF), step_block),
            pl.BlockSpec((d_rb, D_MODEL), down_block),
        ],
        out_shape=[
            jax.ShapeDtypeStruct((batch, seq, D_MODEL), F32),
            jax.ShapeDtypeStruct((D_MODEL, D_FF), BF16),
            jax.ShapeDtypeStruct((D_MODEL, D_FF), BF16),
            jax.ShapeDtypeStruct((D_FF, D_MODEL), BF16),
        ],
        scratch_shapes=[
            pltpu.VMEM((D_MODEL, D_MIX), BF16),
            pltpu.VMEM(sq, BF16),
            pltpu.VMEM(sq, BF16),
            pltpu.VMEM((tile, D_MODEL), BF16),
            pltpu.VMEM((HALO + tile, D_MODEL), F32),
            pltpu.VMEM((HALO + tile, D_MODEL), F32),
            pltpu.VMEM((tile, D_MODEL), BF16),
        ],
        compiler_params=pltpu.CompilerParams(
            dimension_semantics=("arbitrary",),
            vmem_limit_bytes=_vmem_limit(mixer_resident, mixer_streamed, mixer_scratch)),
        name="mixer",
    )(x, sm, vo, row(norm_mix[0]), conv_w[0].astype(F32), w_in[0], w_conv_out[0], w_out[0],
      w_pool[0], row(pool_scale[0]), w_gate[0], w_up[0], w_down[0])

    n_tok = batch * seq
    ffn_resident = [3 * _nbytes((D_MODEL, D_FF), BF16)]
    ffn_streamed = [2 * _nbytes((ffn_rows, D_MODEL), F32)]
    ffn_scratch = [2 * _nbytes((FFN_TILE, D_MODEL), BF16), 2 * _nbytes((FFN_TILE, D_FF), BF16)]
    out = pl.pallas_call(
        _ffn_kernel,
        grid=(n_tok // ffn_rows,),
        in_specs=[
            pl.BlockSpec((ffn_rows, D_MODEL), lambda i: (i, 0)),
            _resident((1, D_MODEL)),
            _resident((D_MODEL, D_FF)),
            _resident((D_MODEL, D_FF)),
            _resident((D_FF, D_MODEL)),
            _resident((1, D_MODEL)),
        ],
        out_specs=pl.BlockSpec((ffn_rows, D_MODEL), lambda i: (i, 0)),
        out_shape=jax.ShapeDtypeStruct((n_tok, D_MODEL), F32),
        scratch_shapes=[
            pltpu.VMEM((2, FFN_TILE, D_MODEL), BF16),
            pltpu.VMEM((2, FFN_TILE, D_FF), BF16),
        ],
        compiler_params=pltpu.CompilerParams(
            dimension_semantics=("arbitrary",),
            vmem_limit_bytes=_vmem_limit(ffn_resident, ffn_streamed, ffn_scratch)),
        name="ffn",
    )(x1.reshape(n_tok, D_MODEL), row(norm_ffn[0]), w_gate_b, w_up_b, w_down_b, row(norm_final))
    return out.reshape(batch, seq, D_MODEL)
```

```python
import functools

import jax
import jax.numpy as jnp
from jax import lax
from jax.experimental import pallas as pl
from jax.experimental.pallas import tpu as pltpu

F32 = jnp.float32
BF16 = jnp.bfloat16

D_MODEL = 1024
N_MEM = 256
CONV_K = 3
POOL_WINDOWS = (2, 4, 8, 16)
N_POOL = len(POOL_WINDOWS)
X_HEADS = 4
X_HEAD_DIM = D_MODEL // X_HEADS
D_FF = 2816
EPS = 1e-6
NEG_LOG2_E = -1.4426950408889634

SRC_B_A, SRC_C_A, SRC_U_A, SRC_U_P, SRC_Q_X, SRC_G_A, SRC_G_P, SRC_G_X = range(8)
D_IN = 8 * D_MODEL
MIX_SRC = (SRC_B_A, SRC_C_A, SRC_U_A, SRC_U_P, SRC_G_A, SRC_G_P, SRC_G_X)
COL_B_A, COL_C_A, COL_U_A, COL_U_P, COL_G_A, COL_G_P, COL_G_X = (
    i * D_MODEL for i in range(len(MIX_SRC)))
D_MIX = len(MIX_SRC) * D_MODEL

V7X_MXU_WIDTH = 256
V7X_VMEM_BYTES = 64 * 1024 * 1024
V7X_SCOPED_VMEM_BYTES = V7X_VMEM_BYTES - 4 * 1024 * 1024
LANES = 128
BF16_SUBLANES = 16
HALO = 16
TOKEN_TILE = 512
FFN_TILE = 256
FFN_STEP_TILES = 4
FFN_EARLY_CHUNKS = 1
CH = V7X_MXU_WIDTH
N_CH = D_MODEL // CH
LEAD_STEPS = 8

assert X_HEAD_DIM == CH and N_MEM == CH and D_MODEL // N_POOL == CH


def _rms_norm(x, gain):
    ms = jnp.mean(x * x, axis=-1, keepdims=True)
    return (x * lax.rsqrt(ms + EPS)) * gain


def _gain_rows(gain_blocks_ref, i, width):
    g = gain_blocks_ref[pl.ds(i, 1), :]
    col = jnp.broadcast_to(g, (LANES, LANES)).T
    return jnp.tile(col, (1, width // LANES))


def _sigmoid(x):
    return 1.0 / (1.0 + jnp.exp2(x * NEG_LOG2_E))


def _dot(a, b):
    return jnp.dot(a, b, preferred_element_type=F32)


def _dot_bt(a, b):
    return lax.dot_general(a, b, (((1,), (1,)), ((), ())), preferred_element_type=F32)


def _cols(j):
    return slice(j * CH, (j + 1) * CH)


def _prep_kernel(mem_ref, gain_ref, wkv_ref, wq_ref, gmix_ref, wxo_ref, sm_ref, vo_ref,
                 wkv_buf, wq_buf, wxo_buf):
    @pl.when(pl.program_id(0) == 0)
    def _():
        wkv_buf[...] = wkv_ref[...].astype(BF16)
        for i in range(D_MODEL // LANES):
            rows = slice(i * LANES, (i + 1) * LANES)
            wq_buf[rows, :] = (wq_ref[rows, :] * _gain_rows(gmix_ref, i, D_MODEL)).astype(BF16)
        wxo_buf[...] = wxo_ref[...].astype(BF16)

    mn = _rms_norm(mem_ref[0], gain_ref[...]).astype(BF16)
    scale = X_HEAD_DIM ** -0.5 * -NEG_LOG2_E
    heads = range(X_HEADS)
    k = [(_dot(mn, wkv_buf[:, _cols(h)]) * scale).astype(BF16) for h in heads]
    v = [_dot(mn, wkv_buf[:, D_MODEL + h * CH:D_MODEL + (h + 1) * CH]).astype(BF16) for h in heads]
    for h in heads:
        sm_ref[0, :, _cols(h)] = _dot_bt(wq_buf[:, _cols(h)], k[h]).astype(BF16)
    for h in heads:
        vo_ref[0, _cols(h), :] = _dot(v[h], wxo_buf[_cols(h), :]).astype(BF16)


def _convert_mixer_weights(i, win_ref, gmix_ref, wco_ref, wout_ref, wpool_ref, pscale_ref,
                           win_b, wco_b, wout_b):
    rb = win_ref.shape[0]
    assert rb == LANES
    rows = pl.ds(pl.multiple_of(i * rb, rb), rb)
    gain = _gain_rows(gmix_ref, i, D_MODEL)
    for dst, src in enumerate(MIX_SRC):
        if src != SRC_U_P:
            win_b[rows, dst * D_MODEL:(dst + 1) * D_MODEL] = (
                win_ref[:, src * D_MODEL:(src + 1) * D_MODEL] * gain).astype(BF16)
            continue
        for g in range(N_POOL):
            w_up = (win_ref[:, src * D_MODEL + g * CH:src * D_MODEL + (g + 1) * CH] * gain[:, :CH]).astype(BF16)
            folded = _dot(w_up, wpool_ref[g].astype(BF16)) * pscale_ref[:, _cols(g)]
            win_b[rows, dst * D_MODEL + g * CH:dst * D_MODEL + (g + 1) * CH] = folded.astype(BF16)
    wco_b[rows, :] = wco_ref[...].astype(BF16)
    wout_b[rows, :] = wout_ref[...].astype(BF16)


def _mixer_kernel(x_ref, sm_ref, vo_ref, gmix_ref, convw_ref,
                  win_f32, wco_f32, wout_f32, wpool_ref, pscale_ref,
                  wg_ref, wu_ref, wd_ref,
                  o_ref, wg_o, wu_o, wd_o,
                  win_ref, wco_ref, wout_ref, h_buf, r_buf, cu_buf, up_buf, z_buf, *, lead, n_t):
    s = pl.program_id(0)

    @pl.when(s < lead)
    def _():
        _convert_mixer_weights(s, win_f32, gmix_ref, wco_f32, wout_f32, wpool_ref, pscale_ref,
                               win_ref, wco_ref, wout_ref)

    @pl.when(s >= lead)
    def _():
        _mixer_tile(lax.rem(s - lead, jnp.int32(n_t)), x_ref.at[0], o_ref.at[0], sm_ref, vo_ref,
                    convw_ref, win_ref, wco_ref, wout_ref, wg_ref, wu_ref, wd_ref,
                    wg_o, wu_o, wd_o, h_buf, r_buf, cu_buf, up_buf, z_buf)


def _mixer_tile(t, x_ref, o_ref, sm_ref, vo_ref, convw_ref, win_ref, wco_ref, wout_ref,
                wg_ref, wu_ref, wd_ref, wg_o, wu_o, wd_o, h_buf, r_buf, cu_buf, up_buf, z_buf):
    tile = x_ref.shape[0]

    @pl.when(t == 0)
    def _():
        cu_buf[0:HALO, :] = jnp.zeros((HALO, D_MODEL), F32)
        up_buf[0:HALO, :] = jnp.zeros((HALO, D_MODEL), F32)

    x = x_ref[...]
    h_buf[...] = x.astype(BF16)
    r_buf[...] = jnp.broadcast_to(lax.rsqrt(jnp.mean(x * x, axis=-1, keepdims=True) + EPS), r_buf.shape)

    def proj(col0, j):
        return _dot(h_buf[...], win_ref[:, col0 + j * CH:col0 + (j + 1) * CH]) * r_buf[...]

    for j in range(N_CH):
        cu_buf[HALO:, _cols(j)] = proj(COL_C_A, j) * proj(COL_U_A, j)
    for j in range(N_CH):
        cw = convw_ref[:, _cols(j)]
        conv = cw[CONV_K - 1:CONV_K] * cu_buf[HALO:, _cols(j)]
        for k in range(CONV_K - 1):
            lag = CONV_K - 1 - k
            conv = conv + cw[k:k + 1] * cu_buf[HALO - lag:HALO - lag + tile, _cols(j)]
        z_buf[:, _cols(j)] = (proj(COL_B_A, j) * conv).astype(BF16)
    cu_buf[0:HALO, :] = cu_buf[tile:tile + HALO, :]
    for j in range(N_CH):
        y = _dot(z_buf[...], wco_ref[:, _cols(j)])
        o_ref[:, _cols(j)] = _sigmoid(proj(COL_G_A, j)) * y

    for g in range(N_POOL):
        up_buf[HALO:, _cols(g)] = proj(COL_U_P, g)
    pos = t * tile + 1 + lax.broadcasted_iota(jnp.int32, (tile, 1), 0)
    scores = []
    for g, w in enumerate(POOL_WINDOWS):
        e = up_buf[:, _cols(g)]
        s = e
        k = 1
        while k < w:
            s = s + pltpu.roll(s, k, 0)
            k *= 2
        inv_cnt = 1.0 / jnp.minimum(pos, w).astype(F32)
        y = s[HALO:] * inv_cnt - e[HALO:]
        o_ref[:, _cols(g)] += _sigmoid(proj(COL_G_P, g)) * y
        scores.append(_dot(h_buf[...], sm_ref[0, :, _cols(g)]) * r_buf[...])
    up_buf[0:HALO, :] = up_buf[tile:tile + HALO, :]

    for h in range(X_HEADS):
        s = scores[h]
        p = jnp.exp2(s - jnp.max(s, axis=-1, keepdims=True))
        inv_l = 1.0 / jnp.sum(p, axis=-1, keepdims=True)
        z_buf[:, _cols(h)] = (p * inv_l).astype(BF16)
    for j in range(N_CH):
        y = _dot(z_buf[...], vo_ref[0, :, _cols(j)])
        o_ref[:, _cols(j)] += _sigmoid(proj(COL_G_X, j)) * y

    h_buf[...] = o_ref[...].astype(BF16)
    for j in range(N_CH):
        o_ref[:, _cols(j)] = x_ref[:, _cols(j)] + _dot(h_buf[...], wout_ref[:, _cols(j)])

    wg_o[...] = wg_ref[...].astype(BF16)
    wu_o[...] = wu_ref[...].astype(BF16)
    wd_o[...] = wd_ref[...].astype(BF16)


def _ffn_kernel(x_ref, gain_ref, wg_ref, wu_ref, wd_ref, gfin_ref, o_ref, h_buf, a_buf):
    tile = h_buf.shape[1]
    n_tiles = x_ref.shape[0] // tile
    rows = [pl.ds(i * tile, tile) for i in range(n_tiles)]

    def norm(i):
        h_buf[i % 2] = _rms_norm(x_ref[rows[i], :], gain_ref[...]).astype(BF16)

    def gate_up(i, chunks):
        for c in chunks:
            g = _dot(h_buf[i % 2], wg_ref[:, _cols(c)])
            u = _dot(h_buf[i % 2], wu_ref[:, _cols(c)])
            a_buf[i % 2, :, _cols(c)] = (g * _sigmoid(g) * u).astype(BF16)

    def down_and_final_norm(i):
        for j in range(N_CH):
            o_ref[rows[i], _cols(j)] = x_ref[rows[i], _cols(j)] + _dot(a_buf[i % 2], wd_ref[:, _cols(j)])
        o_ref[rows[i], :] = _rms_norm(o_ref[rows[i], :], gfin_ref[...])

    n_chunks = D_FF // CH
    norm(0)
    gate_up(0, range(n_chunks))
    for i in range(n_tiles):
        if i + 1 < n_tiles:
            norm(i + 1)
            gate_up(i + 1, range(FFN_EARLY_CHUNKS))
        down_and_final_norm(i)
        if i + 1 < n_tiles:
            gate_up(i + 1, range(FFN_EARLY_CHUNKS, n_chunks))


def _resident(shape, index=None):
    index = (0,) * len(shape) if index is None else index
    return pl.BlockSpec(shape, lambda *_: index, pipeline_mode=pl.Buffered(1))


def _nbytes(shape, dtype):
    n = 1
    for s in shape:
        n *= s
    return n * jnp.dtype(dtype).itemsize


def _vmem_limit(resident, streamed, scratch):
    need = sum(resident) + 2 * sum(streamed) + sum(scratch)
    assert need <= V7X_SCOPED_VMEM_BYTES, need
    return V7X_SCOPED_VMEM_BYTES


def kernel(x, mem, norm_mix, w_in, conv_w, w_conv_out, w_pool, pool_scale, norm_mem, w_kv,
           w_xattn_out, w_out, norm_ffn, w_gate, w_up, w_down, norm_final):
    batch, seq, d = x.shape
    assert d == D_MODEL and mem.shape == (batch, N_MEM, D_MODEL)
    assert w_in.shape == (1, D_MODEL, D_IN) and w_gate.shape == (1, D_MODEL, D_FF)
    tile = TOKEN_TILE
    ffn_rows = FFN_STEP_TILES * FFN_TILE
    assert seq % tile == 0 and (batch * seq) % ffn_rows == 0
    row = lambda a: a.reshape(1, -1).astype(F32)
    gain_blocks = lambda a: a.reshape(-1, LANES).astype(F32)
    sq = (D_MODEL, D_MODEL)

    prep_resident = [_nbytes((D_MODEL, 2 * D_MODEL), F32), 2 * _nbytes(sq, F32)]
    prep_streamed = [_nbytes((N_MEM, D_MODEL), F32), 2 * _nbytes(sq, BF16)]
    prep_scratch = [_nbytes((D_MODEL, 2 * D_MODEL), BF16), 2 * _nbytes(sq, BF16)]
    sm, vo = pl.pallas_call(
        _prep_kernel,
        grid=(batch,),
        in_specs=[
            pl.BlockSpec((1, N_MEM, D_MODEL), lambda b: (b, 0, 0)),
            _resident((1, D_MODEL)),
            _resident((D_MODEL, 2 * D_MODEL)),
            _resident(sq, (0, SRC_Q_X)),
            _resident((D_MODEL // LANES, LANES)),
            _resident(sq),
        ],
        out_specs=[
            pl.BlockSpec((1,) + sq, lambda b: (b, 0, 0)),
            pl.BlockSpec((1,) + sq, lambda b: (b, 0, 0)),
        ],
        out_shape=[
            jax.ShapeDtypeStruct((batch,) + sq, BF16),
            jax.ShapeDtypeStruct((batch,) + sq, BF16),
        ],
        scratch_shapes=[
            pltpu.VMEM((D_MODEL, 2 * D_MODEL), BF16),
            pltpu.VMEM(sq, BF16),
            pltpu.VMEM(sq, BF16),
        ],
        compiler_params=pltpu.CompilerParams(
            dimension_semantics=("arbitrary",),
            vmem_limit_bytes=_vmem_limit(prep_resident, prep_streamed, prep_scratch)),
        name="prep",
    )(mem, row(norm_mem[0]), w_kv[0], w_in[0], gain_blocks(norm_mix[0]), w_xattn_out[0])

    lead = LEAD_STEPS
    rb = D_MODEL // lead
    n_t = seq // tile
    n_steps = batch * n_t
    gu_rb = D_MODEL // n_steps
    d_steps = n_steps // 2
    d_rb = D_FF // d_steps
    assert rb % BF16_SUBLANES == 0 and gu_rb % BF16_SUBLANES == 0 and d_rb % BF16_SUBLANES == 0
    tok = lambda s: jnp.maximum(s - lead, 0)
    tile_block = lambda s: (tok(s) // n_t, tok(s) % n_t, 0)
    batch_block = lambda s: (tok(s) // n_t, 0, 0)
    lead_block = lambda s: (jnp.minimum(s, lead - 1), 0)
    step_block = lambda s: (tok(s), 0)
    down_block = lambda s: (jnp.minimum(tok(s), d_steps - 1), 0)
    mixer_resident = [_nbytes((N_POOL, CH, CH), F32)]
    mixer_streamed = [
        2 * _nbytes((tile, D_MODEL), F32), 2 * _nbytes(sq, BF16),
        _nbytes((rb, D_IN), F32), 2 * _nbytes((rb, D_MODEL), F32),
        2 * _nbytes((gu_rb, D_FF), F32), _nbytes((d_rb, D_MODEL), F32),
        2 * _nbytes((gu_rb, D_FF), BF16), _nbytes((d_rb, D_MODEL), BF16)]
    mixer_scratch = [
        _nbytes((D_MODEL, D_MIX), BF16), 2 * _nbytes(sq, BF16),
        2 * _nbytes((tile, D_MODEL), BF16), 2 * _nbytes((HALO + tile, D_MODEL), F32),
        _nbytes((tile, CH), F32)]
    x1, w_gate_b, w_up_b, w_down_b = pl.pallas_call(
        functools.partial(_mixer_kernel, lead=lead, n_t=n_t),
        grid=(lead + n_steps,),
        in_specs=[
            pl.BlockSpec((1, tile, D_MODEL), tile_block),
            pl.BlockSpec((1,) + sq, batch_block),
            pl.BlockSpec((1,) + sq, batch_block),
            _resident((D_MODEL // LANES, LANES)),
            _resident((CONV_K, D_MODEL)),
            pl.BlockSpec((rb, D_IN), lead_block),
            pl.BlockSpec((rb, D_MODEL), lead_block),
            pl.BlockSpec((rb, D_MODEL), lead_block),
            _resident((N_POOL, CH, CH)),
            _resident((1, D_MODEL)),
            pl.BlockSpec((gu_rb, D_FF), step_block),
            pl.BlockSpec((gu_rb, D_FF), step_block),
            pl.BlockSpec((d_rb, D_MODEL), down_block),
        ],
        out_specs=[
            pl.BlockSpec((1, tile, D_MODEL), tile_block),
            pl.BlockSpec((gu_rb, D_FF), step_block),
            pl.BlockSpec((gu_rb, D_FF), step_block),
            pl.BlockSpec((d_rb, D_MODEL), down_block),
        ],
        out_shape=[
            jax.ShapeDtypeStruct((batch, seq, D_MODEL), F32),
            jax.ShapeDtypeStruct((D_MODEL, D_FF), BF16),
            jax.ShapeDtypeStruct((D_MODEL, D_FF), BF16),
            jax.ShapeDtypeStruct((D_FF, D_MODEL), BF16),
        ],
        scratch_shapes=[
            pltpu.VMEM((D_MODEL, D_MIX), BF16),
            pltpu.VMEM(sq, BF16),
            pltpu.VMEM(sq, BF16),
            pltpu.VMEM((tile, D_MODEL), BF16),
            pltpu.VMEM((tile, CH), F32),
            pltpu.VMEM((HALO + tile, D_MODEL), F32),
            pltpu.VMEM((HALO + tile, D_MODEL), F32),
            pltpu.VMEM((tile, D_MODEL), BF16),
        ],
        compiler_params=pltpu.CompilerParams(
            dimension_semantics=("arbitrary",),
            vmem_limit_bytes=_vmem_limit(mixer_resident, mixer_streamed, mixer_scratch)),
        name="mixer",
    )(x, sm, vo, gain_blocks(norm_mix[0]), conv_w[0].astype(F32), w_in[0], w_conv_out[0], w_out[0],
      w_pool[0], row(pool_scale[0]), w_gate[0], w_up[0], w_down[0])

    n_tok = batch * seq
    ffn_resident = [3 * _nbytes((D_MODEL, D_FF), BF16)]
    ffn_streamed = [2 * _nbytes((ffn_rows, D_MODEL), F32)]
    ffn_scratch = [2 * _nbytes((FFN_TILE, D_MODEL), BF16), 2 * _nbytes((FFN_TILE, D_FF), BF16)]
    out = pl.pallas_call(
        _ffn_kernel,
        grid=(n_tok // ffn_rows,),
        in_specs=[
            pl.BlockSpec((ffn_rows, D_MODEL), lambda i: (i, 0)),
            _resident((1, D_MODEL)),
            _resident((D_MODEL, D_FF)),
            _resident((D_MODEL, D_FF)),
            _resident((D_FF, D_MODEL)),
            _resident((1, D_MODEL)),
        ],
        out_specs=pl.BlockSpec((ffn_rows, D_MODEL), lambda i: (i, 0)),
        out_shape=jax.ShapeDtypeStruct((n_tok, D_MODEL), F32),
        scratch_shapes=[
            pltpu.VMEM((2, FFN_TILE, D_MODEL), BF16),
            pltpu.VMEM((2, FFN_TILE, D_FF), BF16),
        ],
        compiler_params=pltpu.CompilerParams(
            dimension_semantics=("arbitrary",),
            vmem_limit_bytes=_vmem_limit(ffn_resident, ffn_streamed, ffn_scratch)),
        name="ffn",
    )(x1.reshape(n_tok, D_MODEL), row(norm_ffn[0]), w_gate_b, w_up_b, w_down_b, row(norm_final))
    return out.reshape(batch, seq, D_MODEL)
```

```python
import functools

import jax
import jax.numpy as jnp
from jax import lax
from jax.experimental import pallas as pl
from jax.experimental.pallas import tpu as pltpu

F32 = jnp.float32
BF16 = jnp.bfloat16

D_MODEL = 1024
N_MEM = 256
CONV_K = 3
POOL_WINDOWS = (2, 4, 8, 16)
N_POOL = len(POOL_WINDOWS)
X_HEADS = 4
X_HEAD_DIM = D_MODEL // X_HEADS
D_FF = 2816
EPS = 1e-6
NEG_LOG2_E = -1.4426950408889634

SRC_B_A, SRC_C_A, SRC_U_A, SRC_U_P, SRC_Q_X, SRC_G_A, SRC_G_P, SRC_G_X = range(8)
D_IN = 8 * D_MODEL
MIX_SRC = (SRC_B_A, SRC_C_A, SRC_U_A, SRC_U_P, SRC_G_A, SRC_G_P, SRC_G_X)
COL_B_A, COL_C_A, COL_U_A, COL_U_P, COL_G_A, COL_G_P, COL_G_X = (
    i * D_MODEL for i in range(len(MIX_SRC)))
D_MIX = len(MIX_SRC) * D_MODEL

V7X_MXU_WIDTH = 256
V7X_VMEM_BYTES = 64 * 1024 * 1024
V7X_SCOPED_VMEM_BYTES = V7X_VMEM_BYTES - 4 * 1024 * 1024
LANES = 128
BF16_SUBLANES = 16
HALO = 16
TOKEN_TILE = 512
FFN_TILE = 256
FFN_STEP_TILES = 4
FFN_EARLY_CHUNKS = 1
CH = V7X_MXU_WIDTH
N_CH = D_MODEL // CH
LEAD_STEPS = 8

assert X_HEAD_DIM == CH and N_MEM == CH and D_MODEL // N_POOL == CH


def _rms_norm(x, gain):
    ms = jnp.mean(x * x, axis=-1, keepdims=True)
    return (x * lax.rsqrt(ms + EPS)) * gain


def _gain_rows(gain_blocks_ref, i, width):
    g = gain_blocks_ref[pl.ds(i, 1), :]
    col = jnp.broadcast_to(g, (LANES, LANES)).T
    return jnp.tile(col, (1, width // LANES))


def _sigmoid(x):
    return 1.0 / (1.0 + jnp.exp2(x * NEG_LOG2_E))


def _dot(a, b):
    return jnp.dot(a, b, preferred_element_type=F32)


def _dot_bt(a, b):
    return lax.dot_general(a, b, (((1,), (1,)), ((), ())), preferred_element_type=F32)


def _cols(j):
    return slice(j * CH, (j + 1) * CH)


def _prep_kernel(mem_ref, gain_ref, wkv_ref, wq_ref, gmix_ref, wxo_ref, sm_ref, vo_ref,
                 wkv_buf, wq_buf, wxo_buf):
    @pl.when(pl.program_id(0) == 0)
    def _():
        wkv_buf[...] = wkv_ref[...].astype(BF16)
        for i in range(D_MODEL // LANES):
            rows = slice(i * LANES, (i + 1) * LANES)
            wq_buf[rows, :] = (wq_ref[rows, :] * _gain_rows(gmix_ref, i, D_MODEL)).astype(BF16)
        wxo_buf[...] = wxo_ref[...].astype(BF16)

    mn = _rms_norm(mem_ref[0], gain_ref[...]).astype(BF16)
    scale = X_HEAD_DIM ** -0.5 * -NEG_LOG2_E
    heads = range(X_HEADS)
    k = [(_dot(mn, wkv_buf[:, _cols(h)]) * scale).astype(BF16) for h in heads]
    v = [_dot(mn, wkv_buf[:, D_MODEL + h * CH:D_MODEL + (h + 1) * CH]).astype(BF16) for h in heads]
    for h in heads:
        sm_ref[0, :, _cols(h)] = _dot_bt(wq_buf[:, _cols(h)], k[h]).astype(BF16)
    for h in heads:
        vo_ref[0, _cols(h), :] = _dot(v[h], wxo_buf[_cols(h), :]).astype(BF16)


def _convert_mixer_weights(i, win_ref, gmix_ref, wco_ref, wout_ref, wpool_ref, pscale_ref,
                           win_b, wco_b, wout_b):
    rb = win_ref.shape[0]
    assert rb == LANES
    rows = pl.ds(pl.multiple_of(i * rb, rb), rb)
    gain = _gain_rows(gmix_ref, i, D_MODEL)
    for dst, src in enumerate(MIX_SRC):
        if src != SRC_U_P:
            win_b[rows, dst * D_MODEL:(dst + 1) * D_MODEL] = (
                win_ref[:, src * D_MODEL:(src + 1) * D_MODEL] * gain).astype(BF16)
            continue
        for g in range(N_POOL):
            w_up = (win_ref[:, src * D_MODEL + g * CH:src * D_MODEL + (g + 1) * CH] * gain[:, :CH]).astype(BF16)
            folded = _dot(w_up, wpool_ref[g].astype(BF16)) * pscale_ref[:, _cols(g)]
            win_b[rows, dst * D_MODEL + g * CH:dst * D_MODEL + (g + 1) * CH] = folded.astype(BF16)
    wco_b[rows, :] = wco_ref[...].astype(BF16)
    wout_b[rows, :] = wout_ref[...].astype(BF16)


def _mixer_kernel(x_ref, sm_ref, vo_ref, gmix_ref, convw_ref,
                  win_f32, wco_f32, wout_f32, wpool_ref, pscale_ref,
                  wg_ref, wu_ref, wd_ref, gffn_ref,
                  o_ref, wg_o, wu_o, wd_o,
                  win_ref, wco_ref, wout_ref, h_buf, r_buf, cu_buf, up_buf, z_buf, gcol_buf, *, lead, n_t):
    s = pl.program_id(0)

    @pl.when(s < lead)
    def _():
        _convert_mixer_weights(s, win_f32, gmix_ref, wco_f32, wout_f32, wpool_ref, pscale_ref,
                               win_ref, wco_ref, wout_ref)

    @pl.when(s >= lead)
    def _():
        _mixer_tile(lax.rem(s - lead, jnp.int32(n_t)), x_ref.at[0], o_ref.at[0], sm_ref, vo_ref,
                    convw_ref, win_ref, wco_ref, wout_ref, h_buf, r_buf, cu_buf, up_buf, z_buf)
        _convert_ffn_weights(s - lead, wg_ref, wu_ref, wd_ref, gffn_ref, wg_o, wu_o, wd_o, gcol_buf)


def _convert_ffn_weights(step, wg_ref, wu_ref, wd_ref, gffn_ref, wg_o, wu_o, wd_o, gcol_buf):
    rb = wg_ref.shape[0]
    per_block = LANES // rb
    gcol_buf[...] = _gain_rows(gffn_ref, step // per_block, LANES)
    off = pl.multiple_of((step % per_block) * rb, rb)
    gain = jnp.tile(gcol_buf[pl.ds(off, rb), :], (1, D_FF // LANES))
    wg_o[...] = (wg_ref[...] * gain).astype(BF16)
    wu_o[...] = (wu_ref[...] * gain).astype(BF16)
    wd_o[...] = wd_ref[...].astype(BF16)


def _mixer_tile(t, x_ref, o_ref, sm_ref, vo_ref, convw_ref, win_ref, wco_ref, wout_ref,
                h_buf, r_buf, cu_buf, up_buf, z_buf):
    tile = x_ref.shape[0]

    @pl.when(t == 0)
    def _():
        cu_buf[0:HALO, :] = jnp.zeros((HALO, D_MODEL), F32)
        up_buf[0:HALO, :] = jnp.zeros((HALO, D_MODEL), F32)

    x = x_ref[...]
    h_buf[...] = x.astype(BF16)
    r_buf[...] = jnp.broadcast_to(lax.rsqrt(jnp.mean(x * x, axis=-1, keepdims=True) + EPS), r_buf.shape)

    def proj(col0, j):
        return _dot(h_buf[...], win_ref[:, col0 + j * CH:col0 + (j + 1) * CH]) * r_buf[...]

    for j in range(N_CH):
        cu_buf[HALO:, _cols(j)] = proj(COL_C_A, j) * proj(COL_U_A, j)
    for j in range(N_CH):
        cw = convw_ref[:, _cols(j)]
        conv = cw[CONV_K - 1:CONV_K] * cu_buf[HALO:, _cols(j)]
        for k in range(CONV_K - 1):
            lag = CONV_K - 1 - k
            conv = conv + cw[k:k + 1] * cu_buf[HALO - lag:HALO - lag + tile, _cols(j)]
        z_buf[:, _cols(j)] = (proj(COL_B_A, j) * conv).astype(BF16)
    cu_buf[0:HALO, :] = cu_buf[tile:tile + HALO, :]
    for j in range(N_CH):
        y = _dot(z_buf[...], wco_ref[:, _cols(j)])
        o_ref[:, _cols(j)] = _sigmoid(proj(COL_G_A, j)) * y

    for g in range(N_POOL):
        up_buf[HALO:, _cols(g)] = proj(COL_U_P, g)
    pos = t * tile + 1 + lax.broadcasted_iota(jnp.int32, (tile, 1), 0)
    scores = []
    for g, w in enumerate(POOL_WINDOWS):
        e = up_buf[:, _cols(g)]
        s = e
        k = 1
        while k < w:
            s = s + pltpu.roll(s, k, 0)
            k *= 2
        inv_cnt = 1.0 / jnp.minimum(pos, w).astype(F32)
        y = s[HALO:] * inv_cnt - e[HALO:]
        o_ref[:, _cols(g)] += _sigmoid(proj(COL_G_P, g)) * y
        scores.append(_dot(h_buf[...], sm_ref[0, :, _cols(g)]) * r_buf[...])
    up_buf[0:HALO, :] = up_buf[tile:tile + HALO, :]

    for h in range(X_HEADS):
        s = scores[h]
        p = jnp.exp2(s - jnp.max(s, axis=-1, keepdims=True))
        inv_l = 1.0 / jnp.sum(p, axis=-1, keepdims=True)
        z_buf[:, _cols(h)] = (p * inv_l).astype(BF16)
    for j in range(N_CH):
        y = _dot(z_buf[...], vo_ref[0, :, _cols(j)])
        o_ref[:, _cols(j)] += _sigmoid(proj(COL_G_X, j)) * y

    h_buf[...] = o_ref[...].astype(BF16)
    for j in range(N_CH):
        o_ref[:, _cols(j)] = x_ref[:, _cols(j)] + _dot(h_buf[...], wout_ref[:, _cols(j)])


def _ffn_kernel(x_ref, wg_ref, wu_ref, wd_ref, gfin_ref, o_ref, h_buf, r_buf, a_buf):
    tile = h_buf.shape[1]
    n_tiles = x_ref.shape[0] // tile
    rows = [pl.ds(i * tile, tile) for i in range(n_tiles)]

    def norm(i):
        x = x_ref[rows[i], :]
        h_buf[i % 2] = x.astype(BF16)
        r_buf[i % 2] = jnp.broadcast_to(lax.rsqrt(jnp.mean(x * x, axis=-1, keepdims=True) + EPS),
                                        r_buf.shape[1:])

    def gate_up(i, chunks):
        for c in chunks:
            g = _dot(h_buf[i % 2], wg_ref[:, _cols(c)]) * r_buf[i % 2]
            u = _dot(h_buf[i % 2], wu_ref[:, _cols(c)]) * r_buf[i % 2]
            a_buf[i % 2, :, _cols(c)] = (g * _sigmoid(g) * u).astype(BF16)

    def down_and_final_norm(i):
        for j in range(N_CH):
            o_ref[rows[i], _cols(j)] = x_ref[rows[i], _cols(j)] + _dot(a_buf[i % 2], wd_ref[:, _cols(j)])
        o_ref[rows[i], :] = _rms_norm(o_ref[rows[i], :], gfin_ref[...])

    n_chunks = D_FF // CH
    norm(0)
    gate_up(0, range(n_chunks))
    for i in range(n_tiles):
        if i + 1 < n_tiles:
            norm(i + 1)
            gate_up(i + 1, range(FFN_EARLY_CHUNKS))
        down_and_final_norm(i)
        if i + 1 < n_tiles:
            gate_up(i + 1, range(FFN_EARLY_CHUNKS, n_chunks))


def _resident(shape, index=None):
    index = (0,) * len(shape) if index is None else index
    return pl.BlockSpec(shape, lambda *_: index, pipeline_mode=pl.Buffered(1))


def _nbytes(shape, dtype):
    n = 1
    for s in shape:
        n *= s
    return n * jnp.dtype(dtype).itemsize


def _vmem_limit(resident, streamed, scratch):
    need = sum(resident) + 2 * sum(streamed) + sum(scratch)
    assert need <= V7X_SCOPED_VMEM_BYTES, need
    return V7X_SCOPED_VMEM_BYTES


def kernel(x, mem, norm_mix, w_in, conv_w, w_conv_out, w_pool, pool_scale, norm_mem, w_kv,
           w_xattn_out, w_out, norm_ffn, w_gate, w_up, w_down, norm_final):
    batch, seq, d = x.shape
    assert d == D_MODEL and mem.shape == (batch, N_MEM, D_MODEL)
    assert w_in.shape == (1, D_MODEL, D_IN) and w_gate.shape == (1, D_MODEL, D_FF)
    tile = TOKEN_TILE
    ffn_rows = FFN_STEP_TILES * FFN_TILE
    assert seq % tile == 0 and (batch * seq) % ffn_rows == 0
    row = lambda a: a.reshape(1, -1).astype(F32)
    gain_blocks = lambda a: a.reshape(-1, LANES).astype(F32)
    sq = (D_MODEL, D_MODEL)

    prep_resident = [_nbytes((D_MODEL, 2 * D_MODEL), F32), 2 * _nbytes(sq, F32)]
    prep_streamed = [_nbytes((N_MEM, D_MODEL), F32), 2 * _nbytes(sq, BF16)]
    prep_scratch = [_nbytes((D_MODEL, 2 * D_MODEL), BF16), 2 * _nbytes(sq, BF16)]
    sm, vo = pl.pallas_call(
        _prep_kernel,
        grid=(batch,),
        in_specs=[
            pl.BlockSpec((1, N_MEM, D_MODEL), lambda b: (b, 0, 0)),
            _resident((1, D_MODEL)),
            _resident((D_MODEL, 2 * D_MODEL)),
            _resident(sq, (0, SRC_Q_X)),
            _resident((D_MODEL // LANES, LANES)),
            _resident(sq),
        ],
        out_specs=[
            pl.BlockSpec((1,) + sq, lambda b: (b, 0, 0)),
            pl.BlockSpec((1,) + sq, lambda b: (b, 0, 0)),
        ],
        out_shape=[
            jax.ShapeDtypeStruct((batch,) + sq, BF16),
            jax.ShapeDtypeStruct((batch,) + sq, BF16),
        ],
        scratch_shapes=[
            pltpu.VMEM((D_MODEL, 2 * D_MODEL), BF16),
            pltpu.VMEM(sq, BF16),
            pltpu.VMEM(sq, BF16),
        ],
        compiler_params=pltpu.CompilerParams(
            dimension_semantics=("arbitrary",),
            vmem_limit_bytes=_vmem_limit(prep_resident, prep_streamed, prep_scratch)),
        name="prep",
    )(mem, row(norm_mem[0]), w_kv[0], w_in[0], gain_blocks(norm_mix[0]), w_xattn_out[0])

    lead = LEAD_STEPS
    rb = D_MODEL // lead
    n_t = seq // tile
    n_steps = batch * n_t
    gu_rb = D_MODEL // n_steps
    d_steps = n_steps // 2
    d_rb = D_FF // d_steps
    assert rb % BF16_SUBLANES == 0 and gu_rb % BF16_SUBLANES == 0 and d_rb % BF16_SUBLANES == 0
    assert LANES % gu_rb == 0
    tok = lambda s: jnp.maximum(s - lead, 0)
    tile_block = lambda s: (tok(s) // n_t, tok(s) % n_t, 0)
    batch_block = lambda s: (tok(s) // n_t, 0, 0)
    lead_block = lambda s: (jnp.minimum(s, lead - 1), 0)
    step_block = lambda s: (tok(s), 0)
    down_block = lambda s: (jnp.minimum(tok(s), d_steps - 1), 0)
    mixer_resident = [_nbytes((N_POOL, CH, CH), F32)]
    mixer_streamed = [
        2 * _nbytes((tile, D_MODEL), F32), 2 * _nbytes(sq, BF16),
        _nbytes((rb, D_IN), F32), 2 * _nbytes((rb, D_MODEL), F32),
        2 * _nbytes((gu_rb, D_FF), F32), _nbytes((d_rb, D_MODEL), F32),
        2 * _nbytes((gu_rb, D_FF), BF16), _nbytes((d_rb, D_MODEL), BF16)]
    mixer_scratch = [
        _nbytes((D_MODEL, D_MIX), BF16), 2 * _nbytes(sq, BF16),
        2 * _nbytes((tile, D_MODEL), BF16), 2 * _nbytes((HALO + tile, D_MODEL), F32),
        _nbytes((tile, CH), F32)]
    x1, w_gate_b, w_up_b, w_down_b = pl.pallas_call(
        functools.partial(_mixer_kernel, lead=lead, n_t=n_t),
        grid=(lead + n_steps,),
        in_specs=[
            pl.BlockSpec((1, tile, D_MODEL), tile_block),
            pl.BlockSpec((1,) + sq, batch_block),
            pl.BlockSpec((1,) + sq, batch_block),
            _resident((D_MODEL // LANES, LANES)),
            _resident((CONV_K, D_MODEL)),
            pl.BlockSpec((rb, D_IN), lead_block),
            pl.BlockSpec((rb, D_MODEL), lead_block),
            pl.BlockSpec((rb, D_MODEL), lead_block),
            _resident((N_POOL, CH, CH)),
            _resident((1, D_MODEL)),
            pl.BlockSpec((gu_rb, D_FF), step_block),
            pl.BlockSpec((gu_rb, D_FF), step_block),
            pl.BlockSpec((d_rb, D_MODEL), down_block),
            _resident((D_MODEL // LANES, LANES)),
        ],
        out_specs=[
            pl.BlockSpec((1, tile, D_MODEL), tile_block),
            pl.BlockSpec((gu_rb, D_FF), step_block),
            pl.BlockSpec((gu_rb, D_FF), step_block),
            pl.BlockSpec((d_rb, D_MODEL), down_block),
        ],
        out_shape=[
            jax.ShapeDtypeStruct((batch, seq, D_MODEL), F32),
            jax.ShapeDtypeStruct((D_MODEL, D_FF), BF16),
            jax.ShapeDtypeStruct((D_MODEL, D_FF), BF16),
            jax.ShapeDtypeStruct((D_FF, D_MODEL), BF16),
        ],
        scratch_shapes=[
            pltpu.VMEM((D_MODEL, D_MIX), BF16),
            pltpu.VMEM(sq, BF16),
            pltpu.VMEM(sq, BF16),
            pltpu.VMEM((tile, D_MODEL), BF16),
            pltpu.VMEM((tile, CH), F32),
            pltpu.VMEM((HALO + tile, D_MODEL), F32),
            pltpu.VMEM((HALO + tile, D_MODEL), F32),
            pltpu.VMEM((tile, D_MODEL), BF16),
            pltpu.VMEM((LANES, LANES), F32),
        ],
        compiler_params=pltpu.CompilerParams(
            dimension_semantics=("arbitrary",),
            vmem_limit_bytes=_vmem_limit(mixer_resident, mixer_streamed, mixer_scratch)),
        name="mixer",
    )(x, sm, vo, gain_blocks(norm_mix[0]), conv_w[0].astype(F32), w_in[0], w_conv_out[0], w_out[0],
      w_pool[0], row(pool_scale[0]), w_gate[0], w_up[0], w_down[0], gain_blocks(norm_ffn[0]))

    n_tok = batch * seq
    ffn_resident = [3 * _nbytes((D_MODEL, D_FF), BF16)]
    ffn_streamed = [2 * _nbytes((ffn_rows, D_MODEL), F32)]
    ffn_scratch = [
        2 * _nbytes((FFN_TILE, D_MODEL), BF16), 2 * _nbytes((FFN_TILE, CH), F32), 2 * _nbytes((FFN_TILE, D_FF), BF16)]
    out = pl.pallas_call(
        _ffn_kernel,
        grid=(n_tok // ffn_rows,),
        in_specs=[
            pl.BlockSpec((ffn_rows, D_MODEL), lambda i: (i, 0)),
            _resident((D_MODEL, D_FF)),
            _resident((D_MODEL, D_FF)),
            _resident((D_FF, D_MODEL)),
            _resident((1, D_MODEL)),
        ],
        out_specs=pl.BlockSpec((ffn_rows, D_MODEL), lambda i: (i, 0)),
        out_shape=jax.ShapeDtypeStruct((n_tok, D_MODEL), F32),
        scratch_shapes=[
            pltpu.VMEM((2, FFN_TILE, D_MODEL), BF16),
            pltpu.VMEM((2, FFN_TILE, CH), F32),
            pltpu.VMEM((2, FFN_TILE, D_FF), BF16),
        ],
        compiler_params=pltpu.CompilerParams(
            dimension_semantics=("arbitrary",),
            vmem_limit_bytes=_vmem_limit(ffn_resident, ffn_streamed, ffn_scratch)),
        name="ffn",
    )(x1.reshape(n_tok, D_MODEL), w_gate_b, w_up_b, w_down_b, row(norm_final))
    return out.reshape(batch, seq, D_MODEL)
```

```python
import functools

import jax
import jax.numpy as jnp
from jax import lax
from jax.experimental import pallas as pl
from jax.experimental.pallas import tpu as pltpu

F32 = jnp.float32
BF16 = jnp.bfloat16

D_MODEL = 1024
N_MEM = 256
CONV_K = 3
POOL_WINDOWS = (2, 4, 8, 16)
N_POOL = len(POOL_WINDOWS)
X_HEADS = 4
X_HEAD_DIM = D_MODEL // X_HEADS
D_FF = 2816
EPS = 1e-6
NEG_LOG2_E = -1.4426950408889634

SRC_B_A, SRC_C_A, SRC_U_A, SRC_U_P, SRC_Q_X, SRC_G_A, SRC_G_P, SRC_G_X = range(8)
D_IN = 8 * D_MODEL
MIX_SRC = (SRC_B_A, SRC_C_A, SRC_U_A, SRC_U_P, SRC_G_A, SRC_G_P, SRC_G_X)
COL_B_A, COL_C_A, COL_U_A, COL_U_P, COL_G_A, COL_G_P, COL_G_X = (
    i * D_MODEL for i in range(len(MIX_SRC)))
D_MIX = len(MIX_SRC) * D_MODEL

V7X_MXU_WIDTH = 256
V7X_VMEM_BYTES = 64 * 1024 * 1024
V7X_SCOPED_VMEM_BYTES = V7X_VMEM_BYTES - 4 * 1024 * 1024
BF16_SUBLANES = 16
HALO = 16
TOKEN_TILE = 512
FFN_TILE = 256
FFN_STEP_TILES = 4
FFN_EARLY_CHUNKS = 1
CH = V7X_MXU_WIDTH
N_CH = D_MODEL // CH
LEAD_STEPS = 8

assert X_HEAD_DIM == CH and N_MEM == CH and D_MODEL // N_POOL == CH


def _rms_norm(x, gain):
    ms = jnp.mean(x * x, axis=-1, keepdims=True)
    return (x * lax.rsqrt(ms + EPS)) * gain


def _sigmoid(x):
    return 1.0 / (1.0 + jnp.exp2(x * NEG_LOG2_E))


def _dot(a, b):
    return jnp.dot(a, b, preferred_element_type=F32)


def _dot_bt(a, b):
    return lax.dot_general(a, b, (((1,), (1,)), ((), ())), preferred_element_type=F32)


def _cols(j):
    return slice(j * CH, (j + 1) * CH)


def _prep_kernel(mem_ref, gain_ref, wkv_ref, wq_ref, wxo_ref, sm_ref, vo_ref,
                 wkv_buf, wq_buf, wxo_buf):
    @pl.when(pl.program_id(0) == 0)
    def _():
        wkv_buf[...] = wkv_ref[...].astype(BF16)
        wq_buf[...] = wq_ref[...].astype(BF16)
        wxo_buf[...] = wxo_ref[...].astype(BF16)

    mn = _rms_norm(mem_ref[0], gain_ref[...]).astype(BF16)
    scale = X_HEAD_DIM ** -0.5 * -NEG_LOG2_E
    heads = range(X_HEADS)
    k = [(_dot(mn, wkv_buf[:, _cols(h)]) * scale).astype(BF16) for h in heads]
    v = [_dot(mn, wkv_buf[:, D_MODEL + h * CH:D_MODEL + (h + 1) * CH]).astype(BF16) for h in heads]
    for h in heads:
        sm_ref[0, :, _cols(h)] = _dot_bt(wq_buf[:, _cols(h)], k[h]).astype(BF16)
    for h in heads:
        vo_ref[0, _cols(h), :] = _dot(v[h], wxo_buf[_cols(h), :]).astype(BF16)


def _convert_mixer_weights(i, win_ref, wco_ref, wout_ref, wpool_ref, pscale_ref, win_b, wco_b, wout_b):
    rb = win_ref.shape[0]
    rows = pl.ds(pl.multiple_of(i * rb, rb), rb)
    for dst, src in enumerate(MIX_SRC):
        if src != SRC_U_P:
            win_b[rows, dst * D_MODEL:(dst + 1) * D_MODEL] = (
                win_ref[:, src * D_MODEL:(src + 1) * D_MODEL].astype(BF16))
            continue
        for g in range(N_POOL):
            w_up = win_ref[:, src * D_MODEL + g * CH:src * D_MODEL + (g + 1) * CH].astype(BF16)
            folded = _dot(w_up, wpool_ref[g].astype(BF16)) * pscale_ref[:, _cols(g)]
            win_b[rows, dst * D_MODEL + g * CH:dst * D_MODEL + (g + 1) * CH] = folded.astype(BF16)
    wco_b[rows, :] = wco_ref[...].astype(BF16)
    wout_b[rows, :] = wout_ref[...].astype(BF16)


def _mixer_kernel(x_ref, sm_ref, vo_ref, gain_ref, convw_ref,
                  win_f32, wco_f32, wout_f32, wpool_ref, pscale_ref,
                  wg_ref, wu_ref, wd_ref,
                  o_ref, wg_o, wu_o, wd_o,
                  win_ref, wco_ref, wout_ref, h_buf, r_buf, cu_buf, up_buf, z_buf, *, lead, n_t):
    s = pl.program_id(0)

    @pl.when(s < lead)
    def _():
        _convert_mixer_weights(s, win_f32, wco_f32, wout_f32, wpool_ref, pscale_ref,
                               win_ref, wco_ref, wout_ref)

    @pl.when(s >= lead)
    def _():
        _mixer_tile(lax.rem(s - lead, jnp.int32(n_t)), x_ref.at[0], o_ref.at[0], sm_ref, vo_ref, gain_ref,
                    convw_ref, win_ref, wco_ref, wout_ref, h_buf, r_buf, cu_buf, up_buf, z_buf)
        wg_o[...] = wg_ref[...].astype(BF16)
        wu_o[...] = wu_ref[...].astype(BF16)
        wd_o[...] = wd_ref[...].astype(BF16)


def _mixer_tile(t, x_ref, o_ref, sm_ref, vo_ref, gain_ref, convw_ref, win_ref, wco_ref, wout_ref,
                h_buf, r_buf, cu_buf, up_buf, z_buf):
    tile = x_ref.shape[0]

    @pl.when(t == 0)
    def _():
        cu_buf[0:HALO, :] = jnp.zeros((HALO, D_MODEL), F32)
        up_buf[0:HALO, :] = jnp.zeros((HALO, D_MODEL), F32)

    x = x_ref[...]
    h_buf[...] = (x * gain_ref[...]).astype(BF16)
    r_buf[...] = jnp.broadcast_to(lax.rsqrt(jnp.mean(x * x, axis=-1, keepdims=True) + EPS), r_buf.shape)

    def proj(col0, j):
        return _dot(h_buf[...], win_ref[:, col0 + j * CH:col0 + (j + 1) * CH]) * r_buf[...]

    for j in range(N_CH):
        cu_buf[HALO:, _cols(j)] = proj(COL_C_A, j) * proj(COL_U_A, j)
    for j in range(N_CH):
        cw = convw_ref[:, _cols(j)]
        conv = cw[CONV_K - 1:CONV_K] * cu_buf[HALO:, _cols(j)]
        for k in range(CONV_K - 1):
            lag = CONV_K - 1 - k
            conv = conv + cw[k:k + 1] * cu_buf[HALO - lag:HALO - lag + tile, _cols(j)]
        z_buf[:, _cols(j)] = (proj(COL_B_A, j) * conv).astype(BF16)
    cu_buf[0:HALO, :] = cu_buf[tile:tile + HALO, :]
    for j in range(N_CH):
        y = _dot(z_buf[...], wco_ref[:, _cols(j)])
        o_ref[:, _cols(j)] = _sigmoid(proj(COL_G_A, j)) * y

    for g in range(N_POOL):
        up_buf[HALO:, _cols(g)] = proj(COL_U_P, g)
    pos = t * tile + 1 + lax.broadcasted_iota(jnp.int32, (tile, 1), 0)
    scores = []
    for g, w in enumerate(POOL_WINDOWS):
        e = up_buf[:, _cols(g)]
        s = e
        k = 1
        while k < w:
            s = s + pltpu.roll(s, k, 0)
            k *= 2
        inv_cnt = 1.0 / jnp.minimum(pos, w).astype(F32)
        y = s[HALO:] * inv_cnt - e[HALO:]
        o_ref[:, _cols(g)] += _sigmoid(proj(COL_G_P, g)) * y
        scores.append(_dot(h_buf[...], sm_ref[0, :, _cols(g)]) * r_buf[...])
    up_buf[0:HALO, :] = up_buf[tile:tile + HALO, :]

    for h in range(X_HEADS):
        s = scores[h]
        p = jnp.exp2(s - jnp.max(s, axis=-1, keepdims=True))
        inv_l = 1.0 / jnp.sum(p, axis=-1, keepdims=True)
        z_buf[:, _cols(h)] = (p * inv_l).astype(BF16)
    for j in range(N_CH):
        y = _dot(z_buf[...], vo_ref[0, :, _cols(j)])
        o_ref[:, _cols(j)] += _sigmoid(proj(COL_G_X, j)) * y

    h_buf[...] = o_ref[...].astype(BF16)
    for j in range(N_CH):
        o_ref[:, _cols(j)] = x_ref[:, _cols(j)] + _dot(h_buf[...], wout_ref[:, _cols(j)])


def _ffn_kernel(x_ref, gain_ref, wg_ref, wu_ref, wd_ref, gfin_ref, o_ref, h_buf, r_buf, a_buf):
    tile = h_buf.shape[1]
    n_tiles = x_ref.shape[0] // tile
    rows = [pl.ds(i * tile, tile) for i in range(n_tiles)]

    def norm(i):
        x = x_ref[rows[i], :]
        h_buf[i % 2] = (x * gain_ref[...]).astype(BF16)
        r_buf[i % 2] = jnp.broadcast_to(lax.rsqrt(jnp.mean(x * x, axis=-1, keepdims=True) + EPS),
                                        r_buf.shape[1:])

    def gate_up(i, chunks):
        for c in chunks:
            g = _dot(h_buf[i % 2], wg_ref[:, _cols(c)]) * r_buf[i % 2]
            u = _dot(h_buf[i % 2], wu_ref[:, _cols(c)]) * r_buf[i % 2]
            a_buf[i % 2, :, _cols(c)] = (g * _sigmoid(g) * u).astype(BF16)

    def down_and_final_norm(i):
        for j in range(N_CH):
            o_ref[rows[i], _cols(j)] = x_ref[rows[i], _cols(j)] + _dot(a_buf[i % 2], wd_ref[:, _cols(j)])
        o_ref[rows[i], :] = _rms_norm(o_ref[rows[i], :], gfin_ref[...])

    n_chunks = D_FF // CH
    norm(0)
    gate_up(0, range(n_chunks))
    for i in range(n_tiles):
        if i + 1 < n_tiles:
            norm(i + 1)
            gate_up(i + 1, range(FFN_EARLY_CHUNKS))
        down_and_final_norm(i)
        if i + 1 < n_tiles:
            gate_up(i + 1, range(FFN_EARLY_CHUNKS, n_chunks))


def _resident(shape, index=None):
    index = (0,) * len(shape) if index is None else index
    return pl.BlockSpec(shape, lambda *_: index, pipeline_mode=pl.Buffered(1))


def _nbytes(shape, dtype):
    n = 1
    for s in shape:
        n *= s
    return n * jnp.dtype(dtype).itemsize


def _vmem_limit(resident, streamed, scratch):
    need = sum(resident) + 2 * sum(streamed) + sum(scratch)
    assert need <= V7X_SCOPED_VMEM_BYTES, need
    return V7X_SCOPED_VMEM_BYTES


def kernel(x, mem, norm_mix, w_in, conv_w, w_conv_out, w_pool, pool_scale, norm_mem, w_kv,
           w_xattn_out, w_out, norm_ffn, w_gate, w_up, w_down, norm_final):
    batch, seq, d = x.shape
    assert d == D_MODEL and mem.shape == (batch, N_MEM, D_MODEL)
    assert w_in.shape == (1, D_MODEL, D_IN) and w_gate.shape == (1, D_MODEL, D_FF)
    tile = TOKEN_TILE
    ffn_rows = FFN_STEP_TILES * FFN_TILE
    assert seq % tile == 0 and (batch * seq) % ffn_rows == 0
    row = lambda a: a.reshape(1, -1).astype(F32)
    sq = (D_MODEL, D_MODEL)

    prep_resident = [_nbytes((D_MODEL, 2 * D_MODEL), F32), 2 * _nbytes(sq, F32)]
    prep_streamed = [_nbytes((N_MEM, D_MODEL), F32), 2 * _nbytes(sq, BF16)]
    prep_scratch = [_nbytes((D_MODEL, 2 * D_MODEL), BF16), 2 * _nbytes(sq, BF16)]
    sm, vo = pl.pallas_call(
        _prep_kernel,
        grid=(batch,),
        in_specs=[
            pl.BlockSpec((1, N_MEM, D_MODEL), lambda b: (b, 0, 0)),
            _resident((1, D_MODEL)),
            _resident((D_MODEL, 2 * D_MODEL)),
            _resident(sq, (0, SRC_Q_X)),
            _resident(sq),
        ],
        out_specs=[
            pl.BlockSpec((1,) + sq, lambda b: (b, 0, 0)),
            pl.BlockSpec((1,) + sq, lambda b: (b, 0, 0)),
        ],
        out_shape=[
            jax.ShapeDtypeStruct((batch,) + sq, BF16),
            jax.ShapeDtypeStruct((batch,) + sq, BF16),
        ],
        scratch_shapes=[
            pltpu.VMEM((D_MODEL, 2 * D_MODEL), BF16),
            pltpu.VMEM(sq, BF16),
            pltpu.VMEM(sq, BF16),
        ],
        compiler_params=pltpu.CompilerParams(
            dimension_semantics=("arbitrary",),
            vmem_limit_bytes=_vmem_limit(prep_resident, prep_streamed, prep_scratch)),
        name="prep",
    )(mem, row(norm_mem[0]), w_kv[0], w_in[0], w_xattn_out[0])

    lead = LEAD_STEPS
    rb = D_MODEL // lead
    n_t = seq // tile
    n_steps = batch * n_t
    gu_rb = D_MODEL // n_steps
    d_steps = n_steps // 2
    d_rb = D_FF // d_steps
    assert rb % BF16_SUBLANES == 0 and gu_rb % BF16_SUBLANES == 0 and d_rb % BF16_SUBLANES == 0
    tok = lambda s: jnp.maximum(s - lead, 0)
    tile_block = lambda s: (tok(s) // n_t, tok(s) % n_t, 0)
    batch_block = lambda s: (tok(s) // n_t, 0, 0)
    lead_block = lambda s: (jnp.minimum(s, lead - 1), 0)
    step_block = lambda s: (tok(s), 0)
    down_block = lambda s: (jnp.minimum(tok(s), d_steps - 1), 0)
    mixer_resident = [_nbytes((N_POOL, CH, CH), F32)]
    mixer_streamed = [
        2 * _nbytes((tile, D_MODEL), F32), 2 * _nbytes(sq, BF16),
        _nbytes((rb, D_IN), F32), 2 * _nbytes((rb, D_MODEL), F32),
        2 * _nbytes((gu_rb, D_FF), F32), _nbytes((d_rb, D_MODEL), F32),
        2 * _nbytes((gu_rb, D_FF), BF16), _nbytes((d_rb, D_MODEL), BF16)]
    mixer_scratch = [
        _nbytes((D_MODEL, D_MIX), BF16), 2 * _nbytes(sq, BF16),
        2 * _nbytes((tile, D_MODEL), BF16), 2 * _nbytes((HALO + tile, D_MODEL), F32),
        _nbytes((tile, CH), F32)]
    x1, w_gate_b, w_up_b, w_down_b = pl.pallas_call(
        functools.partial(_mixer_kernel, lead=lead, n_t=n_t),
        grid=(lead + n_steps,),
        in_specs=[
            pl.BlockSpec((1, tile, D_MODEL), tile_block),
            pl.BlockSpec((1,) + sq, batch_block),
            pl.BlockSpec((1,) + sq, batch_block),
            _resident((1, D_MODEL)),
            _resident((CONV_K, D_MODEL)),
            pl.BlockSpec((rb, D_IN), lead_block),
            pl.BlockSpec((rb, D_MODEL), lead_block),
            pl.BlockSpec((rb, D_MODEL), lead_block),
            _resident((N_POOL, CH, CH)),
            _resident((1, D_MODEL)),
            pl.BlockSpec((gu_rb, D_FF), step_block),
            pl.BlockSpec((gu_rb, D_FF), step_block),
            pl.BlockSpec((d_rb, D_MODEL), down_block),
        ],
        out_specs=[
            pl.BlockSpec((1, tile, D_MODEL), tile_block),
            pl.BlockSpec((gu_rb, D_FF), step_block),
            pl.BlockSpec((gu_rb, D_FF), step_block),
            pl.BlockSpec((d_rb, D_MODEL), down_block),
        ],
        out_shape=[
            jax.ShapeDtypeStruct((batch, seq, D_MODEL), F32),
            jax.ShapeDtypeStruct((D_MODEL, D_FF), BF16),
            jax.ShapeDtypeStruct((D_MODEL, D_FF), BF16),
            jax.ShapeDtypeStruct((D_FF, D_MODEL), BF16),
        ],
        scratch_shapes=[
            pltpu.VMEM((D_MODEL, D_MIX), BF16),
            pltpu.VMEM(sq, BF16),
            pltpu.VMEM(sq, BF16),
            pltpu.VMEM((tile, D_MODEL), BF16),
            pltpu.VMEM((tile, CH), F32),
            pltpu.VMEM((HALO + tile, D_MODEL), F32),
            pltpu.VMEM((HALO + tile, D_MODEL), F32),
            pltpu.VMEM((tile, D_MODEL), BF16),
        ],
        compiler_params=pltpu.CompilerParams(
            dimension_semantics=("arbitrary",),
            vmem_limit_bytes=_vmem_limit(mixer_resident, mixer_streamed, mixer_scratch)),
        name="mixer",
    )(x, sm, vo, row(norm_mix[0]), conv_w[0].astype(F32), w_in[0], w_conv_out[0], w_out[0],
      w_pool[0], row(pool_scale[0]), w_gate[0], w_up[0], w_down[0])

    n_tok = batch * seq
    ffn_resident = [3 * _nbytes((D_MODEL, D_FF), BF16)]
    ffn_streamed = [2 * _nbytes((ffn_rows, D_MODEL), F32)]
    ffn_scratch = [
        2 * _nbytes((FFN_TILE, D_MODEL), BF16), 2 * _nbytes((FFN_TILE, CH), F32), 2 * _nbytes((FFN_TILE, D_FF), BF16)]
    out = pl.pallas_call(
        _ffn_kernel,
        grid=(n_tok // ffn_rows,),
        in_specs=[
            pl.BlockSpec((ffn_rows, D_MODEL), lambda i: (i, 0)),
            _resident((1, D_MODEL)),
            _resident((D_MODEL, D_FF)),
            _resident((D_MODEL, D_FF)),
            _resident((D_FF, D_MODEL)),
            _resident((1, D_MODEL)),
        ],
        out_specs=pl.BlockSpec((ffn_rows, D_MODEL), lambda i: (i, 0)),
        out_shape=jax.ShapeDtypeStruct((n_tok, D_MODEL), F32),
        scratch_shapes=[
            pltpu.VMEM((2, FFN_TILE, D_MODEL), BF16),
            pltpu.VMEM((2, FFN_TILE, CH), F32),
            pltpu.VMEM((2, FFN_TILE, D_FF), BF16),
        ],
        compiler_params=pltpu.CompilerParams(
            dimension_semantics=("arbitrary",),
            vmem_limit_bytes=_vmem_limit(ffn_resident, ffn_streamed, ffn_scratch)),
        name="ffn",
    )(x1.reshape(n_tok, D_MODEL), row(norm_ffn[0]), w_gate_b, w_up_b, w_down_b, row(norm_final))
    return out.reshape(batch, seq, D_MODEL)
```

```python
import functools

import jax
import jax.numpy as jnp
from jax import lax
from jax.experimental import pallas as pl
from jax.experimental.pallas import tpu as pltpu

F32 = jnp.float32
BF16 = jnp.bfloat16

D_MODEL = 1024
N_MEM = 256
CONV_K = 3
POOL_WINDOWS = (2, 4, 8, 16)
N_POOL = len(POOL_WINDOWS)
X_HEADS = 4
X_HEAD_DIM = D_MODEL // X_HEADS
D_FF = 2816
EPS = 1e-6
NEG_LOG2_E = -1.4426950408889634

SRC_B_A, SRC_C_A, SRC_U_A, SRC_U_P, SRC_Q_X, SRC_G_A, SRC_G_P, SRC_G_X = range(8)
D_IN = 8 * D_MODEL
MIX_SRC = (SRC_B_A, SRC_C_A, SRC_U_A, SRC_U_P, SRC_G_A, SRC_G_P, SRC_G_X)
COL_B_A, COL_C_A, COL_U_A, COL_U_P, COL_G_A, COL_G_P, COL_G_X = (
    i * D_MODEL for i in range(len(MIX_SRC)))
D_MIX = len(MIX_SRC) * D_MODEL

V7X_MXU_WIDTH = 256
V7X_VMEM_BYTES = 64 * 1024 * 1024
V7X_SCOPED_VMEM_BYTES = V7X_VMEM_BYTES - 4 * 1024 * 1024
LANES = 128
BF16_SUBLANES = 16
HALO = 16
TOKEN_TILE = 512
FFN_TILE = 256
FFN_STEP_TILES = 4
FFN_EARLY_CHUNKS = 1
CH = V7X_MXU_WIDTH
N_CH = D_MODEL // CH
LEAD_STEPS = 8

assert X_HEAD_DIM == CH and N_MEM == CH and D_MODEL // N_POOL == CH


def _rms_norm(x, gain):
    ms = jnp.mean(x * x, axis=-1, keepdims=True)
    return (x * lax.rsqrt(ms + EPS)) * gain


def _gain_rows(gain_blocks_ref, i, width):
    g = gain_blocks_ref[pl.ds(i, 1), :]
    col = jnp.broadcast_to(g, (LANES, LANES)).T
    return jnp.tile(col, (1, width // LANES))


def _sigmoid(x):
    return 1.0 / (1.0 + jnp.exp2(x * NEG_LOG2_E))


def _dot(a, b):
    return jnp.dot(a, b, preferred_element_type=F32)


def _dot_bt(a, b):
    return lax.dot_general(a, b, (((1,), (1,)), ((), ())), preferred_element_type=F32)


def _cols(j):
    return slice(j * CH, (j + 1) * CH)


def _prep_kernel(mem_ref, gain_ref, wkv_ref, wq_ref, gmix_ref, wxo_ref, sm_ref, vo_ref,
                 wkv_buf, wq_buf, wxo_buf):
    @pl.when(pl.program_id(0) == 0)
    def _():
        wkv_buf[...] = wkv_ref[...].astype(BF16)
        for i in range(D_MODEL // LANES):
            rows = slice(i * LANES, (i + 1) * LANES)
            wq_buf[rows, :] = (wq_ref[rows, :] * _gain_rows(gmix_ref, i, D_MODEL)).astype(BF16)
        wxo_buf[...] = wxo_ref[...].astype(BF16)

    mn = _rms_norm(mem_ref[0], gain_ref[...]).astype(BF16)
    scale = X_HEAD_DIM ** -0.5 * -NEG_LOG2_E
    heads = range(X_HEADS)
    k = [(_dot(mn, wkv_buf[:, _cols(h)]) * scale).astype(BF16) for h in heads]
    v = [_dot(mn, wkv_buf[:, D_MODEL + h * CH:D_MODEL + (h + 1) * CH]).astype(BF16) for h in heads]
    for h in heads:
        sm_ref[0, :, _cols(h)] = _dot_bt(wq_buf[:, _cols(h)], k[h]).astype(BF16)
    for h in heads:
        vo_ref[0, _cols(h), :] = _dot(v[h], wxo_buf[_cols(h), :]).astype(BF16)


def _convert_mixer_weights(i, win_ref, gmix_ref, wco_ref, wout_ref, wpool_ref, pscale_ref,
                           win_b, wco_b, wout_b):
    rb = win_ref.shape[0]
    assert rb == LANES
    rows = pl.ds(pl.multiple_of(i * rb, rb), rb)
    gain = _gain_rows(gmix_ref, i, D_MODEL)
    for dst, src in enumerate(MIX_SRC):
        if src != SRC_U_P:
            win_b[rows, dst * D_MODEL:(dst + 1) * D_MODEL] = (
                win_ref[:, src * D_MODEL:(src + 1) * D_MODEL] * gain).astype(BF16)
            continue
        for g in range(N_POOL):
            w_up = (win_ref[:, src * D_MODEL + g * CH:src * D_MODEL + (g + 1) * CH] * gain[:, :CH]).astype(BF16)
            folded = _dot(w_up, wpool_ref[g].astype(BF16)) * pscale_ref[:, _cols(g)]
            win_b[rows, dst * D_MODEL + g * CH:dst * D_MODEL + (g + 1) * CH] = folded.astype(BF16)
    wco_b[rows, :] = wco_ref[...].astype(BF16)
    wout_b[rows, :] = wout_ref[...].astype(BF16)


def _mixer_kernel(x_ref, sm_ref, vo_ref, gmix_ref, convw_ref,
                  win_f32, wco_f32, wout_f32, wpool_ref, pscale_ref,
                  wg_ref, wu_ref, wd_ref, gffn_ref,
                  o_ref, wg_o, wu_o, wd_o,
                  win_ref, wco_ref, wout_ref, h_buf, r_buf, cu_buf, up_buf, z_buf, gcol_buf, *, lead, n_t):
    s = pl.program_id(0)

    @pl.when(s < lead)
    def _():
        _convert_mixer_weights(s, win_f32, gmix_ref, wco_f32, wout_f32, wpool_ref, pscale_ref,
                               win_ref, wco_ref, wout_ref)

    @pl.when(s >= lead)
    def _():
        _mixer_tile(lax.rem(s - lead, jnp.int32(n_t)), x_ref.at[0], o_ref.at[0], sm_ref, vo_ref,
                    convw_ref, win_ref, wco_ref, wout_ref, h_buf, r_buf, cu_buf, up_buf, z_buf,
                    functools.partial(_convert_ffn_weights, s - lead, wg_ref, wu_ref, wd_ref, gffn_ref,
                                      wg_o, wu_o, wd_o, gcol_buf))


def _convert_ffn_weights(step, wg_ref, wu_ref, wd_ref, gffn_ref, wg_o, wu_o, wd_o, gcol_buf):
    rb = wg_ref.shape[0]
    per_block = LANES // rb
    gcol_buf[...] = _gain_rows(gffn_ref, step // per_block, LANES)
    off = pl.multiple_of((step % per_block) * rb, rb)
    gain = jnp.tile(gcol_buf[pl.ds(off, rb), :], (1, D_FF // LANES))
    wg_o[...] = (wg_ref[...] * gain).astype(BF16)
    wu_o[...] = (wu_ref[...] * gain).astype(BF16)
    wd_o[...] = wd_ref[...].astype(BF16)


def _mixer_tile(t, x_ref, o_ref, sm_ref, vo_ref, convw_ref, win_ref, wco_ref, wout_ref,
                h_buf, r_buf, cu_buf, up_buf, z_buf, side_work):
    tile = x_ref.shape[0]

    @pl.when(t == 0)
    def _():
        cu_buf[0:HALO, :] = jnp.zeros((HALO, D_MODEL), F32)
        up_buf[0:HALO, :] = jnp.zeros((HALO, D_MODEL), F32)

    x = x_ref[...]
    h_buf[...] = x.astype(BF16)
    r_buf[...] = jnp.broadcast_to(lax.rsqrt(jnp.mean(x * x, axis=-1, keepdims=True) + EPS), r_buf.shape)

    def proj(col0, j):
        return _dot(h_buf[...], win_ref[:, col0 + j * CH:col0 + (j + 1) * CH]) * r_buf[...]

    for j in range(N_CH):
        cu_buf[HALO:, _cols(j)] = proj(COL_C_A, j) * proj(COL_U_A, j)
    for j in range(N_CH):
        cw = convw_ref[:, _cols(j)]
        conv = cw[CONV_K - 1:CONV_K] * cu_buf[HALO:, _cols(j)]
        for k in range(CONV_K - 1):
            lag = CONV_K - 1 - k
            conv = conv + cw[k:k + 1] * cu_buf[HALO - lag:HALO - lag + tile, _cols(j)]
        z_buf[:, _cols(j)] = (proj(COL_B_A, j) * conv).astype(BF16)
    cu_buf[0:HALO, :] = cu_buf[tile:tile + HALO, :]
    for j in range(N_CH):
        y = _dot(z_buf[...], wco_ref[:, _cols(j)])
        o_ref[:, _cols(j)] = _sigmoid(proj(COL_G_A, j)) * y

    for g in range(N_POOL):
        up_buf[HALO:, _cols(g)] = proj(COL_U_P, g)
    pos = t * tile + 1 + lax.broadcasted_iota(jnp.int32, (tile, 1), 0)
    scores = []
    for g, w in enumerate(POOL_WINDOWS):
        e = up_buf[:, _cols(g)]
        s = e
        k = 1
        while k < w:
            s = s + pltpu.roll(s, k, 0)
            k *= 2
        inv_cnt = 1.0 / jnp.minimum(pos, w).astype(F32)
        y = s[HALO:] * inv_cnt - e[HALO:]
        o_ref[:, _cols(g)] += _sigmoid(proj(COL_G_P, g)) * y
        scores.append(_dot(h_buf[...], sm_ref[0, :, _cols(g)]) * r_buf[...])
    up_buf[0:HALO, :] = up_buf[tile:tile + HALO, :]

    for h in range(X_HEADS):
        s = scores[h]
        p = jnp.exp2(s - jnp.max(s, axis=-1, keepdims=True))
        inv_l = 1.0 / jnp.sum(p, axis=-1, keepdims=True)
        z_buf[:, _cols(h)] = (p * inv_l).astype(BF16)
    for j in range(N_CH):
        y = _dot(z_buf[...], vo_ref[0, :, _cols(j)])
        o_ref[:, _cols(j)] += _sigmoid(proj(COL_G_X, j)) * y

    side_work()
    h_buf[...] = o_ref[...].astype(BF16)
    for j in range(N_CH):
        o_ref[:, _cols(j)] = x_ref[:, _cols(j)] + _dot(h_buf[...], wout_ref[:, _cols(j)])


def _ffn_kernel(x_ref, wg_ref, wu_ref, wd_ref, gfin_ref, o_ref, h_buf, r_buf, a_buf):
    tile = h_buf.shape[1]
    n_tiles = x_ref.shape[0] // tile
    rows = [pl.ds(i * tile, tile) for i in range(n_tiles)]

    def norm(i):
        x = x_ref[rows[i], :]
        h_buf[i % 2] = x.astype(BF16)
        r_buf[i % 2] = jnp.broadcast_to(lax.rsqrt(jnp.mean(x * x, axis=-1, keepdims=True) + EPS),
                                        r_buf.shape[1:])

    def gate_up(i, chunks):
        for c in chunks:
            g = _dot(h_buf[i % 2], wg_ref[:, _cols(c)]) * r_buf[i % 2]
            u = _dot(h_buf[i % 2], wu_ref[:, _cols(c)]) * r_buf[i % 2]
            a_buf[i % 2, :, _cols(c)] = (g * _sigmoid(g) * u).astype(BF16)

    def down_and_final_norm(i):
        for j in range(N_CH):
            o_ref[rows[i], _cols(j)] = x_ref[rows[i], _cols(j)] + _dot(a_buf[i % 2], wd_ref[:, _cols(j)])
        o_ref[rows[i], :] = _rms_norm(o_ref[rows[i], :], gfin_ref[...])

    n_chunks = D_FF // CH
    norm(0)
    gate_up(0, range(n_chunks))
    for i in range(n_tiles):
        if i + 1 < n_tiles:
            norm(i + 1)
            gate_up(i + 1, range(FFN_EARLY_CHUNKS))
        down_and_final_norm(i)
        if i + 1 < n_tiles:
            gate_up(i + 1, range(FFN_EARLY_CHUNKS, n_chunks))


def _resident(shape, index=None):
    index = (0,) * len(shape) if index is None else index
    return pl.BlockSpec(shape, lambda *_: index, pipeline_mode=pl.Buffered(1))


def _nbytes(shape, dtype):
    n = 1
    for s in shape:
        n *= s
    return n * jnp.dtype(dtype).itemsize


def _vmem_limit(resident, streamed, scratch):
    need = sum(resident) + 2 * sum(streamed) + sum(scratch)
    assert need <= V7X_SCOPED_VMEM_BYTES, need
    return V7X_SCOPED_VMEM_BYTES


def kernel(x, mem, norm_mix, w_in, conv_w, w_conv_out, w_pool, pool_scale, norm_mem, w_kv,
           w_xattn_out, w_out, norm_ffn, w_gate, w_up, w_down, norm_final):
    batch, seq, d = x.shape
    assert d == D_MODEL and mem.shape == (batch, N_MEM, D_MODEL)
    assert w_in.shape == (1, D_MODEL, D_IN) and w_gate.shape == (1, D_MODEL, D_FF)
    tile = TOKEN_TILE
    ffn_rows = FFN_STEP_TILES * FFN_TILE
    assert seq % tile == 0 and (batch * seq) % ffn_rows == 0
    row = lambda a: a.reshape(1, -1).astype(F32)
    gain_blocks = lambda a: a.reshape(-1, LANES).astype(F32)
    sq = (D_MODEL, D_MODEL)

    prep_resident = [_nbytes((D_MODEL, 2 * D_MODEL), F32), 2 * _nbytes(sq, F32)]
    prep_streamed = [_nbytes((N_MEM, D_MODEL), F32), 2 * _nbytes(sq, BF16)]
    prep_scratch = [_nbytes((D_MODEL, 2 * D_MODEL), BF16), 2 * _nbytes(sq, BF16)]
    sm, vo = pl.pallas_call(
        _prep_kernel,
        grid=(batch,),
        in_specs=[
            pl.BlockSpec((1, N_MEM, D_MODEL), lambda b: (b, 0, 0)),
            _resident((1, D_MODEL)),
            _resident((D_MODEL, 2 * D_MODEL)),
            _resident(sq, (0, SRC_Q_X)),
            _resident((D_MODEL // LANES, LANES)),
            _resident(sq),
        ],
        out_specs=[
            pl.BlockSpec((1,) + sq, lambda b: (b, 0, 0)),
            pl.BlockSpec((1,) + sq, lambda b: (b, 0, 0)),
        ],
        out_shape=[
            jax.ShapeDtypeStruct((batch,) + sq, BF16),
            jax.ShapeDtypeStruct((batch,) + sq, BF16),
        ],
        scratch_shapes=[
            pltpu.VMEM((D_MODEL, 2 * D_MODEL), BF16),
            pltpu.VMEM(sq, BF16),
            pltpu.VMEM(sq, BF16),
        ],
        compiler_params=pltpu.CompilerParams(
            dimension_semantics=("arbitrary",),
            vmem_limit_bytes=_vmem_limit(prep_resident, prep_streamed, prep_scratch)),
        name="prep",
    )(mem, row(norm_mem[0]), w_kv[0], w_in[0], gain_blocks(norm_mix[0]), w_xattn_out[0])

    lead = LEAD_STEPS
    rb = D_MODEL // lead
    n_t = seq // tile
    n_steps = batch * n_t
    gu_rb = D_MODEL // n_steps
    d_steps = n_steps // 2
    d_rb = D_FF // d_steps
    assert rb % BF16_SUBLANES == 0 and gu_rb % BF16_SUBLANES == 0 and d_rb % BF16_SUBLANES == 0
    assert LANES % gu_rb == 0
    tok = lambda s: jnp.maximum(s - lead, 0)
    tile_block = lambda s: (tok(s) // n_t, tok(s) % n_t, 0)
    batch_block = lambda s: (tok(s) // n_t, 0, 0)
    lead_block = lambda s: (jnp.minimum(s, lead - 1), 0)
    step_block = lambda s: (tok(s), 0)
    down_block = lambda s: (jnp.minimum(tok(s), d_steps - 1), 0)
    mixer_resident = [_nbytes((N_POOL, CH, CH), F32)]
    mixer_streamed = [
        2 * _nbytes((tile, D_MODEL), F32), 2 * _nbytes(sq, BF16),
        _nbytes((rb, D_IN), F32), 2 * _nbytes((rb, D_MODEL), F32),
        2 * _nbytes((gu_rb, D_FF), F32), _nbytes((d_rb, D_MODEL), F32),
        2 * _nbytes((gu_rb, D_FF), BF16), _nbytes((d_rb, D_MODEL), BF16)]
    mixer_scratch = [
        _nbytes((D_MODEL, D_MIX), BF16), 2 * _nbytes(sq, BF16),
        2 * _nbytes((tile, D_MODEL), BF16), 2 * _nbytes((HALO + tile, D_MODEL), F32),
        _nbytes((tile, CH), F32)]
    x1, w_gate_b, w_up_b, w_down_b = pl.pallas_call(
        functools.partial(_mixer_kernel, lead=lead, n_t=n_t),
        grid=(lead + n_steps,),
        in_specs=[
            pl.BlockSpec((1, tile, D_MODEL), tile_block),
            pl.BlockSpec((1,) + sq, batch_block),
            pl.BlockSpec((1,) + sq, batch_block),
            _resident((D_MODEL // LANES, LANES)),
            _resident((CONV_K, D_MODEL)),
            pl.BlockSpec((rb, D_IN), lead_block),
            pl.BlockSpec((rb, D_MODEL), lead_block),
            pl.BlockSpec((rb, D_MODEL), lead_block),
            _resident((N_POOL, CH, CH)),
            _resident((1, D_MODEL)),
            pl.BlockSpec((gu_rb, D_FF), step_block),
            pl.BlockSpec((gu_rb, D_FF), step_block),
            pl.BlockSpec((d_rb, D_MODEL), down_block),
            _resident((D_MODEL // LANES, LANES)),
        ],
        out_specs=[
            pl.BlockSpec((1, tile, D_MODEL), tile_block),
            pl.BlockSpec((gu_rb, D_FF), step_block),
            pl.BlockSpec((gu_rb, D_FF), step_block),
            pl.BlockSpec((d_rb, D_MODEL), down_block),
        ],
        out_shape=[
            jax.ShapeDtypeStruct((batch, seq, D_MODEL), F32),
            jax.ShapeDtypeStruct((D_MODEL, D_FF), BF16),
            jax.ShapeDtypeStruct((D_MODEL, D_FF), BF16),
            jax.ShapeDtypeStruct((D_FF, D_MODEL), BF16),
        ],
        scratch_shapes=[
            pltpu.VMEM((D_MODEL, D_MIX), BF16),
            pltpu.VMEM(sq, BF16),
            pltpu.VMEM(sq, BF16),
            pltpu.VMEM((tile, D_MODEL), BF16),
            pltpu.VMEM((tile, CH), F32),
            pltpu.VMEM((HALO + tile, D_MODEL), F32),
            pltpu.VMEM((HALO + tile, D_MODEL), F32),
            pltpu.VMEM((tile, D_MODEL), BF16),
            pltpu.VMEM((LANES, LANES), F32),
        ],
        compiler_params=pltpu.CompilerParams(
            dimension_semantics=("arbitrary",),
            vmem_limit_bytes=_vmem_limit(mixer_resident, mixer_streamed, mixer_scratch)),
        name="mixer",
    )(x, sm, vo, gain_blocks(norm_mix[0]), conv_w[0].astype(F32), w_in[0], w_conv_out[0], w_out[0],
      w_pool[0], row(pool_scale[0]), w_gate[0], w_up[0], w_down[0], gain_blocks(norm_ffn[0]))

    n_tok = batch * seq
    ffn_resident = [3 * _nbytes((D_MODEL, D_FF), BF16)]
    ffn_streamed = [2 * _nbytes((ffn_rows, D_MODEL), F32)]
    ffn_scratch = [
        2 * _nbytes((FFN_TILE, D_MODEL), BF16), 2 * _nbytes((FFN_TILE, CH), F32), 2 * _nbytes((FFN_TILE, D_FF), BF16)]
    out = pl.pallas_call(
        _ffn_kernel,
        grid=(n_tok // ffn_rows,),
        in_specs=[
            pl.BlockSpec((ffn_rows, D_MODEL), lambda i: (i, 0)),
            _resident((D_MODEL, D_FF)),
            _resident((D_MODEL, D_FF)),
            _resident((D_FF, D_MODEL)),
            _resident((1, D_MODEL)),
        ],
        out_specs=pl.BlockSpec((ffn_rows, D_MODEL), lambda i: (i, 0)),
        out_shape=jax.ShapeDtypeStruct((n_tok, D_MODEL), F32),
        scratch_shapes=[
            pltpu.VMEM((2, FFN_TILE, D_MODEL), BF16),
            pltpu.VMEM((2, FFN_TILE, CH), F32),
            pltpu.VMEM((2, FFN_TILE, D_FF), BF16),
        ],
        compiler_params=pltpu.CompilerParams(
            dimension_semantics=("arbitrary",),
            vmem_limit_bytes=_vmem_limit(ffn_resident, ffn_streamed, ffn_scratch)),
        name="ffn",
    )(x1.reshape(n_tok, D_MODEL), w_gate_b, w_up_b, w_down_b, row(norm_final))
    return out.reshape(batch, seq, D_MODEL)
```

```python
import functools

import jax
import jax.numpy as jnp
from jax import lax
from jax.experimental import pallas as pl
from jax.experimental.pallas import tpu as pltpu

F32 = jnp.float32
BF16 = jnp.bfloat16

D_MODEL = 1024
N_MEM = 256
CONV_K = 3
POOL_WINDOWS = (2, 4, 8, 16)
N_POOL = len(POOL_WINDOWS)
X_HEADS = 4
X_HEAD_DIM = D_MODEL // X_HEADS
D_FF = 2816
EPS = 1e-6
NEG_LOG2_E = -1.4426950408889634

SRC_B_A, SRC_C_A, SRC_U_A, SRC_U_P, SRC_Q_X, SRC_G_A, SRC_G_P, SRC_G_X = range(8)
D_IN = 8 * D_MODEL
MIX_SRC = (SRC_B_A, SRC_C_A, SRC_U_A, SRC_U_P, SRC_G_A, SRC_G_P, SRC_G_X)
COL_B_A, COL_C_A, COL_U_A, COL_U_P, COL_G_A, COL_G_P, COL_G_X = (
    i * D_MODEL for i in range(len(MIX_SRC)))
D_MIX = len(MIX_SRC) * D_MODEL

V7X_MXU_WIDTH = 256
V7X_VMEM_BYTES = 64 * 1024 * 1024
V7X_SCOPED_VMEM_BYTES = V7X_VMEM_BYTES - 4 * 1024 * 1024
LANES = 128
BF16_SUBLANES = 16
HALO = 16
TOKEN_TILE = 512
FFN_TILE = 256
FFN_STEP_TILES = 4
FFN_EARLY_CHUNKS = 1
CH = V7X_MXU_WIDTH
N_CH = D_MODEL // CH
LEAD_STEPS = 8

assert X_HEAD_DIM == CH and N_MEM == CH and D_MODEL // N_POOL == CH


def _rms_norm(x, gain):
    ms = jnp.mean(x * x, axis=-1, keepdims=True)
    return (x * lax.rsqrt(ms + EPS)) * gain


def _gain_rows(gain_blocks_ref, i, width):
    g = gain_blocks_ref[pl.ds(i, 1), :]
    col = jnp.broadcast_to(g, (LANES, LANES)).T
    return jnp.tile(col, (1, width // LANES))


def _sigmoid(x):
    return 1.0 / (1.0 + jnp.exp2(x * NEG_LOG2_E))


def _dot(a, b):
    return jnp.dot(a, b, preferred_element_type=F32)


def _dot_bt(a, b):
    return lax.dot_general(a, b, (((1,), (1,)), ((), ())), preferred_element_type=F32)


def _cols(j):
    return slice(j * CH, (j + 1) * CH)


def _prep_kernel(mem_ref, gain_ref, gmix_ref, wkv_hbm, win_hbm, wxo_hbm, sm_ref, vo_ref,
                 wkv_f32, wq_f32, wxo_f32, wkv_buf, wq_buf, wxo_buf, sems):
    first = pl.program_id(0) == 0
    copies = (
        pltpu.make_async_copy(wkv_hbm, wkv_f32, sems.at[0]),
        pltpu.make_async_copy(win_hbm.at[:, pl.ds(SRC_Q_X * D_MODEL, D_MODEL)], wq_f32, sems.at[1]),
        pltpu.make_async_copy(wxo_hbm, wxo_f32, sems.at[2]),
    )

    @pl.when(first)
    def _():
        for copy in copies:
            copy.start()

    mn = _rms_norm(mem_ref[0], gain_ref[...]).astype(BF16)

    @pl.when(first)
    def _():
        copies[0].wait()
        wkv_buf[...] = wkv_f32[...].astype(BF16)

    scale = X_HEAD_DIM ** -0.5 * -NEG_LOG2_E
    heads = range(X_HEADS)
    k = [(_dot(mn, wkv_buf[:, _cols(h)]) * scale).astype(BF16) for h in heads]
    v = [_dot(mn, wkv_buf[:, D_MODEL + h * CH:D_MODEL + (h + 1) * CH]).astype(BF16) for h in heads]

    @pl.when(first)
    def _():
        copies[1].wait()
        for i in range(D_MODEL // LANES):
            rows = slice(i * LANES, (i + 1) * LANES)
            wq_buf[rows, :] = (wq_f32[rows, :] * _gain_rows(gmix_ref, i, D_MODEL)).astype(BF16)

    for h in heads:
        sm_ref[0, :, _cols(h)] = _dot_bt(wq_buf[:, _cols(h)], k[h]).astype(BF16)

    @pl.when(first)
    def _():
        copies[2].wait()
        wxo_buf[...] = wxo_f32[...].astype(BF16)

    for h in heads:
        vo_ref[0, _cols(h), :] = _dot(v[h], wxo_buf[_cols(h), :]).astype(BF16)


def _convert_mixer_weights(i, win_ref, gmix_ref, wco_ref, wout_ref, wpool_ref, pscale_ref,
                           win_b, wco_b, wout_b):
    rb = win_ref.shape[0]
    assert rb == LANES
    rows = pl.ds(pl.multiple_of(i * rb, rb), rb)
    gain = _gain_rows(gmix_ref, i, D_MODEL)
    for dst, src in enumerate(MIX_SRC):
        if src != SRC_U_P:
            win_b[rows, dst * D_MODEL:(dst + 1) * D_MODEL] = (
                win_ref[:, src * D_MODEL:(src + 1) * D_MODEL] * gain).astype(BF16)
            continue
        for g in range(N_POOL):
            w_up = (win_ref[:, src * D_MODEL + g * CH:src * D_MODEL + (g + 1) * CH] * gain[:, :CH]).astype(BF16)
            folded = _dot(w_up, wpool_ref[g].astype(BF16)) * pscale_ref[:, _cols(g)]
            win_b[rows, dst * D_MODEL + g * CH:dst * D_MODEL + (g + 1) * CH] = folded.astype(BF16)
    wco_b[rows, :] = wco_ref[...].astype(BF16)
    wout_b[rows, :] = wout_ref[...].astype(BF16)


def _mixer_kernel(x_ref, sm_ref, vo_ref, gmix_ref, convw_ref,
                  win_f32, wco_f32, wout_f32, wpool_ref, pscale_ref,
                  wg_ref, wu_ref, wd_ref, gffn_ref,
                  o_ref, wg_o, wu_o, wd_o,
                  win_ref, wco_ref, wout_ref, h_buf, r_buf, cu_buf, up_buf, z_buf, gcol_buf, *, lead, n_t):
    s = pl.program_id(0)

    @pl.when(s < lead)
    def _():
        _convert_mixer_weights(s, win_f32, gmix_ref, wco_f32, wout_f32, wpool_ref, pscale_ref,
                               win_ref, wco_ref, wout_ref)

    @pl.when(s >= lead)
    def _():
        _mixer_tile(lax.rem(s - lead, jnp.int32(n_t)), x_ref.at[0], o_ref.at[0], sm_ref, vo_ref,
                    convw_ref, win_ref, wco_ref, wout_ref, h_buf, r_buf, cu_buf, up_buf, z_buf,
                    functools.partial(_convert_ffn_weights, s - lead, wg_ref, wu_ref, wd_ref, gffn_ref,
                                      wg_o, wu_o, wd_o, gcol_buf))


def _convert_ffn_weights(step, wg_ref, wu_ref, wd_ref, gffn_ref, wg_o, wu_o, wd_o, gcol_buf):
    rb = wg_ref.shape[0]
    per_block = LANES // rb
    gcol_buf[...] = _gain_rows(gffn_ref, step // per_block, LANES)
    off = pl.multiple_of((step % per_block) * rb, rb)
    gain = jnp.tile(gcol_buf[pl.ds(off, rb), :], (1, D_FF // LANES))
    wg_o[...] = (wg_ref[...] * gain).astype(BF16)
    wu_o[...] = (wu_ref[...] * gain).astype(BF16)
    wd_o[...] = wd_ref[...].astype(BF16)


def _mixer_tile(t, x_ref, o_ref, sm_ref, vo_ref, convw_ref, win_ref, wco_ref, wout_ref,
                h_buf, r_buf, cu_buf, up_buf, z_buf, side_work):
    tile = x_ref.shape[0]

    @pl.when(t == 0)
    def _():
        cu_buf[0:HALO, :] = jnp.zeros((HALO, D_MODEL), F32)
        up_buf[0:HALO, :] = jnp.zeros((HALO, D_MODEL), F32)

    x = x_ref[...]
    h_buf[...] = x.astype(BF16)
    r_buf[...] = jnp.broadcast_to(lax.rsqrt(jnp.mean(x * x, axis=-1, keepdims=True) + EPS), r_buf.shape)

    def proj(col0, j):
        return _dot(h_buf[...], win_ref[:, col0 + j * CH:col0 + (j + 1) * CH]) * r_buf[...]

    for j in range(N_CH):
        cu_buf[HALO:, _cols(j)] = proj(COL_C_A, j) * proj(COL_U_A, j)
    for j in range(N_CH):
        cw = convw_ref[:, _cols(j)]
        conv = cw[CONV_K - 1:CONV_K] * cu_buf[HALO:, _cols(j)]
        for k in range(CONV_K - 1):
            lag = CONV_K - 1 - k
            conv = conv + cw[k:k + 1] * cu_buf[HALO - lag:HALO - lag + tile, _cols(j)]
        z_buf[:, _cols(j)] = (proj(COL_B_A, j) * conv).astype(BF16)
    cu_buf[0:HALO, :] = cu_buf[tile:tile + HALO, :]
    for j in range(N_CH):
        y = _dot(z_buf[...], wco_ref[:, _cols(j)])
        o_ref[:, _cols(j)] = _sigmoid(proj(COL_G_A, j)) * y

    for g in range(N_POOL):
        up_buf[HALO:, _cols(g)] = proj(COL_U_P, g)
    pos = t * tile + 1 + lax.broadcasted_iota(jnp.int32, (tile, 1), 0)
    scores = []
    for g, w in enumerate(POOL_WINDOWS):
        e = up_buf[:, _cols(g)]
        s = e
        k = 1
        while k < w:
            s = s + pltpu.roll(s, k, 0)
            k *= 2
        inv_cnt = 1.0 / jnp.minimum(pos, w).astype(F32)
        y = s[HALO:] * inv_cnt - e[HALO:]
        o_ref[:, _cols(g)] += _sigmoid(proj(COL_G_P, g)) * y
        scores.append(_dot(h_buf[...], sm_ref[0, :, _cols(g)]) * r_buf[...])
    up_buf[0:HALO, :] = up_buf[tile:tile + HALO, :]

    for h in range(X_HEADS):
        s = scores[h]
        p = jnp.exp2(s - jnp.max(s, axis=-1, keepdims=True))
        inv_l = 1.0 / jnp.sum(p, axis=-1, keepdims=True)
        z_buf[:, _cols(h)] = (p * inv_l).astype(BF16)
    for j in range(N_CH):
        y = _dot(z_buf[...], vo_ref[0, :, _cols(j)])
        o_ref[:, _cols(j)] += _sigmoid(proj(COL_G_X, j)) * y

    side_work()
    h_buf[...] = o_ref[...].astype(BF16)
    for j in range(N_CH):
        o_ref[:, _cols(j)] = x_ref[:, _cols(j)] + _dot(h_buf[...], wout_ref[:, _cols(j)])


def _ffn_kernel(x_ref, wg_ref, wu_ref, wd_ref, gfin_ref, o_ref, h_buf, r_buf, a_buf):
    tile = h_buf.shape[1]
    n_tiles = x_ref.shape[0] // tile
    rows = [pl.ds(i * tile, tile) for i in range(n_tiles)]

    def norm(i):
        x = x_ref[rows[i], :]
        h_buf[i % 2] = x.astype(BF16)
        r_buf[i % 2] = jnp.broadcast_to(lax.rsqrt(jnp.mean(x * x, axis=-1, keepdims=True) + EPS),
                                        r_buf.shape[1:])

    def gate_up(i, chunks):
        for c in chunks:
            g = _dot(h_buf[i % 2], wg_ref[:, _cols(c)]) * r_buf[i % 2]
            u = _dot(h_buf[i % 2], wu_ref[:, _cols(c)]) * r_buf[i % 2]
            a_buf[i % 2, :, _cols(c)] = (g * _sigmoid(g) * u).astype(BF16)

    def down_and_final_norm(i):
        for j in range(N_CH):
            o_ref[rows[i], _cols(j)] = x_ref[rows[i], _cols(j)] + _dot(a_buf[i % 2], wd_ref[:, _cols(j)])
        o_ref[rows[i], :] = _rms_norm(o_ref[rows[i], :], gfin_ref[...])

    n_chunks = D_FF // CH
    norm(0)
    gate_up(0, range(n_chunks))
    for i in range(n_tiles):
        if i + 1 < n_tiles:
            norm(i + 1)
            gate_up(i + 1, range(FFN_EARLY_CHUNKS))
        down_and_final_norm(i)
        if i + 1 < n_tiles:
            gate_up(i + 1, range(FFN_EARLY_CHUNKS, n_chunks))


def _resident(shape, index=None):
    index = (0,) * len(shape) if index is None else index
    return pl.BlockSpec(shape, lambda *_: index, pipeline_mode=pl.Buffered(1))


def _nbytes(shape, dtype):
    n = 1
    for s in shape:
        n *= s
    return n * jnp.dtype(dtype).itemsize


def _vmem_limit(resident, streamed, scratch):
    need = sum(resident) + 2 * sum(streamed) + sum(scratch)
    assert need <= V7X_SCOPED_VMEM_BYTES, need
    return V7X_SCOPED_VMEM_BYTES


def kernel(x, mem, norm_mix, w_in, conv_w, w_conv_out, w_pool, pool_scale, norm_mem, w_kv,
           w_xattn_out, w_out, norm_ffn, w_gate, w_up, w_down, norm_final):
    batch, seq, d = x.shape
    assert d == D_MODEL and mem.shape == (batch, N_MEM, D_MODEL)
    assert w_in.shape == (1, D_MODEL, D_IN) and w_gate.shape == (1, D_MODEL, D_FF)
    tile = TOKEN_TILE
    ffn_rows = FFN_STEP_TILES * FFN_TILE
    assert seq % tile == 0 and (batch * seq) % ffn_rows == 0
    row = lambda a: a.reshape(1, -1).astype(F32)
    gain_blocks = lambda a: a.reshape(-1, LANES).astype(F32)
    sq = (D_MODEL, D_MODEL)

    prep_resident = [_nbytes((D_MODEL, 2 * D_MODEL), F32), 2 * _nbytes(sq, F32)]
    prep_streamed = [_nbytes((N_MEM, D_MODEL), F32), 2 * _nbytes(sq, BF16)]
    prep_scratch = [_nbytes((D_MODEL, 2 * D_MODEL), BF16), 2 * _nbytes(sq, BF16)]
    sm, vo = pl.pallas_call(
        _prep_kernel,
        grid=(batch,),
        in_specs=[
            pl.BlockSpec((1, N_MEM, D_MODEL), lambda b: (b, 0, 0)),
            _resident((1, D_MODEL)),
            _resident((D_MODEL // LANES, LANES)),
            pl.BlockSpec(memory_space=pl.ANY),
            pl.BlockSpec(memory_space=pl.ANY),
            pl.BlockSpec(memory_space=pl.ANY),
        ],
        out_specs=[
            pl.BlockSpec((1,) + sq, lambda b: (b, 0, 0)),
            pl.BlockSpec((1,) + sq, lambda b: (b, 0, 0)),
        ],
        out_shape=[
            jax.ShapeDtypeStruct((batch,) + sq, BF16),
            jax.ShapeDtypeStruct((batch,) + sq, BF16),
        ],
        scratch_shapes=[
            pltpu.VMEM((D_MODEL, 2 * D_MODEL), F32),
            pltpu.VMEM(sq, F32),
            pltpu.VMEM(sq, F32),
            pltpu.VMEM((D_MODEL, 2 * D_MODEL), BF16),
            pltpu.VMEM(sq, BF16),
            pltpu.VMEM(sq, BF16),
            pltpu.SemaphoreType.DMA((3,)),
        ],
        compiler_params=pltpu.CompilerParams(
            dimension_semantics=("arbitrary",),
            vmem_limit_bytes=_vmem_limit(prep_resident, prep_streamed, prep_scratch)),
        name="prep",
    )(mem, row(norm_mem[0]), gain_blocks(norm_mix[0]), w_kv[0], w_in[0], w_xattn_out[0])

    lead = LEAD_STEPS
    rb = D_MODEL // lead
    n_t = seq // tile
    n_steps = batch * n_t
    gu_rb = D_MODEL // n_steps
    d_steps = n_steps // 2
    d_rb = D_FF // d_steps
    assert rb % BF16_SUBLANES == 0 and gu_rb % BF16_SUBLANES == 0 and d_rb % BF16_SUBLANES == 0
    assert LANES % gu_rb == 0
    tok = lambda s: jnp.maximum(s - lead, 0)
    tile_block = lambda s: (tok(s) // n_t, tok(s) % n_t, 0)
    batch_block = lambda s: (tok(s) // n_t, 0, 0)
    lead_block = lambda s: (jnp.minimum(s, lead - 1), 0)
    step_block = lambda s: (tok(s), 0)
    down_block = lambda s: (jnp.minimum(tok(s), d_steps - 1), 0)
    mixer_resident = [_nbytes((N_POOL, CH, CH), F32)]
    mixer_streamed = [
        2 * _nbytes((tile, D_MODEL), F32), 2 * _nbytes(sq, BF16),
        _nbytes((rb, D_IN), F32), 2 * _nbytes((rb, D_MODEL), F32),
        2 * _nbytes((gu_rb, D_FF), F32), _nbytes((d_rb, D_MODEL), F32),
        2 * _nbytes((gu_rb, D_FF), BF16), _nbytes((d_rb, D_MODEL), BF16)]
    mixer_scratch = [
        _nbytes((D_MODEL, D_MIX), BF16), 2 * _nbytes(sq, BF16),
        2 * _nbytes((tile, D_MODEL), BF16), 2 * _nbytes((HALO + tile, D_MODEL), F32),
        _nbytes((tile, CH), F32)]
    x1, w_gate_b, w_up_b, w_down_b = pl.pallas_call(
        functools.partial(_mixer_kernel, lead=lead, n_t=n_t),
        grid=(lead + n_steps,),
        in_specs=[
            pl.BlockSpec((1, tile, D_MODEL), tile_block),
            pl.BlockSpec((1,) + sq, batch_block),
            pl.BlockSpec((1,) + sq, batch_block),
            _resident((D_MODEL // LANES, LANES)),
            _resident((CONV_K, D_MODEL)),
            pl.BlockSpec((rb, D_IN), lead_block),
            pl.BlockSpec((rb, D_MODEL), lead_block),
            pl.BlockSpec((rb, D_MODEL), lead_block),
            _resident((N_POOL, CH, CH)),
            _resident((1, D_MODEL)),
            pl.BlockSpec((gu_rb, D_FF), step_block),
            pl.BlockSpec((gu_rb, D_FF), step_block),
            pl.BlockSpec((d_rb, D_MODEL), down_block),
            _resident((D_MODEL // LANES, LANES)),
        ],
        out_specs=[
            pl.BlockSpec((1, tile, D_MODEL), tile_block),
            pl.BlockSpec((gu_rb, D_FF), step_block),
            pl.BlockSpec((gu_rb, D_FF), step_block),
            pl.BlockSpec((d_rb, D_MODEL), down_block),
        ],
        out_shape=[
            jax.ShapeDtypeStruct((batch, seq, D_MODEL), F32),
            jax.ShapeDtypeStruct((D_MODEL, D_FF), BF16),
            jax.ShapeDtypeStruct((D_MODEL, D_FF), BF16),
            jax.ShapeDtypeStruct((D_FF, D_MODEL), BF16),
        ],
        scratch_shapes=[
            pltpu.VMEM((D_MODEL, D_MIX), BF16),
            pltpu.VMEM(sq, BF16),
            pltpu.VMEM(sq, BF16),
            pltpu.VMEM((tile, D_MODEL), BF16),
            pltpu.VMEM((tile, CH), F32),
            pltpu.VMEM((HALO + tile, D_MODEL), F32),
            pltpu.VMEM((HALO + tile, D_MODEL), F32),
            pltpu.VMEM((tile, D_MODEL), BF16),
            pltpu.VMEM((LANES, LANES), F32),
        ],
        compiler_params=pltpu.CompilerParams(
            dimension_semantics=("arbitrary",),
            vmem_limit_bytes=_vmem_limit(mixer_resident, mixer_streamed, mixer_scratch)),
        name="mixer",
    )(x, sm, vo, gain_blocks(norm_mix[0]), conv_w[0].astype(F32), w_in[0], w_conv_out[0], w_out[0],
      w_pool[0], row(pool_scale[0]), w_gate[0], w_up[0], w_down[0], gain_blocks(norm_ffn[0]))

    n_tok = batch * seq
    ffn_resident = [3 * _nbytes((D_MODEL, D_FF), BF16)]
    ffn_streamed = [2 * _nbytes((ffn_rows, D_MODEL), F32)]
    ffn_scratch = [
        2 * _nbytes((FFN_TILE, D_MODEL), BF16), 2 * _nbytes((FFN_TILE, CH), F32), 2 * _nbytes((FFN_TILE, D_FF), BF16)]
    out = pl.pallas_call(
        _ffn_kernel,
        grid=(n_tok // ffn_rows,),
        in_specs=[
            pl.BlockSpec((ffn_rows, D_MODEL), lambda i: (i, 0)),
            _resident((D_MODEL, D_FF)),
            _resident((D_MODEL, D_FF)),
            _resident((D_FF, D_MODEL)),
            _resident((1, D_MODEL)),
        ],
        out_specs=pl.BlockSpec((ffn_rows, D_MODEL), lambda i: (i, 0)),
        out_shape=jax.ShapeDtypeStruct((n_tok, D_MODEL), F32),
        scratch_shapes=[
            pltpu.VMEM((2, FFN_TILE, D_MODEL), BF16),
            pltpu.VMEM((2, FFN_TILE, CH), F32),
            pltpu.VMEM((2, FFN_TILE, D_FF), BF16),
        ],
        compiler_params=pltpu.CompilerParams(
            dimension_semantics=("arbitrary",),
            vmem_limit_bytes=_vmem_limit(ffn_resident, ffn_streamed, ffn_scratch)),
        name="ffn",
    )(x1.reshape(n_tok, D_MODEL), w_gate_b, w_up_b, w_down_b, row(norm_final))
    return out.reshape(batch, seq, D_MODEL)
```

```python
import functools

import jax
import jax.numpy as jnp
from jax import lax
from jax.experimental import pallas as pl
from jax.experimental.pallas import tpu as pltpu

F32 = jnp.float32
BF16 = jnp.bfloat16

D_MODEL = 1024
N_MEM = 256
CONV_K = 3
POOL_WINDOWS = (2, 4, 8, 16)
N_POOL = len(POOL_WINDOWS)
X_HEADS = 4
X_HEAD_DIM = D_MODEL // X_HEADS
D_FF = 2816
EPS = 1e-6
NEG_LOG2_E = -1.4426950408889634

SRC_B_A, SRC_C_A, SRC_U_A, SRC_U_P, SRC_Q_X, SRC_G_A, SRC_G_P, SRC_G_X = range(8)
D_IN = 8 * D_MODEL
MIX_SRC = (SRC_B_A, SRC_C_A, SRC_U_A, SRC_U_P, SRC_G_A, SRC_G_P, SRC_G_X)
COL_B_A, COL_C_A, COL_U_A, COL_U_P, COL_G_A, COL_G_P, COL_G_X = (
    i * D_MODEL for i in range(len(MIX_SRC)))
D_MIX = len(MIX_SRC) * D_MODEL

V7X_MXU_WIDTH = 256
V7X_VMEM_BYTES = 64 * 1024 * 1024
V7X_SCOPED_VMEM_BYTES = V7X_VMEM_BYTES - 4 * 1024 * 1024
LANES = 128
BF16_SUBLANES = 16
HALO = 16
TOKEN_TILE = 512
FFN_TILE = 256
FFN_STEP_TILES = 4
FFN_EARLY_CHUNKS = 1
CH = V7X_MXU_WIDTH
N_CH = D_MODEL // CH
LEAD_STEPS = 8

assert X_HEAD_DIM == CH and N_MEM == CH and D_MODEL // N_POOL == CH


def _rms_norm(x, gain):
    ms = jnp.mean(x * x, axis=-1, keepdims=True)
    return (x * lax.rsqrt(ms + EPS)) * gain


def _gain_rows(gain_blocks_ref, i, width):
    g = gain_blocks_ref[pl.ds(i, 1), :]
    col = jnp.broadcast_to(g, (LANES, LANES)).T
    return jnp.tile(col, (1, width // LANES))


def _sigmoid(x):
    return 1.0 / (1.0 + jnp.exp2(x * NEG_LOG2_E))


def _dot(a, b):
    return jnp.dot(a, b, preferred_element_type=F32)


def _dot_bt(a, b):
    return lax.dot_general(a, b, (((1,), (1,)), ((), ())), preferred_element_type=F32)


def _cols(j):
    return slice(j * CH, (j + 1) * CH)


def _prep_kernel(mem_ref, gain_ref, gmix_ref, wkv_hbm, win_hbm, wxo_hbm, sm_ref, vo_ref,
                 wkv_f32, wq_f32, wxo_f32, wkv_buf, wq_buf, wxo_buf, k_buf, v_buf, sems):
    heads = range(X_HEADS)
    scale = X_HEAD_DIM ** -0.5 * -NEG_LOG2_E

    @pl.when(pl.program_id(0) == 0)
    def _():
        copies = (
            pltpu.make_async_copy(wkv_hbm, wkv_f32, sems.at[0]),
            pltpu.make_async_copy(win_hbm.at[:, pl.ds(SRC_Q_X * D_MODEL, D_MODEL)], wq_f32, sems.at[1]),
            pltpu.make_async_copy(wxo_hbm, wxo_f32, sems.at[2]),
        )
        for copy in copies:
            copy.start()
        copies[0].wait()
        wkv_buf[...] = wkv_f32[...].astype(BF16)
        for b in range(mem_ref.shape[0]):
            mn = _rms_norm(mem_ref[b], gain_ref[...]).astype(BF16)
            for h in heads:
                k_buf[b, :, _cols(h)] = (_dot(mn, wkv_buf[:, _cols(h)]) * scale).astype(BF16)
                v_buf[b, :, _cols(h)] = _dot(mn, wkv_buf[:, D_MODEL + h * CH:D_MODEL + (h + 1) * CH]).astype(BF16)
        copies[1].wait()
        for i in range(D_MODEL // LANES):
            rows = slice(i * LANES, (i + 1) * LANES)
            wq_buf[rows, :] = (wq_f32[rows, :] * _gain_rows(gmix_ref, i, D_MODEL)).astype(BF16)
        copies[2].wait()
        wxo_buf[...] = wxo_f32[...].astype(BF16)

    b = pl.program_id(0)
    for h in heads:
        sm_ref[0, :, _cols(h)] = _dot_bt(wq_buf[:, _cols(h)], k_buf[b, :, _cols(h)]).astype(BF16)
    for h in heads:
        vo_ref[0, _cols(h), :] = _dot(v_buf[b, :, _cols(h)], wxo_buf[_cols(h), :]).astype(BF16)


def _convert_mixer_weights(i, win_ref, gmix_ref, wco_ref, wout_ref, wpool_ref, pscale_ref,
                           win_b, wco_b, wout_b):
    rb = win_ref.shape[0]
    assert rb == LANES
    rows = pl.ds(pl.multiple_of(i * rb, rb), rb)
    gain = _gain_rows(gmix_ref, i, D_MODEL)
    for dst, src in enumerate(MIX_SRC):
        if src != SRC_U_P:
            win_b[rows, dst * D_MODEL:(dst + 1) * D_MODEL] = (
                win_ref[:, src * D_MODEL:(src + 1) * D_MODEL] * gain).astype(BF16)
            continue
        for g in range(N_POOL):
            w_up = (win_ref[:, src * D_MODEL + g * CH:src * D_MODEL + (g + 1) * CH] * gain[:, :CH]).astype(BF16)
            folded = _dot(w_up, wpool_ref[g].astype(BF16)) * pscale_ref[:, _cols(g)]
            win_b[rows, dst * D_MODEL + g * CH:dst * D_MODEL + (g + 1) * CH] = folded.astype(BF16)
    wco_b[rows, :] = wco_ref[...].astype(BF16)
    wout_b[rows, :] = wout_ref[...].astype(BF16)


def _mixer_kernel(x_ref, sm_ref, vo_ref, gmix_ref, convw_ref,
                  win_f32, wco_f32, wout_f32, wpool_ref, pscale_ref,
                  wg_ref, wu_ref, wd_ref, gffn_ref,
                  o_ref, wg_o, wu_o, wd_o,
                  win_ref, wco_ref, wout_ref, h_buf, r_buf, cu_buf, up_buf, z_buf, gcol_buf, *, lead, n_t):
    s = pl.program_id(0)

    @pl.when(s < lead)
    def _():
        _convert_mixer_weights(s, win_f32, gmix_ref, wco_f32, wout_f32, wpool_ref, pscale_ref,
                               win_ref, wco_ref, wout_ref)

    @pl.when(s >= lead)
    def _():
        _mixer_tile(lax.rem(s - lead, jnp.int32(n_t)), x_ref.at[0], o_ref.at[0], sm_ref, vo_ref,
                    convw_ref, win_ref, wco_ref, wout_ref, h_buf, r_buf, cu_buf, up_buf, z_buf,
                    functools.partial(_convert_ffn_weights, s - lead, wg_ref, wu_ref, wd_ref, gffn_ref,
                                      wg_o, wu_o, wd_o, gcol_buf))


def _convert_ffn_weights(step, wg_ref, wu_ref, wd_ref, gffn_ref, wg_o, wu_o, wd_o, gcol_buf):
    rb = wg_ref.shape[0]
    per_block = LANES // rb
    gcol_buf[...] = _gain_rows(gffn_ref, step // per_block, LANES)
    off = pl.multiple_of((step % per_block) * rb, rb)
    gain = jnp.tile(gcol_buf[pl.ds(off, rb), :], (1, D_FF // LANES))
    wg_o[...] = (wg_ref[...] * gain).astype(BF16)
    wu_o[...] = (wu_ref[...] * gain).astype(BF16)
    wd_o[...] = wd_ref[...].astype(BF16)


def _mixer_tile(t, x_ref, o_ref, sm_ref, vo_ref, convw_ref, win_ref, wco_ref, wout_ref,
                h_buf, r_buf, cu_buf, up_buf, z_buf, side_work):
    tile = x_ref.shape[0]

    @pl.when(t == 0)
    def _():
        cu_buf[0:HALO, :] = jnp.zeros((HALO, D_MODEL), F32)
        up_buf[0:HALO, :] = jnp.zeros((HALO, D_MODEL), F32)

    x = x_ref[...]
    h_buf[...] = x.astype(BF16)
    r_buf[...] = jnp.broadcast_to(lax.rsqrt(jnp.mean(x * x, axis=-1, keepdims=True) + EPS), r_buf.shape)

    def proj(col0, j):
        return _dot(h_buf[...], win_ref[:, col0 + j * CH:col0 + (j + 1) * CH]) * r_buf[...]

    for j in range(N_CH):
        cu_buf[HALO:, _cols(j)] = proj(COL_C_A, j) * proj(COL_U_A, j)
    for j in range(N_CH):
        cw = convw_ref[:, _cols(j)]
        conv = cw[CONV_K - 1:CONV_K] * cu_buf[HALO:, _cols(j)]
        for k in range(CONV_K - 1):
            lag = CONV_K - 1 - k
            conv = conv + cw[k:k + 1] * cu_buf[HALO - lag:HALO - lag + tile, _cols(j)]
        z_buf[:, _cols(j)] = (proj(COL_B_A, j) * conv).astype(BF16)
    cu_buf[0:HALO, :] = cu_buf[tile:tile + HALO, :]
    for j in range(N_CH):
        y = _dot(z_buf[...], wco_ref[:, _cols(j)])
        o_ref[:, _cols(j)] = _sigmoid(proj(COL_G_A, j)) * y

    for g in range(N_POOL):
        up_buf[HALO:, _cols(g)] = proj(COL_U_P, g)
    pos = t * tile + 1 + lax.broadcasted_iota(jnp.int32, (tile, 1), 0)
    scores = []
    for g, w in enumerate(POOL_WINDOWS):
        e = up_buf[:, _cols(g)]
        s = e
        k = 1
        while k < w:
            s = s + pltpu.roll(s, k, 0)
            k *= 2
        inv_cnt = 1.0 / jnp.minimum(pos, w).astype(F32)
        y = s[HALO:] * inv_cnt - e[HALO:]
        o_ref[:, _cols(g)] += _sigmoid(proj(COL_G_P, g)) * y
        scores.append(_dot(h_buf[...], sm_ref[0, :, _cols(g)]) * r_buf[...])
    up_buf[0:HALO, :] = up_buf[tile:tile + HALO, :]

    for h in range(X_HEADS):
        s = scores[h]
        p = jnp.exp2(s - jnp.max(s, axis=-1, keepdims=True))
        inv_l = 1.0 / jnp.sum(p, axis=-1, keepdims=True)
        z_buf[:, _cols(h)] = (p * inv_l).astype(BF16)
    for j in range(N_CH):
        y = _dot(z_buf[...], vo_ref[0, :, _cols(j)])
        o_ref[:, _cols(j)] += _sigmoid(proj(COL_G_X, j)) * y

    side_work()
    h_buf[...] = o_ref[...].astype(BF16)
    for j in range(N_CH):
        o_ref[:, _cols(j)] = x_ref[:, _cols(j)] + _dot(h_buf[...], wout_ref[:, _cols(j)])


def _ffn_kernel(x_ref, wg_ref, wu_ref, wd_ref, gfin_ref, o_ref, h_buf, r_buf, a_buf):
    tile = h_buf.shape[1]
    n_tiles = x_ref.shape[0] // tile
    rows = [pl.ds(i * tile, tile) for i in range(n_tiles)]

    def norm(i):
        x = x_ref[rows[i], :]
        h_buf[i % 2] = x.astype(BF16)
        r_buf[i % 2] = jnp.broadcast_to(lax.rsqrt(jnp.mean(x * x, axis=-1, keepdims=True) + EPS),
                                        r_buf.shape[1:])

    def gate_up(i, chunks):
        for c in chunks:
            g = _dot(h_buf[i % 2], wg_ref[:, _cols(c)]) * r_buf[i % 2]
            u = _dot(h_buf[i % 2], wu_ref[:, _cols(c)]) * r_buf[i % 2]
            a_buf[i % 2, :, _cols(c)] = (g * _sigmoid(g) * u).astype(BF16)

    def down_and_final_norm(i):
        for j in range(N_CH):
            o_ref[rows[i], _cols(j)] = x_ref[rows[i], _cols(j)] + _dot(a_buf[i % 2], wd_ref[:, _cols(j)])
        o_ref[rows[i], :] = _rms_norm(o_ref[rows[i], :], gfin_ref[...])

    n_chunks = D_FF // CH
    norm(0)
    gate_up(0, range(n_chunks))
    for i in range(n_tiles):
        if i + 1 < n_tiles:
            norm(i + 1)
            gate_up(i + 1, range(FFN_EARLY_CHUNKS))
        down_and_final_norm(i)
        if i + 1 < n_tiles:
            gate_up(i + 1, range(FFN_EARLY_CHUNKS, n_chunks))


def _resident(shape, index=None):
    index = (0,) * len(shape) if index is None else index
    return pl.BlockSpec(shape, lambda *_: index, pipeline_mode=pl.Buffered(1))


def _nbytes(shape, dtype):
    n = 1
    for s in shape:
        n *= s
    return n * jnp.dtype(dtype).itemsize


def _vmem_limit(resident, streamed, scratch):
    need = sum(resident) + 2 * sum(streamed) + sum(scratch)
    assert need <= V7X_SCOPED_VMEM_BYTES, need
    return V7X_SCOPED_VMEM_BYTES


def kernel(x, mem, norm_mix, w_in, conv_w, w_conv_out, w_pool, pool_scale, norm_mem, w_kv,
           w_xattn_out, w_out, norm_ffn, w_gate, w_up, w_down, norm_final):
    batch, seq, d = x.shape
    assert d == D_MODEL and mem.shape == (batch, N_MEM, D_MODEL)
    assert w_in.shape == (1, D_MODEL, D_IN) and w_gate.shape == (1, D_MODEL, D_FF)
    tile = TOKEN_TILE
    ffn_rows = FFN_STEP_TILES * FFN_TILE
    assert seq % tile == 0 and (batch * seq) % ffn_rows == 0
    row = lambda a: a.reshape(1, -1).astype(F32)
    gain_blocks = lambda a: a.reshape(-1, LANES).astype(F32)
    sq = (D_MODEL, D_MODEL)

    prep_resident = [_nbytes((D_MODEL, 2 * D_MODEL), F32), 2 * _nbytes(sq, F32)]
    prep_streamed = [2 * _nbytes(sq, BF16)]
    prep_scratch = [_nbytes((D_MODEL, 2 * D_MODEL), BF16), 2 * _nbytes(sq, BF16),
                    _nbytes((batch, N_MEM, D_MODEL), F32), 2 * _nbytes((batch, N_MEM, D_MODEL), BF16)]
    sm, vo = pl.pallas_call(
        _prep_kernel,
        grid=(batch,),
        in_specs=[
            _resident((batch, N_MEM, D_MODEL)),
            _resident((1, D_MODEL)),
            _resident((D_MODEL // LANES, LANES)),
            pl.BlockSpec(memory_space=pl.ANY),
            pl.BlockSpec(memory_space=pl.ANY),
            pl.BlockSpec(memory_space=pl.ANY),
        ],
        out_specs=[
            pl.BlockSpec((1,) + sq, lambda b: (b, 0, 0)),
            pl.BlockSpec((1,) + sq, lambda b: (b, 0, 0)),
        ],
        out_shape=[
            jax.ShapeDtypeStruct((batch,) + sq, BF16),
            jax.ShapeDtypeStruct((batch,) + sq, BF16),
        ],
        scratch_shapes=[
            pltpu.VMEM((D_MODEL, 2 * D_MODEL), F32),
            pltpu.VMEM(sq, F32),
            pltpu.VMEM(sq, F32),
            pltpu.VMEM((D_MODEL, 2 * D_MODEL), BF16),
            pltpu.VMEM(sq, BF16),
            pltpu.VMEM(sq, BF16),
            pltpu.VMEM((batch, N_MEM, D_MODEL), BF16),
            pltpu.VMEM((batch, N_MEM, D_MODEL), BF16),
            pltpu.SemaphoreType.DMA((3,)),
        ],
        compiler_params=pltpu.CompilerParams(
            dimension_semantics=("arbitrary",),
            vmem_limit_bytes=_vmem_limit(prep_resident, prep_streamed, prep_scratch)),
        name="prep",
    )(mem, row(norm_mem[0]), gain_blocks(norm_mix[0]), w_kv[0], w_in[0], w_xattn_out[0])

    lead = LEAD_STEPS
    rb = D_MODEL // lead
    n_t = seq // tile
    n_steps = batch * n_t
    gu_rb = D_MODEL // n_steps
    d_steps = n_steps // 2
    d_rb = D_FF // d_steps
    assert rb % BF16_SUBLANES == 0 and gu_rb % BF16_SUBLANES == 0 and d_rb % BF16_SUBLANES == 0
    assert LANES % gu_rb == 0
    tok = lambda s: jnp.maximum(s - lead, 0)
    tile_block = lambda s: (tok(s) // n_t, tok(s) % n_t, 0)
    batch_block = lambda s: (tok(s) // n_t, 0, 0)
    lead_block = lambda s: (jnp.minimum(s, lead - 1), 0)
    step_block = lambda s: (tok(s), 0)
    down_block = lambda s: (jnp.minimum(tok(s), d_steps - 1), 0)
    mixer_resident = [_nbytes((N_POOL, CH, CH), F32)]
    mixer_streamed = [
        2 * _nbytes((tile, D_MODEL), F32), 2 * _nbytes(sq, BF16),
        _nbytes((rb, D_IN), F32), 2 * _nbytes((rb, D_MODEL), F32),
        2 * _nbytes((gu_rb, D_FF), F32), _nbytes((d_rb, D_MODEL), F32),
        2 * _nbytes((gu_rb, D_FF), BF16), _nbytes((d_rb, D_MODEL), BF16)]
    mixer_scratch = [
        _nbytes((D_MODEL, D_MIX), BF16), 2 * _nbytes(sq, BF16),
        2 * _nbytes((tile, D_MODEL), BF16), 2 * _nbytes((HALO + tile, D_MODEL), F32),
        _nbytes((tile, CH), F32)]
    x1, w_gate_b, w_up_b, w_down_b = pl.pallas_call(
        functools.partial(_mixer_kernel, lead=lead, n_t=n_t),
        grid=(lead + n_steps,),
        in_specs=[
            pl.BlockSpec((1, tile, D_MODEL), tile_block),
            pl.BlockSpec((1,) + sq, batch_block),
            pl.BlockSpec((1,) + sq, batch_block),
            _resident((D_MODEL // LANES, LANES)),
            _resident((CONV_K, D_MODEL)),
            pl.BlockSpec((rb, D_IN), lead_block),
            pl.BlockSpec((rb, D_MODEL), lead_block),
            pl.BlockSpec((rb, D_MODEL), lead_block),
            _resident((N_POOL, CH, CH)),
            _resident((1, D_MODEL)),
            pl.BlockSpec((gu_rb, D_FF), step_block),
            pl.BlockSpec((gu_rb, D_FF), step_block),
            pl.BlockSpec((d_rb, D_MODEL), down_block),
            _resident((D_MODEL // LANES, LANES)),
        ],
        out_specs=[
            pl.BlockSpec((1, tile, D_MODEL), tile_block),
            pl.BlockSpec((gu_rb, D_FF), step_block),
            pl.BlockSpec((gu_rb, D_FF), step_block),
            pl.BlockSpec((d_rb, D_MODEL), down_block),
        ],
        out_shape=[
            jax.ShapeDtypeStruct((batch, seq, D_MODEL), F32),
            jax.ShapeDtypeStruct((D_MODEL, D_FF), BF16),
            jax.ShapeDtypeStruct((D_MODEL, D_FF), BF16),
            jax.ShapeDtypeStruct((D_FF, D_MODEL), BF16),
        ],
        scratch_shapes=[
            pltpu.VMEM((D_MODEL, D_MIX), BF16),
            pltpu.VMEM(sq, BF16),
            pltpu.VMEM(sq, BF16),
            pltpu.VMEM((tile, D_MODEL), BF16),
            pltpu.VMEM((tile, CH), F32),
            pltpu.VMEM((HALO + tile, D_MODEL), F32),
            pltpu.VMEM((HALO + tile, D_MODEL), F32),
            pltpu.VMEM((tile, D_MODEL), BF16),
            pltpu.VMEM((LANES, LANES), F32),
        ],
        compiler_params=pltpu.CompilerParams(
            dimension_semantics=("arbitrary",),
            vmem_limit_bytes=_vmem_limit(mixer_resident, mixer_streamed, mixer_scratch)),
        name="mixer",
    )(x, sm, vo, gain_blocks(norm_mix[0]), conv_w[0].astype(F32), w_in[0], w_conv_out[0], w_out[0],
      w_pool[0], row(pool_scale[0]), w_gate[0], w_up[0], w_down[0], gain_blocks(norm_ffn[0]))

    n_tok = batch * seq
    ffn_resident = [3 * _nbytes((D_MODEL, D_FF), BF16)]
    ffn_streamed = [2 * _nbytes((ffn_rows, D_MODEL), F32)]
    ffn_scratch = [
        2 * _nbytes((FFN_TILE, D_MODEL), BF16), 2 * _nbytes((FFN_TILE, CH), F32), 2 * _nbytes((FFN_TILE, D_FF), BF16)]
    out = pl.pallas_call(
        _ffn_kernel,
        grid=(n_tok // ffn_rows,),
        in_specs=[
            pl.BlockSpec((ffn_rows, D_MODEL), lambda i: (i, 0)),
            _resident((D_MODEL, D_FF)),
            _resident((D_MODEL, D_FF)),
            _resident((D_FF, D_MODEL)),
            _resident((1, D_MODEL)),
        ],
        out_specs=pl.BlockSpec((ffn_rows, D_MODEL), lambda i: (i, 0)),
        out_shape=jax.ShapeDtypeStruct((n_tok, D_MODEL), F32),
        scratch_shapes=[
            pltpu.VMEM((2, FFN_TILE, D_MODEL), BF16),
            pltpu.VMEM((2, FFN_TILE, CH), F32),
            pltpu.VMEM((2, FFN_TILE, D_FF), BF16),
        ],
        compiler_params=pltpu.CompilerParams(
            dimension_semantics=("arbitrary",),
            vmem_limit_bytes=_vmem_limit(ffn_resident, ffn_streamed, ffn_scratch)),
        name="ffn",
    )(x1.reshape(n_tok, D_MODEL), w_gate_b, w_up_b, w_down_b, row(norm_final))
    return out.reshape(batch, seq, D_MODEL)
```

```python
import functools

import jax
import jax.numpy as jnp
from jax import lax
from jax.experimental import pallas as pl
from jax.experimental.pallas import tpu as pltpu

F32 = jnp.float32
BF16 = jnp.bfloat16

D_MODEL = 1024
N_MEM = 256
CONV_K = 3
POOL_WINDOWS = (2, 4, 8, 16)
N_POOL = len(POOL_WINDOWS)
X_HEADS = 4
X_HEAD_DIM = D_MODEL // X_HEADS
D_FF = 2816
EPS = 1e-6
NEG_LOG2_E = -1.4426950408889634

SRC_B_A, SRC_C_A, SRC_U_A, SRC_U_P, SRC_Q_X, SRC_G_A, SRC_G_P, SRC_G_X = range(8)
D_IN = 8 * D_MODEL
MIX_SRC = (SRC_B_A, SRC_C_A, SRC_U_A, SRC_U_P, SRC_G_A, SRC_G_P, SRC_G_X)
COL_B_A, COL_C_A, COL_U_A, COL_U_P, COL_G_A, COL_G_P, COL_G_X = (
    i * D_MODEL for i in range(len(MIX_SRC)))
D_MIX = len(MIX_SRC) * D_MODEL

V7X_MXU_WIDTH = 256
V7X_VMEM_BYTES = 64 * 1024 * 1024
V7X_SCOPED_VMEM_BYTES = V7X_VMEM_BYTES - 4 * 1024 * 1024
LANES = 128
BF16_SUBLANES = 16
HALO = 16
TOKEN_TILE = 512
FFN_TILE = 256
FFN_STEP_TILES = 4
FFN_EARLY_CHUNKS = 1
CH = V7X_MXU_WIDTH
N_CH = D_MODEL // CH
LEAD_STEPS = 8

assert X_HEAD_DIM == CH and N_MEM == CH and D_MODEL // N_POOL == CH


def _rms_norm(x, gain):
    ms = jnp.mean(x * x, axis=-1, keepdims=True)
    return (x * lax.rsqrt(ms + EPS)) * gain


def _gain_rows(gain_blocks_ref, i, width):
    g = gain_blocks_ref[pl.ds(i, 1), :]
    col = jnp.broadcast_to(g, (LANES, LANES)).T
    return jnp.tile(col, (1, width // LANES))


def _sigmoid(x):
    return 1.0 / (1.0 + jnp.exp2(x * NEG_LOG2_E))


def _dot(a, b):
    return jnp.dot(a, b, preferred_element_type=F32)


def _dot_bt(a, b):
    return lax.dot_general(a, b, (((1,), (1,)), ((), ())), preferred_element_type=F32)


def _cols(j):
    return slice(j * CH, (j + 1) * CH)


def _prep_kernel(mem_ref, gain_ref, wkv_ref, wq_ref, gmix_ref, wxo_ref, sm_ref, vo_ref,
                 wkv_buf, wq_buf, wxo_buf):
    @pl.when(pl.program_id(0) == 0)
    def _():
        wkv_buf[...] = wkv_ref[...].astype(BF16)
        for i in range(D_MODEL // LANES):
            rows = slice(i * LANES, (i + 1) * LANES)
            wq_buf[rows, :] = (wq_ref[rows, :] * _gain_rows(gmix_ref, i, D_MODEL)).astype(BF16)
        wxo_buf[...] = wxo_ref[...].astype(BF16)

    mn = _rms_norm(mem_ref[0], gain_ref[...]).astype(BF16)
    scale = X_HEAD_DIM ** -0.5 * -NEG_LOG2_E
    heads = range(X_HEADS)
    k = [(_dot(mn, wkv_buf[:, _cols(h)]) * scale).astype(BF16) for h in heads]
    v = [_dot(mn, wkv_buf[:, D_MODEL + h * CH:D_MODEL + (h + 1) * CH]).astype(BF16) for h in heads]
    for h in heads:
        sm_ref[0, :, _cols(h)] = _dot_bt(wq_buf[:, _cols(h)], k[h]).astype(BF16)
    for h in heads:
        vo_ref[0, _cols(h), :] = _dot(v[h], wxo_buf[_cols(h), :]).astype(BF16)


def _convert_mixer_weights(i, win_ref, gmix_ref, wco_ref, wout_ref, wpool_ref, pscale_ref,
                           win_b, wco_b, wout_b):
    rb = win_ref.shape[0]
    assert rb == LANES
    rows = pl.ds(pl.multiple_of(i * rb, rb), rb)
    gain = _gain_rows(gmix_ref, i, D_MODEL)
    for dst, src in enumerate(MIX_SRC):
        if src != SRC_U_P:
            win_b[rows, dst * D_MODEL:(dst + 1) * D_MODEL] = (
                win_ref[:, src * D_MODEL:(src + 1) * D_MODEL] * gain).astype(BF16)
            continue
        for g in range(N_POOL):
            w_up = (win_ref[:, src * D_MODEL + g * CH:src * D_MODEL + (g + 1) * CH] * gain[:, :CH]).astype(BF16)
            folded = _dot(w_up, wpool_ref[g].astype(BF16)) * pscale_ref[:, _cols(g)]
            win_b[rows, dst * D_MODEL + g * CH:dst * D_MODEL + (g + 1) * CH] = folded.astype(BF16)
    wco_b[rows, :] = wco_ref[...].astype(BF16)
    wout_b[rows, :] = wout_ref[...].astype(BF16)


def _mixer_kernel(x_ref, sm_ref, vo_ref, gmix_ref, convw_ref,
                  win_f32, wco_f32, wout_f32, wpool_ref, pscale_ref,
                  wg_ref, wu_ref, wd_ref, gffn_ref,
                  o_ref, wgu_o, wd_o,
                  win_ref, wco_ref, wout_ref, h_buf, r_buf, cu_buf, up_buf, z_buf, gcol_buf, *, lead, n_t):
    s = pl.program_id(0)

    @pl.when(s < lead)
    def _():
        _convert_mixer_weights(s, win_f32, gmix_ref, wco_f32, wout_f32, wpool_ref, pscale_ref,
                               win_ref, wco_ref, wout_ref)

    @pl.when(s >= lead)
    def _():
        _mixer_tile(lax.rem(s - lead, jnp.int32(n_t)), x_ref.at[0], o_ref.at[0], sm_ref, vo_ref,
                    convw_ref, win_ref, wco_ref, wout_ref, h_buf, r_buf, cu_buf, up_buf, z_buf,
                    functools.partial(_convert_ffn_weights, s - lead, wg_ref, wu_ref, wd_ref, gffn_ref,
                                      wgu_o.at[:, pl.ds(0, D_FF)], wgu_o.at[:, pl.ds(D_FF, D_FF)], wd_o,
                                      gcol_buf))


def _convert_ffn_weights(step, wg_ref, wu_ref, wd_ref, gffn_ref, wg_o, wu_o, wd_o, gcol_buf):
    rb = wg_ref.shape[0]
    per_block = LANES // rb
    gcol_buf[...] = _gain_rows(gffn_ref, step // per_block, LANES)
    off = pl.multiple_of((step % per_block) * rb, rb)
    gain = jnp.tile(gcol_buf[pl.ds(off, rb), :], (1, D_FF // LANES))
    wg_o[...] = (wg_ref[...] * gain).astype(BF16)
    wu_o[...] = (wu_ref[...] * gain).astype(BF16)
    wd_o[...] = wd_ref[...].astype(BF16)


def _mixer_tile(t, x_ref, o_ref, sm_ref, vo_ref, convw_ref, win_ref, wco_ref, wout_ref,
                h_buf, r_buf, cu_buf, up_buf, z_buf, side_work):
    tile = x_ref.shape[0]

    @pl.when(t == 0)
    def _():
        cu_buf[0:HALO, :] = jnp.zeros((HALO, D_MODEL), F32)
        up_buf[0:HALO, :] = jnp.zeros((HALO, D_MODEL), F32)

    x = x_ref[...]
    h_buf[...] = x.astype(BF16)
    r_buf[...] = jnp.broadcast_to(lax.rsqrt(jnp.mean(x * x, axis=-1, keepdims=True) + EPS), r_buf.shape)

    def proj(col0, j):
        return _dot(h_buf[...], win_ref[:, col0 + j * CH:col0 + (j + 1) * CH]) * r_buf[...]

    for j in range(N_CH):
        cu_buf[HALO:, _cols(j)] = proj(COL_C_A, j) * proj(COL_U_A, j)
    for j in range(N_CH):
        cw = convw_ref[:, _cols(j)]
        conv = cw[CONV_K - 1:CONV_K] * cu_buf[HALO:, _cols(j)]
        for k in range(CONV_K - 1):
            lag = CONV_K - 1 - k
            conv = conv + cw[k:k + 1] * cu_buf[HALO - lag:HALO - lag + tile, _cols(j)]
        z_buf[:, _cols(j)] = (proj(COL_B_A, j) * conv).astype(BF16)
    cu_buf[0:HALO, :] = cu_buf[tile:tile + HALO, :]
    for j in range(N_CH):
        y = _dot(z_buf[...], wco_ref[:, _cols(j)])
        o_ref[:, _cols(j)] = _sigmoid(proj(COL_G_A, j)) * y

    for g in range(N_POOL):
        up_buf[HALO:, _cols(g)] = proj(COL_U_P, g)
    pos = t * tile + 1 + lax.broadcasted_iota(jnp.int32, (tile, 1), 0)
    scores = []
    for g, w in enumerate(POOL_WINDOWS):
        e = up_buf[:, _cols(g)]
        s = e
        k = 1
        while k < w:
            s = s + pltpu.roll(s, k, 0)
            k *= 2
        inv_cnt = 1.0 / jnp.minimum(pos, w).astype(F32)
        y = s[HALO:] * inv_cnt - e[HALO:]
        o_ref[:, _cols(g)] += _sigmoid(proj(COL_G_P, g)) * y
        scores.append(_dot(h_buf[...], sm_ref[0, :, _cols(g)]) * r_buf[...])
    up_buf[0:HALO, :] = up_buf[tile:tile + HALO, :]

    for h in range(X_HEADS):
        s = scores[h]
        p = jnp.exp2(s - jnp.max(s, axis=-1, keepdims=True))
        inv_l = 1.0 / jnp.sum(p, axis=-1, keepdims=True)
        z_buf[:, _cols(h)] = (p * inv_l).astype(BF16)
    for j in range(N_CH):
        y = _dot(z_buf[...], vo_ref[0, :, _cols(j)])
        o_ref[:, _cols(j)] += _sigmoid(proj(COL_G_X, j)) * y

    side_work()
    h_buf[...] = o_ref[...].astype(BF16)
    for j in range(N_CH):
        o_ref[:, _cols(j)] = x_ref[:, _cols(j)] + _dot(h_buf[...], wout_ref[:, _cols(j)])


def _ffn_kernel(x_ref, wgu_ref, wd_ref, gfin_ref, o_ref, h_buf, r_buf, a_buf):
    tile = h_buf.shape[1]
    wg_ref = wgu_ref.at[:, pl.ds(0, D_FF)]
    wu_ref = wgu_ref.at[:, pl.ds(D_FF, D_FF)]
    n_tiles = x_ref.shape[0] // tile
    rows = [pl.ds(i * tile, tile) for i in range(n_tiles)]

    def norm(i):
        x = x_ref[rows[i], :]
        h_buf[i % 2] = x.astype(BF16)
        r_buf[i % 2] = jnp.broadcast_to(lax.rsqrt(jnp.mean(x * x, axis=-1, keepdims=True) + EPS),
                                        r_buf.shape[1:])

    def gate_up(i, chunks):
        for c in chunks:
            g = _dot(h_buf[i % 2], wg_ref[:, _cols(c)]) * r_buf[i % 2]
            u = _dot(h_buf[i % 2], wu_ref[:, _cols(c)]) * r_buf[i % 2]
            a_buf[i % 2, :, _cols(c)] = (g * _sigmoid(g) * u).astype(BF16)

    def down_and_final_norm(i):
        for j in range(N_CH):
            o_ref[rows[i], _cols(j)] = x_ref[rows[i], _cols(j)] + _dot(a_buf[i % 2], wd_ref[:, _cols(j)])
        o_ref[rows[i], :] = _rms_norm(o_ref[rows[i], :], gfin_ref[...])

    n_chunks = D_FF // CH
    norm(0)
    gate_up(0, range(n_chunks))
    for i in range(n_tiles):
        if i + 1 < n_tiles:
            norm(i + 1)
            gate_up(i + 1, range(FFN_EARLY_CHUNKS))
        down_and_final_norm(i)
        if i + 1 < n_tiles:
            gate_up(i + 1, range(FFN_EARLY_CHUNKS, n_chunks))


def _resident(shape, index=None):
    index = (0,) * len(shape) if index is None else index
    return pl.BlockSpec(shape, lambda *_: index, pipeline_mode=pl.Buffered(1))


def _nbytes(shape, dtype):
    n = 1
    for s in shape:
        n *= s
    return n * jnp.dtype(dtype).itemsize


def _vmem_limit(resident, streamed, scratch):
    need = sum(resident) + 2 * sum(streamed) + sum(scratch)
    assert need <= V7X_SCOPED_VMEM_BYTES, need
    return V7X_SCOPED_VMEM_BYTES


def kernel(x, mem, norm_mix, w_in, conv_w, w_conv_out, w_pool, pool_scale, norm_mem, w_kv,
           w_xattn_out, w_out, norm_ffn, w_gate, w_up, w_down, norm_final):
    batch, seq, d = x.shape
    assert d == D_MODEL and mem.shape == (batch, N_MEM, D_MODEL)
    assert w_in.shape == (1, D_MODEL, D_IN) and w_gate.shape == (1, D_MODEL, D_FF)
    tile = TOKEN_TILE
    ffn_rows = FFN_STEP_TILES * FFN_TILE
    assert seq % tile == 0 and (batch * seq) % ffn_rows == 0
    row = lambda a: a.reshape(1, -1).astype(F32)
    gain_blocks = lambda a: a.reshape(-1, LANES).astype(F32)
    sq = (D_MODEL, D_MODEL)

    prep_resident = [_nbytes((D_MODEL, 2 * D_MODEL), F32), 2 * _nbytes(sq, F32)]
    prep_streamed = [_nbytes((N_MEM, D_MODEL), F32), 2 * _nbytes(sq, BF16)]
    prep_scratch = [_nbytes((D_MODEL, 2 * D_MODEL), BF16), 2 * _nbytes(sq, BF16)]
    sm, vo = pl.pallas_call(
        _prep_kernel,
        grid=(batch,),
        in_specs=[
            pl.BlockSpec((1, N_MEM, D_MODEL), lambda b: (b, 0, 0)),
            _resident((1, D_MODEL)),
            _resident((D_MODEL, 2 * D_MODEL)),
            _resident(sq, (0, SRC_Q_X)),
            _resident((D_MODEL // LANES, LANES)),
            _resident(sq),
        ],
        out_specs=[
            pl.BlockSpec((1,) + sq, lambda b: (b, 0, 0)),
            pl.BlockSpec((1,) + sq, lambda b: (b, 0, 0)),
        ],
        out_shape=[
            jax.ShapeDtypeStruct((batch,) + sq, BF16),
            jax.ShapeDtypeStruct((batch,) + sq, BF16),
        ],
        scratch_shapes=[
            pltpu.VMEM((D_MODEL, 2 * D_MODEL), BF16),
            pltpu.VMEM(sq, BF16),
            pltpu.VMEM(sq, BF16),
        ],
        compiler_params=pltpu.CompilerParams(
            dimension_semantics=("arbitrary",),
            vmem_limit_bytes=_vmem_limit(prep_resident, prep_streamed, prep_scratch)),
        name="prep",
    )(mem, row(norm_mem[0]), w_kv[0], w_in[0], gain_blocks(norm_mix[0]), w_xattn_out[0])

    lead = LEAD_STEPS
    rb = D_MODEL // lead
    n_t = seq // tile
    n_steps = batch * n_t
    gu_rb = D_MODEL // n_steps
    d_steps = n_steps // 2
    d_rb = D_FF // d_steps
    assert rb % BF16_SUBLANES == 0 and gu_rb % BF16_SUBLANES == 0 and d_rb % BF16_SUBLANES == 0
    assert LANES % gu_rb == 0
    tok = lambda s: jnp.maximum(s - lead, 0)
    tile_block = lambda s: (tok(s) // n_t, tok(s) % n_t, 0)
    batch_block = lambda s: (tok(s) // n_t, 0, 0)
    lead_block = lambda s: (jnp.minimum(s, lead - 1), 0)
    step_block = lambda s: (tok(s), 0)
    down_block = lambda s: (jnp.minimum(tok(s), d_steps - 1), 0)
    mixer_resident = [_nbytes((N_POOL, CH, CH), F32)]
    mixer_streamed = [
        2 * _nbytes((tile, D_MODEL), F32), 2 * _nbytes(sq, BF16),
        _nbytes((rb, D_IN), F32), 2 * _nbytes((rb, D_MODEL), F32),
        2 * _nbytes((gu_rb, D_FF), F32), _nbytes((d_rb, D_MODEL), F32),
        2 * _nbytes((gu_rb, D_FF), BF16), _nbytes((d_rb, D_MODEL), BF16)]
    mixer_scratch = [
        _nbytes((D_MODEL, D_MIX), BF16), 2 * _nbytes(sq, BF16),
        2 * _nbytes((tile, D_MODEL), BF16), 2 * _nbytes((HALO + tile, D_MODEL), F32),
        _nbytes((tile, CH), F32)]
    x1, w_gate_up_b, w_down_b = pl.pallas_call(
        functools.partial(_mixer_kernel, lead=lead, n_t=n_t),
        grid=(lead + n_steps,),
        in_specs=[
            pl.BlockSpec((1, tile, D_MODEL), tile_block),
            pl.BlockSpec((1,) + sq, batch_block),
            pl.BlockSpec((1,) + sq, batch_block),
            _resident((D_MODEL // LANES, LANES)),
            _resident((CONV_K, D_MODEL)),
            pl.BlockSpec((rb, D_IN), lead_block),
            pl.BlockSpec((rb, D_MODEL), lead_block),
            pl.BlockSpec((rb, D_MODEL), lead_block),
            _resident((N_POOL, CH, CH)),
            _resident((1, D_MODEL)),
            pl.BlockSpec((gu_rb, D_FF), step_block),
            pl.BlockSpec((gu_rb, D_FF), step_block),
            pl.BlockSpec((d_rb, D_MODEL), down_block),
            _resident((D_MODEL // LANES, LANES)),
        ],
        out_specs=[
            pl.BlockSpec((1, tile, D_MODEL), tile_block),
            pl.BlockSpec((gu_rb, 2 * D_FF), step_block),
            pl.BlockSpec((d_rb, D_MODEL), down_block),
        ],
        out_shape=[
            jax.ShapeDtypeStruct((batch, seq, D_MODEL), F32),
            jax.ShapeDtypeStruct((D_MODEL, 2 * D_FF), BF16),
            jax.ShapeDtypeStruct((D_FF, D_MODEL), BF16),
        ],
        scratch_shapes=[
            pltpu.VMEM((D_MODEL, D_MIX), BF16),
            pltpu.VMEM(sq, BF16),
            pltpu.VMEM(sq, BF16),
            pltpu.VMEM((tile, D_MODEL), BF16),
            pltpu.VMEM((tile, CH), F32),
            pltpu.VMEM((HALO + tile, D_MODEL), F32),
            pltpu.VMEM((HALO + tile, D_MODEL), F32),
            pltpu.VMEM((tile, D_MODEL), BF16),
            pltpu.VMEM((LANES, LANES), F32),
        ],
        compiler_params=pltpu.CompilerParams(
            dimension_semantics=("arbitrary",),
            vmem_limit_bytes=_vmem_limit(mixer_resident, mixer_streamed, mixer_scratch)),
        name="mixer",
    )(x, sm, vo, gain_blocks(norm_mix[0]), conv_w[0].astype(F32), w_in[0], w_conv_out[0], w_out[0],
      w_pool[0], row(pool_scale[0]), w_gate[0], w_up[0], w_down[0], gain_blocks(norm_ffn[0]))

    n_tok = batch * seq
    ffn_resident = [3 * _nbytes((D_MODEL, D_FF), BF16)]
    ffn_streamed = [2 * _nbytes((ffn_rows, D_MODEL), F32)]
    ffn_scratch = [
        2 * _nbytes((FFN_TILE, D_MODEL), BF16), 2 * _nbytes((FFN_TILE, CH), F32), 2 * _nbytes((FFN_TILE, D_FF), BF16)]
    out = pl.pallas_call(
        _ffn_kernel,
        grid=(n_tok // ffn_rows,),
        in_specs=[
            pl.BlockSpec((ffn_rows, D_MODEL), lambda i: (i, 0)),
            _resident((D_MODEL, 2 * D_FF)),
            _resident((D_FF, D_MODEL)),
            _resident((1, D_MODEL)),
        ],
        out_specs=pl.BlockSpec((ffn_rows, D_MODEL), lambda i: (i, 0)),
        out_shape=jax.ShapeDtypeStruct((n_tok, D_MODEL), F32),
        scratch_shapes=[
            pltpu.VMEM((2, FFN_TILE, D_MODEL), BF16),
            pltpu.VMEM((2, FFN_TILE, CH), F32),
            pltpu.VMEM((2, FFN_TILE, D_FF), BF16),
        ],
        compiler_params=pltpu.CompilerParams(
            dimension_semantics=("arbitrary",),
            vmem_limit_bytes=_vmem_limit(ffn_resident, ffn_streamed, ffn_scratch)),
        name="ffn",
    )(x1.reshape(n_tok, D_MODEL), w_gate_up_b, w_down_b, row(norm_final))
    return out.reshape(batch, seq, D_MODEL)
```
